```python
import jax, jax.numpy as jnp
from jax import lax
import numpy as np

D_MODEL = 4096
BATCH = 2
SEQ = 8192
DEPTH = 2

GRID_W = 64
CTX_LEN = 256
ATTN_HEADS = 16
ATTN_KV_HEADS = 4
ATTN_GROUP = ATTN_HEADS // ATTN_KV_HEADS
HEAD_DIM = 128
ATTN_SCALE = HEAD_DIM ** -0.5
WINDOW = 128
QBLK = 128
ROPE_THETA = 10000.0
ROPE_PAIRS = HEAD_DIM // 4
MLSTM_HEADS = 8
MLSTM_QK_DIM = 128
MLSTM_V_DIM = 256
MLSTM_CHUNK = 128
KCONV = 3
ATTN_WIDTH = ATTN_HEADS * HEAD_DIM
KV_WIDTH = ATTN_KV_HEADS * HEAD_DIM
MLSTM_QK_WIDTH = MLSTM_HEADS * MLSTM_QK_DIM
MLSTM_WIDTH = MLSTM_HEADS * MLSTM_V_DIM
N_GATE_COLS = 4 * MLSTM_HEADS
COL_SIZES = (ATTN_WIDTH, KV_WIDTH, KV_WIDTH, MLSTM_QK_WIDTH, MLSTM_QK_WIDTH, MLSTM_WIDTH, MLSTM_WIDTH, N_GATE_COLS, D_MODEL, D_MODEL)
D_IN = ATTN_WIDTH + 2 * KV_WIDTH + 2 * MLSTM_QK_WIDTH + 2 * MLSTM_WIDTH + N_GATE_COLS + 2 * D_MODEL
GATE_OFFSET = ATTN_WIDTH + 2 * KV_WIDTH + 2 * MLSTM_QK_WIDTH + 2 * MLSTM_WIDTH
N_EXPERTS = 16
N_GROUPS = 4
EXPERTS_PER_GROUP = N_EXPERTS // N_GROUPS
TOP_K = 2
D_FF_EXPERT = 1024
EPS = 1e-6
NEG = -1e30

kernel_name = 'hybrid_mlstm_swa_moe_dit_trunk'


def rms_norm(x, g):
    x32 = x.astype(jnp.float32)
    y = x32 * lax.rsqrt(jnp.mean(x32 * x32, axis=-1, keepdims=True) + EPS)
    return (y * g.astype(jnp.float32)).astype(x.dtype)


def modulate(x, g, shift, scale):
    return rms_norm(x, g) * (1.0 + scale) + shift


def adaln_params(cvec, w, b):
    return jnp.split(jax.nn.silu(cvec) @ w + b, 6, axis=-1)


def axial_angles(n_lat):
    rows = n_lat // GRID_W
    inv_freq = ROPE_THETA ** (-jnp.arange(ROPE_PAIRS, dtype=jnp.float32) / ROPE_PAIRS)
    row_pos = jnp.repeat(jnp.arange(rows, dtype=jnp.float32), GRID_W)
    col_pos = jnp.tile(jnp.arange(GRID_W, dtype=jnp.float32), rows)
    return row_pos[:, None] * inv_freq, col_pos[:, None] * inv_freq


def rope_half(x, ang):
    x1, x2 = jnp.split(x, 2, axis=-1)
    cos = jnp.cos(ang)[:, None, :]
    sin = jnp.sin(ang)[:, None, :]
    return jnp.concatenate([x1 * cos - x2 * sin, x1 * sin + x2 * cos], axis=-1)


def axial_rope(x, ang_row, ang_col):
    xr, xc = jnp.split(x.astype(jnp.float32), 2, axis=-1)
    return jnp.concatenate([rope_half(xr, ang_row), rope_half(xc, ang_col)], axis=-1).astype(x.dtype)


def short_conv(t, w, b):
    y = lax.conv_general_dilated(t, w[:, None, :].astype(t.dtype), window_strides=(1,), padding='SAME',
                                 dimension_numbers=('NWC', 'WIO', 'NWC'), feature_group_count=t.shape[-1])
    return y + b.astype(t.dtype)


def mlstm_scan(q, k, v, i_pre, f_pre):
    B, T, H, _ = q.shape
    L = MLSTM_CHUNK
    nc = T // L
    f32 = jnp.float32

    def vec_chunks(t):
        return jnp.transpose(t.astype(f32).reshape(B, nc, L, H, -1), (1, 0, 3, 2, 4))

    def gate_chunks(t):
        return jnp.transpose(t.astype(f32).reshape(B, nc, L, H), (1, 0, 3, 2))

    tril = jnp.tril(jnp.ones((L, L), dtype=bool))

    def step(carry, xs):
        C, n, m = carry
        qc, kc, vc, ic, lf = xs
        b = jnp.cumsum(lf, axis=-1)
        D = jnp.where(tril, b[..., :, None] - b[..., None, :] + ic[..., None, :], NEG)
        m_inter = b + m[..., None]
        m_t = jnp.maximum(m_inter, jnp.max(D, axis=-1))
        P = jnp.exp(D - m_t[..., None])
        a = jnp.exp(m_inter - m_t)
        S = jnp.einsum('bhtd,bhsd->bhts', qc, kc) * P
        num = jnp.einsum('bhts,bhsv->bhtv', S, vc) + a[..., None] * jnp.einsum('bhtd,bhvd->bhtv', qc, C)
        qn = jnp.sum(S, axis=-1) + a * jnp.einsum('bhtd,bhd->bht', qc, n)
        h = num / jnp.maximum(jnp.abs(qn), jnp.exp(-m_t))[..., None]
        b_end = b[..., -1]
        g = b_end[..., None] - b + ic
        m_new = jnp.maximum(b_end + m, jnp.max(g, axis=-1))
        w = jnp.exp(g - m_new[..., None])
        a_end = jnp.exp(b_end + m - m_new)
        C_new = a_end[..., None, None] * C + jnp.einsum('bhsv,bhsd->bhvd', vc * w[..., None], kc)
        n_new = a_end[..., None] * n + jnp.einsum('bhs,bhsd->bhd', w, kc)
        return (C_new, n_new, m_new), h

    init = (jnp.zeros((B, H, v.shape[-1], q.shape[-1]), f32), jnp.zeros((B, H, q.shape[-1]), f32), jnp.zeros((B, H), f32))
    xs = (vec_chunks(q), vec_chunks(k), vec_chunks(v), gate_chunks(i_pre), gate_chunks(jax.nn.log_sigmoid(f_pre.astype(f32))))
    _, h = lax.scan(step, init, xs)
    return jnp.transpose(h, (1, 0, 3, 2, 4)).reshape(B, T, H, v.shape[-1])


def windowed_attention(q, k, v, k_ctx, v_ctx, sink):
    B, N = q.shape[:2]
    nb = N // QBLK
    n_band = 3 * QBLK
    n_ctx = k_ctx.shape[1]
    qb = q.reshape(B, nb, QBLK, ATTN_KV_HEADS, ATTN_GROUP, HEAD_DIM)

    def band(t):
        tp = jnp.pad(t, ((0, 0), (QBLK, QBLK), (0, 0), (0, 0))).reshape(B, nb + 2, QBLK, ATTN_KV_HEADS, HEAD_DIM)
        return jnp.concatenate([tp[:, :-2], tp[:, 1:-1], tp[:, 2:]], axis=2)

    kb, vb = band(k), band(v)
    s_loc = jnp.einsum('bnqhgd,bnkhd->bnhgqk', qb, kb).astype(jnp.float32) * ATTN_SCALE
    s_ctx = jnp.einsum('bnqhgd,bchd->bnhgqc', qb, k_ctx).astype(jnp.float32) * ATTN_SCALE
    blk = jnp.arange(nb)[:, None]
    q_pos = blk * QBLK + jnp.arange(QBLK)[None, :]
    k_pos = (blk - 1) * QBLK + jnp.arange(n_band)[None, :]
    valid = ((jnp.abs(k_pos[:, None, :] - q_pos[:, :, None]) <= WINDOW)
             & (k_pos >= 0)[:, None, :] & (k_pos < N)[:, None, :])
    s_loc = jnp.where(valid[None, :, None, None], s_loc, NEG)
    s_sink = jnp.broadcast_to(sink.astype(jnp.float32).reshape(1, 1, ATTN_KV_HEADS, ATTN_GROUP, 1, 1), s_loc.shape[:-1] + (1,))
    p = jax.nn.softmax(jnp.concatenate([s_loc, s_ctx, s_sink], axis=-1), axis=-1).astype(v.dtype)
    o = (jnp.einsum('bnhgqk,bnkhd->bnqhgd', p[..., :n_band], vb)
         + jnp.einsum('bnhgqc,bchd->bnqhgd', p[..., n_band:n_band + n_ctx], v_ctx))
    return o.reshape(B, N, ATTN_WIDTH)


def context_attention(q, k, v, sink):
    B, C = q.shape[:2]
    qg = q.reshape(B, C, ATTN_KV_HEADS, ATTN_GROUP, HEAD_DIM)
    s = jnp.einsum('bqhgd,bkhd->bhgqk', qg, k).astype(jnp.float32) * ATTN_SCALE
    s_sink = jnp.broadcast_to(sink.astype(jnp.float32).reshape(1, ATTN_KV_HEADS, ATTN_GROUP, 1, 1), s.shape[:-1] + (1,))
    p = jax.nn.softmax(jnp.concatenate([s, s_sink], axis=-1), axis=-1)[..., :C].astype(v.dtype)
    return jnp.einsum('bhgqk,bkhd->bqhgd', p, v).reshape(B, C, ATTN_WIDTH)


def moe(h, w_router, b_router, w_gate, w_up, w_down):
    aff = jax.nn.sigmoid((h @ w_router).astype(jnp.float32))
    biased = aff + b_router.astype(jnp.float32)
    grp = biased.reshape(biased.shape[:-1] + (N_GROUPS, EXPERTS_PER_GROUP))
    grp_score = jnp.sum(lax.top_k(grp, TOP_K)[0], axis=-1)
    best = jnp.argmax(grp_score, axis=-1)
    in_group = (jnp.arange(N_EXPERTS) // EXPERTS_PER_GROUP) == best[..., None]
    _, idx = lax.top_k(jnp.where(in_group, biased, NEG), TOP_K)
    wts = jnp.take_along_axis(aff, idx, axis=-1)
    wts = wts / jnp.sum(wts, axis=-1, keepdims=True)
    combine = jnp.sum(jax.nn.one_hot(idx, N_EXPERTS, dtype=jnp.float32) * wts[..., None], axis=-2).astype(h.dtype)
    y = jnp.zeros_like(h)
    for e in range(N_EXPERTS):
        act = jax.nn.silu(h @ w_gate[e]) * (h @ w_up[e])
        y = y + combine[..., e:e + 1] * (act @ w_down[e])
    return y


def mixer(hc, hl, ang_row, ang_col, w_in, b_in, g_q, g_k, sink, conv_w, conv_b, g_mh,
          w_br_attn, w_br_mlstm, w_out, need_ctx):
    B, n_ctx = hc.shape[:2]
    proj = jnp.concatenate([hc, hl], axis=1) @ w_in + b_in
    T = proj.shape[1]
    splits = np.cumsum(COL_SIZES)[:-1].tolist()
    a_q, a_k, a_v, m_q, m_k, m_v, m_o, m_gates, gate_attn, gate_mlstm = jnp.split(proj, splits, axis=-1)

    a_q = rms_norm(a_q.reshape(B, T, ATTN_HEADS, HEAD_DIM), g_q)
    a_k = rms_norm(a_k.reshape(B, T, ATTN_KV_HEADS, HEAD_DIM), g_k)
    a_v = a_v.reshape(B, T, ATTN_KV_HEADS, HEAD_DIM)
    q_lat = axial_rope(a_q[:, n_ctx:], ang_row, ang_col)
    k_lat = axial_rope(a_k[:, n_ctx:], ang_row, ang_col)
    k_ctx, v_ctx = a_k[:, :n_ctx], a_v[:, :n_ctx]
    attn_lat = windowed_attention(q_lat, k_lat, a_v[:, n_ctx:], k_ctx, v_ctx, sink)

    qk = jnp.concatenate([m_q, m_k], axis=-1)
    qk = jax.nn.silu(jnp.concatenate([short_conv(qk[:, :n_ctx], conv_w, conv_b),
                                      short_conv(qk[:, n_ctx:], conv_w, conv_b)], axis=1))
    m_q, m_k = jnp.split(qk, 2, axis=-1)
    m_q = m_q.reshape(B, T, MLSTM_HEADS, MLSTM_QK_DIM)
    m_k = m_k.reshape(B, T, MLSTM_HEADS, MLSTM_QK_DIM) * (MLSTM_QK_DIM ** -0.5)
    m_v = m_v.reshape(B, T, MLSTM_HEADS, MLSTM_V_DIM)
    i_fwd, f_fwd, i_bwd, f_bwd = jnp.split(m_gates, 4, axis=-1)

    def rev(t):
        return jnp.concatenate([jnp.flip(t[:, :n_ctx], axis=1), jnp.flip(t[:, n_ctx:], axis=1)], axis=1)

    h_fwd = mlstm_scan(m_q, m_k, m_v, i_fwd, f_fwd)
    h_bwd = rev(mlstm_scan(rev(m_q), rev(m_k), rev(m_v), rev(i_bwd), rev(f_bwd)))
    h_m = jax.nn.sigmoid(m_o.astype(jnp.float32)).reshape(B, T, MLSTM_HEADS, MLSTM_V_DIM) * (h_fwd + h_bwd)
    h_m = rms_norm(h_m, g_mh.reshape(MLSTM_HEADS, MLSTM_V_DIM)).reshape(B, T, MLSTM_WIDTH).astype(proj.dtype)

    def merge(attn, hm, ga, gm):
        return (jax.nn.sigmoid(ga) * (attn @ w_br_attn) + jax.nn.sigmoid(gm) * (hm @ w_br_mlstm)) @ w_out

    out_lat = merge(attn_lat, h_m[:, n_ctx:], gate_attn[:, n_ctx:], gate_mlstm[:, n_ctx:])
    if not need_ctx:
        return out_lat, None
    attn_ctx = context_attention(a_q[:, :n_ctx], k_ctx, v_ctx, sink)
    out_ctx = merge(attn_ctx, h_m[:, :n_ctx], gate_attn[:, :n_ctx], gate_mlstm[:, :n_ctx])
    return out_lat, out_ctx


def setup_inputs(seed: int = 0) -> dict:
    key = jax.random.key(seed)
    ks = jax.random.split(key, 24)
    nrm = jax.random.normal
    f_bias = jnp.linspace(3.0, 6.0, MLSTM_HEADS)
    b_in = 0.02 * nrm(ks[9], (DEPTH, D_IN))
    b_in = b_in.at[:, GATE_OFFSET + MLSTM_HEADS:GATE_OFFSET + 2 * MLSTM_HEADS].add(f_bias)
    b_in = b_in.at[:, GATE_OFFSET + 3 * MLSTM_HEADS:GATE_OFFSET + 4 * MLSTM_HEADS].add(f_bias)
    return {
        'x': nrm(ks[0], (BATCH, SEQ, D_MODEL)),
        'c': nrm(ks[1], (BATCH, D_MODEL)),
        'ctx': nrm(ks[2], (BATCH, CTX_LEN, D_MODEL)),
        'c_ctx': nrm(ks[3], (D_MODEL,)),
        'w_ada': nrm(ks[4], (DEPTH, D_MODEL, 6 * D_MODEL)) * (0.5 * D_MODEL ** -0.5),
        'b_ada': 0.01 * nrm(ks[5], (DEPTH, 6 * D_MODEL)),
        'g_mix': 1.0 + 0.02 * nrm(ks[6], (DEPTH, D_MODEL)),
        'g_ffn': 1.0 + 0.02 * nrm(ks[7], (DEPTH, D_MODEL)),
        'w_in': nrm(ks[8], (DEPTH, D_MODEL, D_IN)) * D_MODEL ** -0.5,
        'b_in': b_in,
        'g_q': 1.0 + 0.02 * nrm(ks[10], (DEPTH, HEAD_DIM)),
        'g_k': 1.0 + 0.02 * nrm(ks[11], (DEPTH, HEAD_DIM)),
        'sink': 0.5 * nrm(ks[12], (DEPTH, ATTN_HEADS)),
        'conv_w': nrm(ks[13], (DEPTH, KCONV, 2 * MLSTM_QK_WIDTH)) * KCONV ** -0.5,
        'conv_b': 0.02 * nrm(ks[14], (DEPTH, 2 * MLSTM_QK_WIDTH)),
        'g_mh': 1.0 + 0.02 * nrm(ks[15], (DEPTH, MLSTM_WIDTH)),
        'w_br_attn': nrm(ks[16], (DEPTH, ATTN_WIDTH, D_MODEL)) * ATTN_WIDTH ** -0.5,
        'w_br_mlstm': nrm(ks[17], (DEPTH, MLSTM_WIDTH, D_MODEL)) * MLSTM_WIDTH ** -0.5,
        'w_out': nrm(ks[18], (DEPTH, D_MODEL, D_MODEL)) * D_MODEL ** -0.5,
        'w_router': nrm(ks[19], (D_MODEL, N_EXPERTS)) * D_MODEL ** -0.5,
        'b_router': 0.01 * nrm(ks[20], (N_EXPERTS,)),
        'w_gate': nrm(ks[21], (DEPTH, N_EXPERTS, D_MODEL, D_FF_EXPERT)) * D_MODEL ** -0.5,
        'w_up': nrm(ks[22], (DEPTH, N_EXPERTS, D_MODEL, D_FF_EXPERT)) * D_MODEL ** -0.5,
        'w_down': nrm(ks[23], (DEPTH, N_EXPERTS, D_FF_EXPERT, D_MODEL)) * D_FF_EXPERT ** -0.5,
    }


def reference(x, c, ctx, c_ctx, w_ada, b_ada, g_mix, g_ffn, w_in, b_in, g_q, g_k, sink, conv_w, conv_b,
              g_mh, w_br_attn, w_br_mlstm, w_out, w_router, b_router, w_gate, w_up, w_down):
    n_lat = x.shape[1]
    n_ctx = ctx.shape[1]
    ang_row, ang_col = axial_angles(n_lat)
    for l in range(DEPTH):
        need_ctx = l < DEPTH - 1
        sh_m, sc_m, gt_m, sh_f, sc_f, gt_f = adaln_params(c, w_ada[l], b_ada[l])
        csh_m, csc_m, cgt_m, csh_f, csc_f, cgt_f = adaln_params(c_ctx, w_ada[l], b_ada[l])
        hl = modulate(x, g_mix[l], sh_m[:, None], sc_m[:, None])
        hc = modulate(ctx, g_mix[l], csh_m, csc_m)
        out_lat, out_ctx = mixer(hc, hl, ang_row, ang_col, w_in[l], b_in[l], g_q[l], g_k[l], sink[l],
                                 conv_w[l], conv_b[l], g_mh[l], w_br_attn[l], w_br_mlstm[l], w_out[l], need_ctx)
        x = x + gt_m[:, None] * out_lat
        hl = modulate(x, g_ffn[l], sh_f[:, None], sc_f[:, None])
        if need_ctx:
            ctx = ctx + cgt_m * out_ctx
            hc = modulate(ctx, g_ffn[l], csh_f, csc_f)
            y = moe(jnp.concatenate([hc, hl], axis=1), w_router, b_router, w_gate[l], w_up[l], w_down[l])
            ctx = ctx + cgt_f * y[:, :n_ctx]
            x = x + gt_f[:, None] * y[:, n_ctx:]
        else:
            x = x + gt_f[:, None] * moe(hl, w_router, b_router, w_gate[l], w_up[l], w_down[l])
    return x
```

```python
import functools

import jax
import jax.numpy as jnp
from jax import lax
from jax.experimental import pallas as pl
from jax.experimental.pallas import tpu as pltpu

GRID_W = 64
HEAD_DIM = 128
WINDOW = 128
QBLK = 128
ROPE_THETA = 10000.0
ROPE_PAIRS = HEAD_DIM // 4
ATTN_SCALE = HEAD_DIM ** -0.5
MLSTM_QK_DIM = 128
MLSTM_V_DIM = 256
MLSTM_CHUNK = 128
N_GROUPS = 4
EXPERTS_PER_GROUP = 4
EPS = 1e-6
NEG = -1e30

LANES = 128
MOD_ROWS = 8
ROW_CHUNK = 64
VMEM_LIMIT = 56 << 20

F32 = jnp.float32
BF16 = jnp.bfloat16


def _pick(n, cands):
    for c in cands:
        if n % c == 0:
            return c
    raise ValueError(f"no tile in {cands} divides {n}")


def _cparams(sem, vmem=VMEM_LIMIT):
    return pltpu.CompilerParams(dimension_semantics=sem, vmem_limit_bytes=vmem)


def _seg_of_block(i, n_lat_blocks, blocks_per_batch, n_batch):
    return jnp.where(i < n_lat_blocks, i // blocks_per_batch, n_batch)


def _modulated(x, g, sc, sh):
    ms = jnp.mean(x * x, axis=-1, keepdims=True)
    y = x * lax.rsqrt(ms + EPS) * g
    return y * (1.0 + sc) + sh


def _adaln_kernel(c_ref, w_ref, b_ref, o_ref):
    c = c_ref[...]
    cs = (c * jax.nn.sigmoid(c)).astype(BF16)
    o_ref[...] = jnp.dot(cs, w_ref[...].astype(BF16), preferred_element_type=F32) + b_ref[...]


def _adaln(cvec, w_ada, b_ada):
    depth, d, n6 = w_ada.shape
    tn = _pick(n6, (512, 256, 128))
    return pl.pallas_call(
        _adaln_kernel,
        grid=(depth, n6 // tn),
        in_specs=[
            pl.BlockSpec((MOD_ROWS, d), lambda l, j: (0, 0)),
            pl.BlockSpec((None, d, tn), lambda l, j: (l, 0, j)),
            pl.BlockSpec((None, 1, tn), lambda l, j: (l, 0, j)),
        ],
        out_specs=pl.BlockSpec((None, MOD_ROWS, tn), lambda l, j: (l, 0, j)),
        out_shape=jax.ShapeDtypeStruct((depth, MOD_ROWS, n6), F32),
        compiler_params=_cparams(("arbitrary", "arbitrary")),
        name="adaln",
    )(cvec, w_ada, b_ada.reshape(depth, 1, n6))


def _inproj_kernel(x_ref, g_ref, sh_ref, sc_ref, w_ref, b_ref, wg_ref, bg_ref, o_ref, og_ref, h_scr, *, seg_args):
    i = pl.program_id(0)
    j = pl.program_id(1)

    @pl.when(j == 0)
    def _():
        seg = _seg_of_block(i, *seg_args)
        g = g_ref[...]
        sc = sc_ref[pl.ds(seg, 1), :]
        sh = sh_ref[pl.ds(seg, 1), :]

        def rows_body(r, carry):
            rs = pl.ds(pl.multiple_of(r * ROW_CHUNK, ROW_CHUNK), ROW_CHUNK)
            h_scr[rs, :] = _modulated(x_ref[rs, :], g, sc, sh).astype(BF16)
            return carry

        lax.fori_loop(0, x_ref.shape[0] // ROW_CHUNK, rows_body, 0)
        for d in range(2):
            og_ref[d] = jnp.dot(h_scr[...], wg_ref[d], preferred_element_type=F32) + bg_ref[d]

    o_ref[...] = (jnp.dot(h_scr[...], w_ref[...], preferred_element_type=F32) + b_ref[...]).astype(o_ref.dtype)


def _inproj(xs, g, mod, layer, w_main, b_main, w_gates, b_gates, tm, seg_args):
    t, d = xs.shape
    nc = w_main.shape[1]
    tn = _pick(nc, (1024, 512, 256, 128))
    return pl.pallas_call(
        functools.partial(_inproj_kernel, seg_args=seg_args),
        grid=(t // tm, nc // tn),
        in_specs=[
            pl.BlockSpec((tm, d), lambda i, j: (i, 0)),
            pl.BlockSpec((1, d), lambda i, j: (0, 0)),
            pl.BlockSpec((None, MOD_ROWS, d), lambda i, j: (layer, 0, 0)),
            pl.BlockSpec((None, MOD_ROWS, d), lambda i, j: (layer, 0, 1)),
            pl.BlockSpec((d, tn), lambda i, j: (0, j)),
            pl.BlockSpec((1, tn), lambda i, j: (0, j)),
            pl.BlockSpec((2, d, LANES), lambda i, j: (0, 0, 0)),
            pl.BlockSpec((2, 1, LANES), lambda i, j: (0, 0, 0)),
        ],
        out_specs=[
            pl.BlockSpec((tm, tn), lambda i, j: (i, j)),
            pl.BlockSpec((2, tm, LANES), lambda i, j: (0, i, 0)),
        ],
        out_shape=[
            jax.ShapeDtypeStruct((t, nc), BF16),
            jax.ShapeDtypeStruct((2, t, LANES), F32),
        ],
        scratch_shapes=[pltpu.VMEM((tm, d), BF16)],
        compiler_params=_cparams(("arbitrary", "arbitrary")),
        name="inproj",
    )(xs, g, mod, mod, w_main, b_main, w_gates, b_gates)


CONV_ROWS = 256
HALO_ROWS = 16


def _conv_kernel(cur_ref, prev_ref, next_ref, w_ref, b_ref, o_ref, *, n_lat_rows, lat_len, ctx_len, k_col_block):
    i = pl.program_id(0)
    j = pl.program_id(1)
    row0 = i * CONV_ROWS
    in_lat = row0 < n_lat_rows
    seg_len = jnp.where(in_lat, lat_len, ctx_len)
    off = jnp.where(in_lat, row0, row0 - n_lat_rows) % seg_len
    has_prev = (off != 0).astype(F32)
    has_next = (off + CONV_ROWS != seg_len).astype(F32)

    x = cur_ref[...].astype(F32)
    prev_row = prev_ref[HALO_ROWS - 1:HALO_ROWS, :].astype(F32) * has_prev
    next_row = next_ref[0:1, :].astype(F32) * has_next
    rows = lax.broadcasted_iota(jnp.int32, x.shape, 0)
    xm1 = jnp.where(rows == 0, prev_row, pltpu.roll(x, 1, 0))
    xp1 = jnp.where(rows == CONV_ROWS - 1, next_row, pltpu.roll(x, CONV_ROWS - 1, 0))
    w = w_ref[...]
    y = w[0:1, :] * xm1 + w[1:2, :] * x + w[2:3, :] * xp1 + b_ref[...]
    y = y * jax.nn.sigmoid(y)
    scale = jnp.where(j >= k_col_block, MLSTM_QK_DIM ** -0.5, 1.0).astype(F32)
    o_ref[...] = (y * scale).astype(o_ref.dtype)


def _conv(proj, conv_w, conv_b, qk_off, n_lat_rows, lat_len, ctx_len):
    t = proj.shape[0]
    width = conv_w.shape[1]
    tc = _pick(width // 2, (512, 256, 128))
    assert qk_off % tc == 0 and lat_len % CONV_ROWS == 0 and ctx_len % CONV_ROWS == 0
    cb = qk_off // tc
    halo_per_blk = CONV_ROWS // HALO_ROWS
    n_halo = t // HALO_ROWS
    return pl.pallas_call(
        functools.partial(_conv_kernel, n_lat_rows=n_lat_rows, lat_len=lat_len, ctx_len=ctx_len,
                          k_col_block=(width // 2) // tc),
        grid=(t // CONV_ROWS, width // tc),
        in_specs=[
            pl.BlockSpec((CONV_ROWS, tc), lambda i, j: (i, cb + j)),
            pl.BlockSpec((HALO_ROWS, tc), lambda i, j: (jnp.maximum(i * halo_per_blk - 1, 0), cb + j)),
            pl.BlockSpec((HALO_ROWS, tc), lambda i, j: (jnp.minimum((i + 1) * halo_per_blk, n_halo - 1), cb + j)),
            pl.BlockSpec((3, tc), lambda i, j: (0, j)),
            pl.BlockSpec((1, tc), lambda i, j: (0, j)),
        ],
        out_specs=pl.BlockSpec((CONV_ROWS, tc), lambda i, j: (i, j)),
        out_shape=jax.ShapeDtypeStruct((t, width), BF16),
        compiler_params=_cparams(("arbitrary", "arbitrary")),
        name="qk_conv",
    )(proj, proj, proj, conv_w, conv_b.reshape(1, width))


ROPE_ROWS = 256


def _rope_kernel(q_ref, k_ref, cos_ref, s1_ref, s2_ref, gq_ref, gk_ref, qo_ref, ko_ref, *, n_q_heads, n_k_heads):
    cos = cos_ref[...]
    s1 = s1_ref[...]
    s2 = s2_ref[...]

    def prep(x, g):
        x = x.astype(F32)
        xn = x * lax.rsqrt(jnp.mean(x * x, axis=-1, keepdims=True) + EPS) * g
        return xn * cos + pltpu.roll(xn, HEAD_DIM - ROPE_PAIRS, 1) * s1 + pltpu.roll(xn, ROPE_PAIRS, 1) * s2

    gq = gq_ref[...]
    gk = gk_ref[...]
    for h in range(n_q_heads):
        sl = slice(h * HEAD_DIM, (h + 1) * HEAD_DIM)
        qo_ref[:, sl] = (prep(q_ref[:, sl], gq) * ATTN_SCALE).astype(qo_ref.dtype)
    for h in range(n_k_heads):
        sl = slice(h * HEAD_DIM, (h + 1) * HEAD_DIM)
        ko_ref[:, sl] = prep(k_ref[:, sl], gk).astype(ko_ref.dtype)


def _rope(proj, tabs, g_q, g_k, aw, kvw, k_off, n_lat_rows):
    t = proj.shape[0]
    assert k_off % kvw == 0
    kb = k_off // kvw
    n_lat_blk = n_lat_rows // ROPE_ROWS
    lat_blk_per_batch = (tabs[0].shape[0] - ROPE_ROWS) // ROPE_ROWS

    def tab_map(i):
        return (jnp.where(i < n_lat_blk, i % lat_blk_per_batch, lat_blk_per_batch), 0)

    tab_spec = pl.BlockSpec((ROPE_ROWS, HEAD_DIM), tab_map)
    return pl.pallas_call(
        functools.partial(_rope_kernel, n_q_heads=aw // HEAD_DIM, n_k_heads=kvw // HEAD_DIM),
        grid=(t // ROPE_ROWS,),
        in_specs=[
            pl.BlockSpec((ROPE_ROWS, aw), lambda i: (i, 0)),
            pl.BlockSpec((ROPE_ROWS, kvw), lambda i: (i, kb)),
            tab_spec, tab_spec, tab_spec,
            pl.BlockSpec((1, HEAD_DIM), lambda i: (0, 0)),
            pl.BlockSpec((1, HEAD_DIM), lambda i: (0, 0)),
        ],
        out_specs=[
            pl.BlockSpec((ROPE_ROWS, aw), lambda i: (i, 0)),
            pl.BlockSpec((ROPE_ROWS, kvw), lambda i: (i, 0)),
        ],
        out_shape=[jax.ShapeDtypeStruct((t, aw), BF16), jax.ShapeDtypeStruct((t, kvw), BF16)],
        compiler_params=_cparams(("arbitrary",)),
        name="qk_norm_rope",
    )(proj, proj, tabs[0], tabs[1], tabs[2], g_q.reshape(1, HEAD_DIM), g_k.reshape(1, HEAD_DIM))


def _rope_tables(n_lat):
    rows = n_lat // GRID_W
    inv_freq = ROPE_THETA ** (-jnp.arange(ROPE_PAIRS, dtype=F32) / ROPE_PAIRS)
    row_pos = jnp.repeat(jnp.arange(rows, dtype=F32), GRID_W)
    col_pos = jnp.tile(jnp.arange(GRID_W, dtype=F32), rows)
    ang_r = row_pos[:, None] * inv_freq
    ang_c = col_pos[:, None] * inv_freq
    zeros = jnp.zeros_like(ang_r)
    cos = jnp.concatenate([jnp.cos(ang_r), jnp.cos(ang_r), jnp.cos(ang_c), jnp.cos(ang_c)], axis=-1)
    s1 = jnp.concatenate([-jnp.sin(ang_r), zeros, -jnp.sin(ang_c), zeros], axis=-1)
    s2 = jnp.concatenate([zeros, jnp.sin(ang_r), zeros, jnp.sin(ang_c)], axis=-1)
    ident = jnp.ones((ROPE_ROWS, HEAD_DIM), F32)
    zpad = jnp.zeros((ROPE_ROWS, HEAD_DIM), F32)
    return (jnp.concatenate([cos, ident], 0), jnp.concatenate([s1, zpad], 0), jnp.concatenate([s2, zpad], 0))


def _attn_kernel(sink_ref, q_ref, kp_ref, kc_ref, kn_ref, kx_ref, vp_ref, vc_ref, vn_ref, vx_ref, o_ref,
                 *, n_lat_blk, n_kv, group, ctx_len):
    n = pl.program_id(1)
    is_ctx = n >= n_lat_blk
    n_band = 3 * QBLK
    n_keys = n_band + ctx_len
    qi = lax.broadcasted_iota(jnp.int32, (QBLK, n_keys), 0)
    kj = lax.broadcasted_iota(jnp.int32, (QBLK, n_keys), 1)
    rel = kj - QBLK - qi
    kpos = n * QBLK + kj - QBLK
    band_ok = (jnp.abs(rel) <= WINDOW) & (kpos >= 0) & (kpos < n_lat_blk * QBLK) & jnp.logical_not(is_ctx)
    valid = band_ok | (kj >= n_band)

    for hk in range(n_kv):
        ksl = slice(hk * HEAD_DIM, (hk + 1) * HEAD_DIM)
        k_all = jnp.concatenate([kp_ref[:, ksl], kc_ref[:, ksl], kn_ref[:, ksl], kx_ref[:, ksl]], axis=0)
        v_all = jnp.concatenate([vp_ref[:, ksl], vc_ref[:, ksl], vn_ref[:, ksl], vx_ref[:, ksl]], axis=0)
        for g in range(group):
            h = hk * group + g
            qsl = slice(h * HEAD_DIM, (h + 1) * HEAD_DIM)
            s = lax.dot_general(q_ref[:, qsl], k_all, (((1,), (1,)), ((), ())), preferred_element_type=F32)
            s = jnp.where(valid, s, NEG)
            sink = sink_ref[h]
            m = jnp.maximum(jnp.max(s, axis=-1, keepdims=True), sink)
            p = jnp.exp(s - m)
            denom = jnp.sum(p, axis=-1, keepdims=True) + jnp.exp(sink - m)
            o = jnp.dot(p.astype(BF16), v_all, preferred_element_type=F32)
            o_ref[:, qsl] = (o / denom).astype(o_ref.dtype)


def _attention(sink, qr, kr, proj, aw, kvw, v_off, n_batch, lat_len, ctx_len, with_ctx):
    t = proj.shape[0]
    assert v_off % kvw == 0
    vb = v_off // kvw
    n_lat_blk = lat_len // QBLK
    n_ctx_blk = ctx_len // QBLK
    n_lat_rows = n_batch * lat_len
    nblk = n_lat_blk + (n_ctx_blk if with_ctx else 0)

    def qrow(b, n):
        return jnp.where(n < n_lat_blk, b * n_lat_blk + n, n_lat_rows // QBLK + b * n_ctx_blk + (n - n_lat_blk))

    def band(delta):
        def f(b, n):
            nn = jnp.clip(jnp.minimum(n, n_lat_blk - 1) + delta, 0, n_lat_blk - 1)
            return b * n_lat_blk + nn
        return f

    def ctx_row(b, n):
        return n_lat_rows // ctx_len + b

    def kspec(rowf):
        return pl.BlockSpec((QBLK, kvw), lambda b, n: (rowf(b, n), 0))

    def vspec(rowf):
        return pl.BlockSpec((QBLK, kvw), lambda b, n: (rowf(b, n), vb))

    return pl.pallas_call(
        functools.partial(_attn_kernel, n_lat_blk=n_lat_blk, n_kv=kvw // HEAD_DIM,
                          group=aw // kvw, ctx_len=ctx_len),
        grid=(n_batch, nblk),
        in_specs=[
            pl.BlockSpec(memory_space=pltpu.SMEM),
            pl.BlockSpec((QBLK, aw), lambda b, n: (qrow(b, n), 0)),
            kspec(band(-1)), kspec(band(0)), kspec(band(1)),
            pl.BlockSpec((ctx_len, kvw), lambda b, n: (ctx_row(b, n), 0)),
            vspec(band(-1)), vspec(band(0)), vspec(band(1)),
            pl.BlockSpec((ctx_len, kvw), lambda b, n: (ctx_row(b, n), vb)),
        ],
        out_specs=pl.BlockSpec((QBLK, aw), lambda b, n: (qrow(b, n), 0)),
        out_shape=jax.ShapeDtypeStruct((n_lat_rows + (n_batch * ctx_len if with_ctx else 0), aw), BF16),
        compiler_params=_cparams(("arbitrary", "arbitrary")),
        name="attention",
    )(sink, qr, kr, kr, kr, kr, proj, proj, proj, proj)


def _mlstm_kernel(q_ref, k_ref, v_ref, g_ref, o_ref, c_scr, nm_scr, *, n_heads):
    d = pl.program_id(0)
    c = pl.program_id(2)
    L = MLSTM_CHUNK
    dk = MLSTM_QK_DIM
    dv = MLSTM_V_DIM

    @pl.when(c == 0)
    def _():
        c_scr[...] = jnp.zeros_like(c_scr)
        nm_scr[...] = jnp.zeros_like(nm_scr)

    r = lax.broadcasted_iota(jnp.int32, (L, L), 0)
    s = lax.broadcasted_iota(jnp.int32, (L, L), 1)
    fwd = d == 0
    lag = (r - s) * (1 - 2 * d)
    tri = lag >= 0
    tri_t = lag <= 0
    tri_b = tri.astype(BF16)
    tri_tb = tri_t.astype(BF16)

    gates = g_ref[...]
    logf = jnp.minimum(gates, 0.0) - jnp.log1p(jnp.exp(-jnp.abs(gates)))
    gates_t = gates.T
    logf_t = logf.T

    def split_dot_l(mat_b, x):
        hi = x.astype(BF16)
        lo = (x - hi.astype(F32)).astype(BF16)
        return jnp.dot(mat_b, hi, preferred_element_type=F32) + jnp.dot(mat_b, lo, preferred_element_type=F32)

    def split_dot_r(x, mat_b):
        hi = x.astype(BF16)
        lo = (x - hi.astype(F32)).astype(BF16)
        return jnp.dot(hi, mat_b, preferred_element_type=F32) + jnp.dot(lo, mat_b, preferred_element_type=F32)

    cum_col = split_dot_l(tri_b, logf)
    cum_row = split_dot_r(logf_t, tri_tb)
    end_col = jnp.where(fwd, cum_col[L - 1:L, :], cum_col[0:1, :])

    for h in range(n_heads):
        ci, cf = h, n_heads + h
        q = q_ref[:, h * dk:(h + 1) * dk]
        k = k_ref[:, h * dk:(h + 1) * dk]
        v = v_ref[:, h * dv:(h + 1) * dv]
        b_col = cum_col[:, cf:cf + 1]
        b_row = cum_row[cf:cf + 1, :]
        i_col = gates[:, ci:ci + 1]
        i_row = gates_t[ci:ci + 1, :]
        b_end = end_col[:, cf:cf + 1]
        n_prev = nm_scr[h, 0:1, :]
        m_prev = nm_scr[h, 1:2, 0:1]
        ct_prev = c_scr[h]

        dmat = jnp.where(tri, b_col - b_row + i_row, NEG)
        m_inter = b_col + m_prev
        m_t = jnp.maximum(m_inter, jnp.max(dmat, axis=-1, keepdims=True))
        p = jnp.exp(dmat - m_t)
        a = jnp.exp(m_inter - m_t)
        smat = lax.dot_general(q, k, (((1,), (1,)), ((), ())), preferred_element_type=F32) * p
        num = (jnp.dot(smat.astype(BF16), v, preferred_element_type=F32)
               + a * jnp.dot(q, ct_prev.astype(BF16), preferred_element_type=F32))
        qn = jnp.sum(smat, axis=-1, keepdims=True) + a * jnp.sum(q.astype(F32) * n_prev, axis=-1, keepdims=True)
        hout = num / jnp.maximum(jnp.abs(qn), jnp.exp(-m_t))
        o_ref[:, h * dv:(h + 1) * dv] = hout.astype(o_ref.dtype)

        g_row = b_end - b_row + i_row
        g_col = b_end - b_col + i_col
        m_new = jnp.maximum(b_end + m_prev, jnp.max(g_row, axis=-1, keepdims=True))
        w_col = jnp.exp(g_col - m_new)
        a_end = jnp.exp(b_end + m_prev - m_new)
        kw = k.astype(F32) * w_col
        c_scr[h] = a_end * ct_prev + jnp.dot(kw.T.astype(BF16), v, preferred_element_type=F32)
        nm_scr[h, 0:1, :] = a_end * n_prev + jnp.sum(kw, axis=0, keepdims=True)
        nm_scr[h, 1:2, :] = jnp.broadcast_to(m_new, (1, dk))


def _mlstm(qk, proj, gates, mqk, mw, v_off, n_batch, lat_len, ctx_len):
    t = proj.shape[0]
    L = MLSTM_CHUNK
    n_heads = mw // MLSTM_V_DIM
    assert v_off % mw == 0 and mqk == n_heads * MLSTM_QK_DIM
    vb = v_off // mw
    n_ctx = ctx_len // L
    n_lat = lat_len // L
    lat_blocks = n_batch * n_lat

    def row(d, b, c):
        cc = jnp.where(d == 0, c, n_ctx - 1 - c)
        lc = jnp.where(d == 0, c - n_ctx, n_lat - 1 - (c - n_ctx))
        return jnp.where(c < n_ctx, lat_blocks + b * n_ctx + cc, b * n_lat + lc)

    return pl.pallas_call(
        functools.partial(_mlstm_kernel, n_heads=n_heads),
        grid=(2, n_batch, n_ctx + n_lat),
        in_specs=[
            pl.BlockSpec((L, mqk), lambda d, b, c: (row(d, b, c), 0)),
            pl.BlockSpec((L, mqk), lambda d, b, c: (row(d, b, c), 1)),
            pl.BlockSpec((L, mw), lambda d, b, c: (row(d, b, c), vb)),
            pl.BlockSpec((None, L, LANES), lambda d, b, c: (d, row(d, b, c), 0)),
        ],
        out_specs=pl.BlockSpec((None, L, mw), lambda d, b, c: (d, row(d, b, c), 0)),
        out_shape=jax.ShapeDtypeStruct((2, t, mw), BF16),
        scratch_shapes=[
            pltpu.VMEM((n_heads, MLSTM_QK_DIM, MLSTM_V_DIM), F32),
            pltpu.VMEM((n_heads, 8, MLSTM_QK_DIM), F32),
        ],
        compiler_params=_cparams(("arbitrary", "arbitrary", "arbitrary")),
        name="mlstm",
    )(qk, qk, proj, gates)


def _branch_kernel(attn_ref, hfb_ref, mo_ref, gmh_ref, wa_ref, wm_ref, ga_ref, gm_ref, o_ref, hm_scr, *, n_heads):
    j = pl.program_id(1)

    @pl.when(j == 0)
    def _():
        dv = MLSTM_V_DIM
        for h in range(n_heads):
            sl = slice(h * dv, (h + 1) * dv)
            hsum = hfb_ref[0, :, sl].astype(F32) + hfb_ref[1, :, sl].astype(F32)
            x = jax.nn.sigmoid(mo_ref[:, sl].astype(F32)) * hsum
            y = x * lax.rsqrt(jnp.mean(x * x, axis=-1, keepdims=True) + EPS) * gmh_ref[:, sl]
            hm_scr[:, sl] = y.astype(BF16)

    ya = jnp.dot(attn_ref[...], wa_ref[...], preferred_element_type=F32)
    ym = jnp.dot(hm_scr[...], wm_ref[...], preferred_element_type=F32)
    u = jax.nn.sigmoid(ga_ref[...].astype(F32)) * ya + jax.nn.sigmoid(gm_ref[...].astype(F32)) * ym
    o_ref[...] = u.astype(o_ref.dtype)


def _branch(attn, hfb, proj, g_mh, wa, wm, mo_off, ga_off, gm_off, rows, tm):
    aw = attn.shape[1]
    mw = hfb.shape[2]
    d = wa.shape[1]
    tn = _pick(d, (1024, 512, 256, 128))
    assert mo_off % mw == 0 and ga_off % tn == 0 and gm_off % tn == 0
    mob, gab, gmb = mo_off // mw, ga_off // tn, gm_off // tn
    return pl.pallas_call(
        functools.partial(_branch_kernel, n_heads=mw // MLSTM_V_DIM),
        grid=(rows // tm, d // tn),
        in_specs=[
            pl.BlockSpec((tm, aw), lambda i, j: (i, 0)),
            pl.BlockSpec((2, tm, mw), lambda i, j: (0, i, 0)),
            pl.BlockSpec((tm, mw), lambda i, j: (i, mob)),
            pl.BlockSpec((1, mw), lambda i, j: (0, 0)),
            pl.BlockSpec((aw, tn), lambda i, j: (0, j)),
            pl.BlockSpec((mw, tn), lambda i, j: (0, j)),
            pl.BlockSpec((tm, tn), lambda i, j: (i, gab + j)),
            pl.BlockSpec((tm, tn), lambda i, j: (i, gmb + j)),
        ],
        out_specs=pl.BlockSpec((tm, tn), lambda i, j: (i, j)),
        out_shape=jax.ShapeDtypeStruct((rows, d), BF16),
        scratch_shapes=[pltpu.VMEM((tm, mw), BF16)],
        compiler_params=_cparams(("arbitrary", "arbitrary")),
        name="branch_merge",
    )(attn, hfb, proj, g_mh.reshape(1, mw), wa, wm, proj, proj)


def _outproj_kernel(u_ref, w_ref, x_ref, gt_ref, o_ref, *, seg_args):
    seg = _seg_of_block(pl.program_id(0), *seg_args)
    y = jnp.dot(u_ref[...], w_ref[...], preferred_element_type=F32)
    o_ref[...] = x_ref[...] + gt_ref[pl.ds(seg, 1), :] * y


def _outproj(u, w_out, xs, mod, layer, tm, seg_args):
    rows, d = u.shape
    tn = _pick(d, (1024, 512, 256, 128))
    gate_blk = 2 * (d // tn)
    return pl.pallas_call(
        functools.partial(_outproj_kernel, seg_args=seg_args),
        grid=(rows // tm, d // tn),
        in_specs=[
            pl.BlockSpec((tm, d), lambda i, j: (i, 0)),
            pl.BlockSpec((d, tn), lambda i, j: (0, j)),
            pl.BlockSpec((tm, tn), lambda i, j: (i, j)),
            pl.BlockSpec((None, MOD_ROWS, tn), lambda i, j: (layer, 0, gate_blk + j)),
        ],
        out_specs=pl.BlockSpec((tm, tn), lambda i, j: (i, j)),
        out_shape=jax.ShapeDtypeStruct(xs.shape, F32),
        input_output_aliases={2: 0},
        compiler_params=_cparams(("arbitrary", "arbitrary")),
        name="outproj_residual",
    )(u, w_out, xs, mod)


def _pack_bf16_pairs(h):
    half = h.shape[1] // 2
    hi = pltpu.bitcast(h[:, :half].astype(BF16).astype(F32), jnp.uint32)
    lo = pltpu.bitcast(h[:, half:].astype(BF16).astype(F32), jnp.uint32)
    return hi | (lo >> 16)


def _unpack_bf16_pairs(p):
    hi = pltpu.bitcast(p & jnp.uint32(0xFFFF0000), F32)
    lo = pltpu.bitcast(p << 16, F32)
    return hi, lo


ROUTER_ROWS = 256


def _router_kernel(x_ref, g_ref, sh_ref, sc_ref, wr_ref, br_ref, hp_ref, idx_ref, wt_ref, *, seg_args, n_experts):
    seg = _seg_of_block(pl.program_id(0), *seg_args)
    h = _modulated(x_ref[...], g_ref[...], sc_ref[pl.ds(seg, 1), :], sh_ref[pl.ds(seg, 1), :])
    hp_ref[...] = _pack_bf16_pairs(h)

    wr = wr_ref[...]
    h_hi = h.astype(BF16)
    h_lo = (h - h_hi.astype(F32)).astype(BF16)
    w_hi = wr.astype(BF16)
    w_lo = (wr - w_hi.astype(F32)).astype(BF16)
    nt = (((1,), (1,)), ((), ()))
    logits = (lax.dot_general(w_hi, h_hi, nt, preferred_element_type=F32)
              + lax.dot_general(w_hi, h_lo, nt, preferred_element_type=F32)
              + lax.dot_general(w_lo, h_hi, nt, preferred_element_type=F32))
    aff = jax.nn.sigmoid(logits)
    biased = aff + br_ref[...]
    rb = [biased[e:e + 1, :] for e in range(n_experts)]
    ra = [aff[e:e + 1, :] for e in range(n_experts)]

    epg = EXPERTS_PER_GROUP
    scores = []
    for g in range(N_GROUPS):
        a, b, c, d = rb[epg * g:epg * g + epg]
        hi1, lo1 = jnp.maximum(a, b), jnp.minimum(a, b)
        hi2, lo2 = jnp.maximum(c, d), jnp.minimum(c, d)
        scores.append(jnp.maximum(hi1, hi2) + jnp.maximum(jnp.minimum(hi1, hi2), jnp.maximum(lo1, lo2)))
    best = jnp.zeros(scores[0].shape, jnp.int32)
    best_s = scores[0]
    for g in range(1, N_GROUPS):
        upd = scores[g] > best_s
        best = jnp.where(upd, g, best)
        best_s = jnp.where(upd, scores[g], best_s)

    vb, va = [], []
    for j in range(epg):
        xb, xa = rb[j], ra[j]
        for g in range(1, N_GROUPS):
            sel = best == g
            xb = jnp.where(sel, rb[epg * g + j], xb)
            xa = jnp.where(sel, ra[epg * g + j], xa)
        vb.append(xb)
        va.append(xa)

    i1 = jnp.zeros_like(best)
    m1, a1 = vb[0], va[0]
    for j in range(1, epg):
        upd = vb[j] > m1
        i1 = jnp.where(upd, j, i1)
        m1 = jnp.where(upd, vb[j], m1)
        a1 = jnp.where(upd, va[j], a1)
    i2 = jnp.zeros_like(best)
    m2 = jnp.full_like(m1, -jnp.inf)
    a2 = jnp.zeros_like(a1)
    for j in range(epg):
        upd = (i1 != j) & (vb[j] > m2)
        i2 = jnp.where(upd, j, i2)
        m2 = jnp.where(upd, vb[j], m2)
        a2 = jnp.where(upd, va[j], a2)

    idx_ref[0:1, :] = best * epg + i1
    idx_ref[1:2, :] = best * epg + i2
    tot = a1 + a2
    wt_ref[0:1, :] = a1 / tot
    wt_ref[1:2, :] = a2 / tot


def _router(xs, g, mod, layer, w_router_t, b_router, rows, tm, seg_args):
    d = xs.shape[1]
    e = w_router_t.shape[0]
    assert e == N_GROUPS * EXPERTS_PER_GROUP
    return pl.pallas_call(
        functools.partial(_router_kernel, seg_args=seg_args, n_experts=e),
        grid=(rows // tm,),
        in_specs=[
            pl.BlockSpec((tm, d), lambda i: (i, 0)),
            pl.BlockSpec((1, d), lambda i: (0, 0)),
            pl.BlockSpec((None, MOD_ROWS, d), lambda i: (layer, 0, 3)),
            pl.BlockSpec((None, MOD_ROWS, d), lambda i: (layer, 0, 4)),
            pl.BlockSpec((e, d), lambda i: (0, 0)),
            pl.BlockSpec((e, 1), lambda i: (0, 0)),
        ],
        out_specs=[
            pl.BlockSpec((tm, d // 2), lambda i: (i, 0)),
            pl.BlockSpec((2, tm), lambda i: (0, i)),
            pl.BlockSpec((2, tm), lambda i: (0, i)),
        ],
        out_shape=[
            jax.ShapeDtypeStruct((rows, d // 2), jnp.uint32),
            jax.ShapeDtypeStruct((2, rows), jnp.int32),
            jax.ShapeDtypeStruct((2, rows), F32),
        ],
        compiler_params=_cparams(("arbitrary",)),
        name="ffn_modulate_route",
    )(xs, g, mod, mod, w_router_t, b_router.reshape(e, 1))


GATHER_ROWS = 256


def _gather_kernel(tok_ref, src_ref, dst_ref, sem):
    base = pl.program_id(0) * GATHER_ROWS

    def row_copy(r):
        return pltpu.make_async_copy(src_ref.at[pl.ds(tok_ref[base + r], 1)], dst_ref.at[pl.ds(base + r, 1)], sem)

    def start(r, carry):
        row_copy(r).start()
        return carry

    def wait(r, carry):
        row_copy(r).wait()
        return carry

    lax.fori_loop(0, GATHER_ROWS, start, 0)
    lax.fori_loop(0, GATHER_ROWS, wait, 0)


def _gather_rows(tok_sorted, src):
    p = tok_sorted.shape[0]
    return pl.pallas_call(
        _gather_kernel,
        grid_spec=pltpu.PrefetchScalarGridSpec(
            num_scalar_prefetch=1,
            grid=(p // GATHER_ROWS,),
            in_specs=[pl.BlockSpec(memory_space=pl.ANY)],
            out_specs=pl.BlockSpec(memory_space=pl.ANY),
            scratch_shapes=[pltpu.SemaphoreType.DMA],
        ),
        out_shape=jax.ShapeDtypeStruct((p, src.shape[1]), src.dtype),
        compiler_params=_cparams(("arbitrary",)),
        name="moe_gather",
    )(tok_sorted, src)


def _expert_kernel(te_ref, nused_ref, x_ref, wg_ref, wu_ref, wd_ref, o_ref, xs_scr, acc_scr):
    i = pl.program_id(0)
    f = pl.program_id(1)

    @pl.when((i >= nused_ref[0]) & (f == 0))
    def _():
        o_ref[...] = jnp.zeros_like(o_ref)

    @pl.when(i < nused_ref[0])
    def _():
        half = x_ref.shape[1]

        @pl.when(f == 0)
        def _():
            def rows_body(r, carry):
                rs = pl.ds(pl.multiple_of(r * ROW_CHUNK, ROW_CHUNK), ROW_CHUNK)
                hi, lo = _unpack_bf16_pairs(x_ref[rs, :])
                xs_scr[rs, pl.ds(0, half)] = hi.astype(BF16)
                xs_scr[rs, pl.ds(half, half)] = lo.astype(BF16)
                return carry

            lax.fori_loop(0, x_ref.shape[0] // ROW_CHUNK, rows_body, 0)
            acc_scr[...] = jnp.zeros_like(acc_scr)

        xs = xs_scr[...]
        gate = jnp.dot(xs, wg_ref[...], preferred_element_type=F32)
        up = jnp.dot(xs, wu_ref[...], preferred_element_type=F32)
        act = (gate * jax.nn.sigmoid(gate) * up).astype(BF16)
        acc_scr[...] += jnp.dot(act, wd_ref[...], preferred_element_type=F32)

        @pl.when(f == pl.num_programs(1) - 1)
        def _():
            def rows_body(r, carry):
                rs = pl.ds(pl.multiple_of(r * ROW_CHUNK, ROW_CHUNK), ROW_CHUNK)
                o_ref[rs, :] = _pack_bf16_pairs(acc_scr[rs, :])
                return carry

            lax.fori_loop(0, o_ref.shape[0] // ROW_CHUNK, rows_body, 0)


def _experts(tile_expert, n_used, hs, wg, wu, wd, tm):
    p, half = hs.shape
    e, d, ff = wg.shape
    fc = _pick(ff, (256, 128))
    n_tiles = p // tm

    def row(i, f, te, nu):
        return (jnp.minimum(i, nu[0] - 1), 0)

    return pl.pallas_call(
        _expert_kernel,
        grid_spec=pltpu.PrefetchScalarGridSpec(
            num_scalar_prefetch=2,
            grid=(n_tiles, ff // fc),
            in_specs=[
                pl.BlockSpec((tm, half), row),
                pl.BlockSpec((None, d, fc), lambda i, f, te, nu: (te[i], 0, f)),
                pl.BlockSpec((None, d, fc), lambda i, f, te, nu: (te[i], 0, f)),
                pl.BlockSpec((None, fc, d), lambda i, f, te, nu: (te[i], f, 0)),
            ],
            out_specs=pl.BlockSpec((tm, half), lambda i, f, te, nu: (i, 0)),
            scratch_shapes=[pltpu.VMEM((tm, d), BF16), pltpu.VMEM((tm, d), F32)],
        ),
        out_shape=jax.ShapeDtypeStruct((p, half), jnp.uint32),
        compiler_params=_cparams(("arbitrary", "arbitrary")),
        name="moe_experts",
    )(tile_expert, n_used, hs, wg, wu, wd)


COMBINE_ROWS = 256


def _combine_kernel(pos_ref, x_ref, wt_ref, gt_ref, ys_ref, o_ref, buf, sem, *, seg_args, n_rows):
    i = pl.program_id(0)
    base = i * COMBINE_ROWS
    seg = _seg_of_block(i, *seg_args)

    def row_copy(k, r):
        return pltpu.make_async_copy(ys_ref.at[pl.ds(pos_ref[k * n_rows + base + r], 1)],
                                     buf.at[k, pl.ds(r, 1)], sem)

    def start(r, carry):
        row_copy(0, r).start()
        row_copy(1, r).start()
        return carry

    def wait(r, carry):
        row_copy(0, r).wait()
        row_copy(1, r).wait()
        return carry

    lax.fori_loop(0, COMBINE_ROWS, start, 0)
    lax.fori_loop(0, COMBINE_ROWS, wait, 0)

    half = buf.shape[2]
    hi0, lo0 = _unpack_bf16_pairs(buf[0])
    hi1, lo1 = _unpack_bf16_pairs(buf[1])
    w0 = wt_ref[:, 0:1]
    w1 = wt_ref[:, 1:2]
    gt = gt_ref[pl.ds(seg, 1), :]
    o_ref[:, :half] = x_ref[:, :half] + gt[:, :half] * (w0 * hi0 + w1 * hi1)
    o_ref[:, half:] = x_ref[:, half:] + gt[:, half:] * (w0 * lo0 + w1 * lo1)


def _combine(pos, xs, wts_t, mod, layer, ys, rows, seg_args):
    d = xs.shape[1]
    half = d // 2
    return pl.pallas_call(
        functools.partial(_combine_kernel, seg_args=seg_args, n_rows=rows),
        grid_spec=pltpu.PrefetchScalarGridSpec(
            num_scalar_prefetch=1,
            grid=(rows // COMBINE_ROWS,),
            in_specs=[
                pl.BlockSpec((COMBINE_ROWS, d), lambda i, pos: (i, 0)),
                pl.BlockSpec((COMBINE_ROWS, 2), lambda i, pos: (i, 0)),
                pl.BlockSpec((None, MOD_ROWS, d), lambda i, pos: (layer, 0, 5)),
                pl.BlockSpec(memory_space=pl.ANY),
            ],
            out_specs=pl.BlockSpec((COMBINE_ROWS, d), lambda i, pos: (i, 0)),
            scratch_shapes=[pltpu.VMEM((2, COMBINE_ROWS, half), jnp.uint32), pltpu.SemaphoreType.DMA],
        ),
        out_shape=jax.ShapeDtypeStruct((rows, d), F32),
        compiler_params=_cparams(("arbitrary",)),
        name="moe_combine",
    )(pos, xs, wts_t, mod, ys)


def _route_tables(idx, n_experts, tm):
    k, rows = idx.shape
    n_pairs = k * rows
    n_tiles = n_pairs // tm + n_experts
    e_flat = idx.reshape(n_pairs)
    onehot = (e_flat[:, None] == jnp.arange(n_experts, dtype=jnp.int32)[None, :]).astype(jnp.int32)
    csum = jnp.cumsum(onehot, axis=0)
    rank = jnp.sum((csum - onehot) * onehot, axis=1)
    counts = csum[-1]
    tiles_per = (counts + tm - 1) // tm
    tile_end = jnp.cumsum(tiles_per)
    row_off = (tile_end - tiles_per) * tm
    pos = (jnp.sum(onehot * row_off[None, :], axis=1) + rank).astype(jnp.int32)
    tok = (jnp.arange(n_pairs, dtype=jnp.int32) % rows)
    tok_sorted = jnp.zeros((n_tiles * tm,), jnp.int32).at[pos].set(tok)
    tile_ids = jnp.arange(n_tiles, dtype=jnp.int32)
    tile_expert = jnp.minimum(jnp.sum((tile_ids[:, None] >= tile_end[None, :]).astype(jnp.int32), axis=1),
                              n_experts - 1).astype(jnp.int32)
    n_used = tile_end[-1:].astype(jnp.int32)
    return pos, tok_sorted, tile_expert, n_used


def kernel(x, c, ctx, c_ctx, w_ada, b_ada, g_mix, g_ffn, w_in, b_in, g_q, g_k, sink, conv_w, conv_b, g_mh,
           w_br_attn, w_br_mlstm, w_out, w_router, b_router, w_gate, w_up, w_down):
    n_batch, lat_len, d = x.shape
    ctx_len = ctx.shape[1]
    depth = w_ada.shape[0]
    d_in = w_in.shape[2]
    aw = w_br_attn.shape[1]
    mw = w_br_mlstm.shape[1]
    mqk = conv_w.shape[2] // 2
    n_mh = mw // MLSTM_V_DIM
    kvw = (d_in - aw - 2 * mqk - 2 * mw - 4 * n_mh - 2 * d) // 2
    n_experts = w_router.shape[1]
    assert n_batch + 1 <= MOD_ROWS and 2 * n_mh <= LANES

    n_lat_rows = n_batch * lat_len
    n_ctx_rows = n_batch * ctx_len
    tm = _pick(n_ctx_rows, (512, 256))
    assert lat_len % tm == 0
    seg_args = (n_lat_rows // tm, lat_len // tm, n_batch)
    tm_e = 512

    o_aq = 0
    o_ak = o_aq + aw
    o_av = o_ak + kvw
    o_mq = o_av + kvw
    o_mk = o_mq + mqk
    o_mv = o_mk + mqk
    o_mo = o_mv + mw
    o_g = o_mo + mw
    o_ga = o_g + 4 * n_mh
    o_gm = o_ga + d
    order = [(o_aq, aw), (o_mv, mw), (o_mo, mw), (o_ga, d), (o_gm, d), (o_ak, kvw), (o_av, kvw), (o_mq, mqk), (o_mk, mqk)]
    starts = [sum(w for _, w in order[:k]) for k in range(len(order))]
    _, n_mv, n_mo, n_ga, n_gm, n_ak, n_av, n_mq, _ = starts

    xs = jnp.concatenate([x.reshape(n_lat_rows, d), ctx.reshape(n_ctx_rows, d)], axis=0)
    cvec = jnp.zeros((MOD_ROWS, d), F32).at[:n_batch].set(c).at[n_batch].set(c_ctx)
    mod = _adaln(cvec, w_ada, b_ada)
    tabs = _rope_tables(lat_len)
    w_router_t = w_router.T

    for l in range(depth):
        need_ctx = l < depth - 1
        rows = n_lat_rows + (n_ctx_rows if need_ctx else 0)

        w_main = jnp.concatenate([w_in[l][:, o:o + w] for o, w in order], axis=1).astype(BF16)
        b_main = jnp.concatenate([b_in[l][o:o + w] for o, w in order]).reshape(1, -1)
        wg = jnp.zeros((2, d, LANES), F32)
        bg = jnp.zeros((2, 1, LANES), F32)
        for dr in range(2):
            wg = wg.at[dr, :, :2 * n_mh].set(w_in[l][:, o_g + 2 * n_mh * dr:o_g + 2 * n_mh * (dr + 1)])
            bg = bg.at[dr, 0, :2 * n_mh].set(b_in[l][o_g + 2 * n_mh * dr:o_g + 2 * n_mh * (dr + 1)])

        proj, gates = _inproj(xs, g_mix[l].reshape(1, d), mod, l, w_main, b_main, wg.astype(BF16), bg, tm, seg_args)
        qk = _conv(proj, conv_w[l], conv_b[l], n_mq, n_lat_rows, lat_len, ctx_len)
        hfb = _mlstm(qk, proj, gates, mqk, mw, n_mv, n_batch, lat_len, ctx_len)
        qr, kr = _rope(proj, tabs, g_q[l], g_k[l], aw, kvw, n_ak, n_lat_rows)
        attn = _attention(sink[l], qr, kr, proj, aw, kvw, n_av, n_batch, lat_len, ctx_len, need_ctx)
        u = _branch(attn, hfb, proj, g_mh[l], w_br_attn[l].astype(BF16), w_br_mlstm[l].astype(BF16),
                    n_mo, n_ga, n_gm, rows, tm)
        xs = _outproj(u, w_out[l].astype(BF16), xs, mod, l, tm, seg_args)

        hp, idx, wts = _router(xs, g_ffn[l].reshape(1, d), mod, l, w_router_t, b_router, rows, ROUTER_ROWS,
                               (n_lat_rows // ROUTER_ROWS, lat_len // ROUTER_ROWS, n_batch))
        pos, tok_sorted, tile_expert, n_used = _route_tables(idx, n_experts, tm_e)
        hs = _gather_rows(tok_sorted, hp)
        ys = _experts(tile_expert, n_used, hs, w_gate[l].astype(BF16), w_up[l].astype(BF16),
                      w_down[l].astype(BF16), tm_e)
        xs = _combine(pos, xs, wts.T, mod, l, ys, rows, (n_lat_rows // COMBINE_ROWS, lat_len // COMBINE_ROWS, n_batch))

    return xs[:n_lat_rows].reshape(n_batch, lat_len, d)
```

```python
import functools

import jax
import jax.numpy as jnp
from jax import lax
from jax.experimental import pallas as pl
from jax.experimental.pallas import tpu as pltpu

GRID_W = 64
HEAD_DIM = 128
WINDOW = 128
QBLK = 128
ROPE_THETA = 10000.0
ROPE_PAIRS = HEAD_DIM // 4
ATTN_SCALE = HEAD_DIM ** -0.5
MLSTM_QK_DIM = 128
MLSTM_V_DIM = 256
MLSTM_CHUNK = 128
N_GROUPS = 4
EXPERTS_PER_GROUP = 4
EPS = 1e-6
NEG = -1e30

LANES = 128
MOD_ROWS = 8
ROW_CHUNK = 64
VMEM_LIMIT = 56 << 20

F32 = jnp.float32
BF16 = jnp.bfloat16


def _pick(n, cands):
    for c in cands:
        if n % c == 0:
            return c
    raise ValueError(f"no tile in {cands} divides {n}")


def _cparams(sem, vmem=VMEM_LIMIT):
    return pltpu.CompilerParams(dimension_semantics=sem, vmem_limit_bytes=vmem)


def _seg_of_block(i, n_lat_blocks, blocks_per_batch, n_batch):
    return jnp.where(i < n_lat_blocks, i // blocks_per_batch, n_batch)


def _modulated(x, g, sc, sh):
    ms = jnp.mean(x * x, axis=-1, keepdims=True)
    y = x * lax.rsqrt(ms + EPS) * g
    return y * (1.0 + sc) + sh


def _adaln_kernel(c_ref, w_ref, b_ref, o_ref):
    c = c_ref[...]
    cs = (c * jax.nn.sigmoid(c)).astype(BF16)
    o_ref[...] = jnp.dot(cs, w_ref[...].astype(BF16), preferred_element_type=F32) + b_ref[...]


def _adaln(cvec, w_ada, b_ada):
    depth, d, n6 = w_ada.shape
    tn = _pick(n6, (512, 256, 128))
    return pl.pallas_call(
        _adaln_kernel,
        grid=(depth, n6 // tn),
        in_specs=[
            pl.BlockSpec((MOD_ROWS, d), lambda l, j: (0, 0)),
            pl.BlockSpec((None, d, tn), lambda l, j: (l, 0, j)),
            pl.BlockSpec((None, 1, tn), lambda l, j: (l, 0, j)),
        ],
        out_specs=pl.BlockSpec((None, MOD_ROWS, tn), lambda l, j: (l, 0, j)),
        out_shape=jax.ShapeDtypeStruct((depth, MOD_ROWS, n6), F32),
        compiler_params=_cparams(("arbitrary", "arbitrary")),
        name="adaln",
    )(cvec, w_ada, b_ada.reshape(depth, 1, n6))


def _inproj_kernel(x_ref, g_ref, sh_ref, sc_ref, w_ref, b_ref, wg_ref, bg_ref, o_ref, og_ref, h_scr, *, seg_args):
    i = pl.program_id(0)
    j = pl.program_id(1)

    @pl.when(j == 0)
    def _():
        seg = _seg_of_block(i, *seg_args)
        g = g_ref[...]
        sc = sc_ref[pl.ds(seg, 1), :]
        sh = sh_ref[pl.ds(seg, 1), :]

        def rows_body(r, carry):
            rs = pl.ds(pl.multiple_of(r * ROW_CHUNK, ROW_CHUNK), ROW_CHUNK)
            h_scr[rs, :] = _modulated(x_ref[rs, :], g, sc, sh).astype(BF16)
            return carry

        lax.fori_loop(0, x_ref.shape[0] // ROW_CHUNK, rows_body, 0)
        for d in range(2):
            og_ref[d] = jnp.dot(h_scr[...], wg_ref[d], preferred_element_type=F32) + bg_ref[d]

    o_ref[...] = (jnp.dot(h_scr[...], w_ref[...], preferred_element_type=F32) + b_ref[...]).astype(o_ref.dtype)


def _inproj(xs, g, mod, layer, w_main, b_main, w_gates, b_gates, tm, seg_args):
    t, d = xs.shape
    nc = w_main.shape[1]
    tn = _pick(nc, (1024, 512, 256, 128))
    return pl.pallas_call(
        functools.partial(_inproj_kernel, seg_args=seg_args),
        grid=(t // tm, nc // tn),
        in_specs=[
            pl.BlockSpec((tm, d), lambda i, j: (i, 0)),
            pl.BlockSpec((1, d), lambda i, j: (0, 0)),
            pl.BlockSpec((None, MOD_ROWS, d), lambda i, j: (layer, 0, 0)),
            pl.BlockSpec((None, MOD_ROWS, d), lambda i, j: (layer, 0, 1)),
            pl.BlockSpec((d, tn), lambda i, j: (0, j)),
            pl.BlockSpec((1, tn), lambda i, j: (0, j)),
            pl.BlockSpec((2, d, LANES), lambda i, j: (0, 0, 0)),
            pl.BlockSpec((2, 1, LANES), lambda i, j: (0, 0, 0)),
        ],
        out_specs=[
            pl.BlockSpec((tm, tn), lambda i, j: (i, j)),
            pl.BlockSpec((2, tm, LANES), lambda i, j: (0, i, 0)),
        ],
        out_shape=[
            jax.ShapeDtypeStruct((t, nc), BF16),
            jax.ShapeDtypeStruct((2, t, LANES), F32),
        ],
        scratch_shapes=[pltpu.VMEM((tm, d), BF16)],
        compiler_params=_cparams(("arbitrary", "arbitrary")),
        name="inproj",
    )(xs, g, mod, mod, w_main, b_main, w_gates, b_gates)


CONV_ROWS = 256
HALO_ROWS = 16


def _conv_kernel(cur_ref, prev_ref, next_ref, w_ref, b_ref, o_ref, *, n_lat_rows, lat_len, ctx_len, k_col_block):
    i = pl.program_id(0)
    j = pl.program_id(1)
    row0 = i * CONV_ROWS
    in_lat = row0 < n_lat_rows
    seg_len = jnp.where(in_lat, lat_len, ctx_len)
    off = jnp.where(in_lat, row0, row0 - n_lat_rows) % seg_len
    has_prev = (off != 0).astype(F32)
    has_next = (off + CONV_ROWS != seg_len).astype(F32)

    x = cur_ref[...].astype(F32)
    prev_row = prev_ref[HALO_ROWS - 1:HALO_ROWS, :].astype(F32) * has_prev
    next_row = next_ref[0:1, :].astype(F32) * has_next
    rows = lax.broadcasted_iota(jnp.int32, x.shape, 0)
    xm1 = jnp.where(rows == 0, prev_row, pltpu.roll(x, 1, 0))
    xp1 = jnp.where(rows == CONV_ROWS - 1, next_row, pltpu.roll(x, CONV_ROWS - 1, 0))
    w = w_ref[...]
    y = w[0:1, :] * xm1 + w[1:2, :] * x + w[2:3, :] * xp1 + b_ref[...]
    y = y * jax.nn.sigmoid(y)
    scale = jnp.where(j >= k_col_block, MLSTM_QK_DIM ** -0.5, 1.0).astype(F32)
    o_ref[...] = (y * scale).astype(o_ref.dtype)


def _conv(proj, conv_w, conv_b, qk_off, n_lat_rows, lat_len, ctx_len):
    t = proj.shape[0]
    width = conv_w.shape[1]
    tc = _pick(width // 2, (512, 256, 128))
    assert qk_off % tc == 0 and lat_len % CONV_ROWS == 0 and ctx_len % CONV_ROWS == 0
    cb = qk_off // tc
    halo_per_blk = CONV_ROWS // HALO_ROWS
    n_halo = t // HALO_ROWS
    return pl.pallas_call(
        functools.partial(_conv_kernel, n_lat_rows=n_lat_rows, lat_len=lat_len, ctx_len=ctx_len,
                          k_col_block=(width // 2) // tc),
        grid=(t // CONV_ROWS, width // tc),
        in_specs=[
            pl.BlockSpec((CONV_ROWS, tc), lambda i, j: (i, cb + j)),
            pl.BlockSpec((HALO_ROWS, tc), lambda i, j: (jnp.maximum(i * halo_per_blk - 1, 0), cb + j)),
            pl.BlockSpec((HALO_ROWS, tc), lambda i, j: (jnp.minimum((i + 1) * halo_per_blk, n_halo - 1), cb + j)),
            pl.BlockSpec((3, tc), lambda i, j: (0, j)),
            pl.BlockSpec((1, tc), lambda i, j: (0, j)),
        ],
        out_specs=pl.BlockSpec((CONV_ROWS, tc), lambda i, j: (i, j)),
        out_shape=jax.ShapeDtypeStruct((t, width), BF16),
        compiler_params=_cparams(("arbitrary", "arbitrary")),
        name="qk_conv",
    )(proj, proj, proj, conv_w, conv_b.reshape(1, width))


ROPE_ROWS = 256


def _rope_kernel(q_ref, k_ref, cos_ref, s1_ref, s2_ref, gq_ref, gk_ref, qo_ref, ko_ref, *, n_q_heads, n_k_heads):
    cos = cos_ref[...]
    s1 = s1_ref[...]
    s2 = s2_ref[...]

    def prep(x, g):
        x = x.astype(F32)
        xn = x * lax.rsqrt(jnp.mean(x * x, axis=-1, keepdims=True) + EPS) * g
        return xn * cos + pltpu.roll(xn, HEAD_DIM - ROPE_PAIRS, 1) * s1 + pltpu.roll(xn, ROPE_PAIRS, 1) * s2

    gq = gq_ref[...]
    gk = gk_ref[...]
    for h in range(n_q_heads):
        sl = slice(h * HEAD_DIM, (h + 1) * HEAD_DIM)
        qo_ref[:, sl] = (prep(q_ref[:, sl], gq) * ATTN_SCALE).astype(qo_ref.dtype)
    for h in range(n_k_heads):
        sl = slice(h * HEAD_DIM, (h + 1) * HEAD_DIM)
        ko_ref[:, sl] = prep(k_ref[:, sl], gk).astype(ko_ref.dtype)


def _rope(proj, tabs, g_q, g_k, aw, kvw, k_off, n_lat_rows):
    t = proj.shape[0]
    assert k_off % kvw == 0
    kb = k_off // kvw
    n_lat_blk = n_lat_rows // ROPE_ROWS
    lat_blk_per_batch = (tabs[0].shape[0] - ROPE_ROWS) // ROPE_ROWS

    def tab_map(i):
        return (jnp.where(i < n_lat_blk, i % lat_blk_per_batch, lat_blk_per_batch), 0)

    tab_spec = pl.BlockSpec((ROPE_ROWS, HEAD_DIM), tab_map)
    return pl.pallas_call(
        functools.partial(_rope_kernel, n_q_heads=aw // HEAD_DIM, n_k_heads=kvw // HEAD_DIM),
        grid=(t // ROPE_ROWS,),
        in_specs=[
            pl.BlockSpec((ROPE_ROWS, aw), lambda i: (i, 0)),
            pl.BlockSpec((ROPE_ROWS, kvw), lambda i: (i, kb)),
            tab_spec, tab_spec, tab_spec,
            pl.BlockSpec((1, HEAD_DIM), lambda i: (0, 0)),
            pl.BlockSpec((1, HEAD_DIM), lambda i: (0, 0)),
        ],
        out_specs=[
            pl.BlockSpec((ROPE_ROWS, aw), lambda i: (i, 0)),
            pl.BlockSpec((ROPE_ROWS, kvw), lambda i: (i, 0)),
        ],
        out_shape=[jax.ShapeDtypeStruct((t, aw), BF16), jax.ShapeDtypeStruct((t, kvw), BF16)],
        compiler_params=_cparams(("arbitrary",)),
        name="qk_norm_rope",
    )(proj, proj, tabs[0], tabs[1], tabs[2], g_q.reshape(1, HEAD_DIM), g_k.reshape(1, HEAD_DIM))


def _rope_tables(n_lat):
    rows = n_lat // GRID_W
    inv_freq = ROPE_THETA ** (-jnp.arange(ROPE_PAIRS, dtype=F32) / ROPE_PAIRS)
    row_pos = jnp.repeat(jnp.arange(rows, dtype=F32), GRID_W)
    col_pos = jnp.tile(jnp.arange(GRID_W, dtype=F32), rows)
    ang_r = row_pos[:, None] * inv_freq
    ang_c = col_pos[:, None] * inv_freq
    zeros = jnp.zeros_like(ang_r)
    cos = jnp.concatenate([jnp.cos(ang_r), jnp.cos(ang_r), jnp.cos(ang_c), jnp.cos(ang_c)], axis=-1)
    s1 = jnp.concatenate([-jnp.sin(ang_r), zeros, -jnp.sin(ang_c), zeros], axis=-1)
    s2 = jnp.concatenate([zeros, jnp.sin(ang_r), zeros, jnp.sin(ang_c)], axis=-1)
    ident = jnp.ones((ROPE_ROWS, HEAD_DIM), F32)
    zpad = jnp.zeros((ROPE_ROWS, HEAD_DIM), F32)
    return (jnp.concatenate([cos, ident], 0), jnp.concatenate([s1, zpad], 0), jnp.concatenate([s2, zpad], 0))


def _attn_kernel(sink_ref, q_ref, kp_ref, kc_ref, kn_ref, kx_ref, vp_ref, vc_ref, vn_ref, vx_ref, o_ref,
                 *, n_lat_blk, n_kv, group, ctx_len):
    n = pl.program_id(1)
    is_ctx = n >= n_lat_blk
    n_band = 3 * QBLK
    n_keys = n_band + ctx_len
    qi = lax.broadcasted_iota(jnp.int32, (QBLK, n_keys), 0)
    kj = lax.broadcasted_iota(jnp.int32, (QBLK, n_keys), 1)
    rel = kj - QBLK - qi
    kpos = n * QBLK + kj - QBLK
    band_ok = (jnp.abs(rel) <= WINDOW) & (kpos >= 0) & (kpos < n_lat_blk * QBLK) & jnp.logical_not(is_ctx)
    valid = band_ok | (kj >= n_band)

    for hk in range(n_kv):
        ksl = slice(hk * HEAD_DIM, (hk + 1) * HEAD_DIM)
        k_all = jnp.concatenate([kp_ref[:, ksl], kc_ref[:, ksl], kn_ref[:, ksl], kx_ref[:, ksl]], axis=0)
        v_all = jnp.concatenate([vp_ref[:, ksl], vc_ref[:, ksl], vn_ref[:, ksl], vx_ref[:, ksl]], axis=0)
        for g in range(group):
            h = hk * group + g
            qsl = slice(h * HEAD_DIM, (h + 1) * HEAD_DIM)
            s = lax.dot_general(q_ref[:, qsl], k_all, (((1,), (1,)), ((), ())), preferred_element_type=F32)
            s = jnp.where(valid, s, NEG)
            sink = sink_ref[h]
            m = jnp.maximum(jnp.max(s, axis=-1, keepdims=True), sink)
            p = jnp.exp(s - m)
            denom = jnp.sum(p, axis=-1, keepdims=True) + jnp.exp(sink - m)
            o = jnp.dot(p.astype(BF16), v_all, preferred_element_type=F32)
            o_ref[:, qsl] = (o / denom).astype(o_ref.dtype)


def _attention(sink, qr, kr, proj, aw, kvw, v_off, n_batch, lat_len, ctx_len, with_ctx):
    t = proj.shape[0]
    assert v_off % kvw == 0
    vb = v_off // kvw
    n_lat_blk = lat_len // QBLK
    n_ctx_blk = ctx_len // QBLK
    n_lat_rows = n_batch * lat_len
    nblk = n_lat_blk + (n_ctx_blk if with_ctx else 0)

    def qrow(b, n):
        return jnp.where(n < n_lat_blk, b * n_lat_blk + n, n_lat_rows // QBLK + b * n_ctx_blk + (n - n_lat_blk))

    def band(delta):
        def f(b, n):
            nn = jnp.clip(jnp.minimum(n, n_lat_blk - 1) + delta, 0, n_lat_blk - 1)
            return b * n_lat_blk + nn
        return f

    def ctx_row(b, n):
        return n_lat_rows // ctx_len + b

    def kspec(rowf):
        return pl.BlockSpec((QBLK, kvw), lambda b, n: (rowf(b, n), 0))

    def vspec(rowf):
        return pl.BlockSpec((QBLK, kvw), lambda b, n: (rowf(b, n), vb))

    return pl.pallas_call(
        functools.partial(_attn_kernel, n_lat_blk=n_lat_blk, n_kv=kvw // HEAD_DIM,
                          group=aw // kvw, ctx_len=ctx_len),
        grid=(n_batch, nblk),
        in_specs=[
            pl.BlockSpec(memory_space=pltpu.SMEM),
            pl.BlockSpec((QBLK, aw), lambda b, n: (qrow(b, n), 0)),
            kspec(band(-1)), kspec(band(0)), kspec(band(1)),
            pl.BlockSpec((ctx_len, kvw), lambda b, n: (ctx_row(b, n), 0)),
            vspec(band(-1)), vspec(band(0)), vspec(band(1)),
            pl.BlockSpec((ctx_len, kvw), lambda b, n: (ctx_row(b, n), vb)),
        ],
        out_specs=pl.BlockSpec((QBLK, aw), lambda b, n: (qrow(b, n), 0)),
        out_shape=jax.ShapeDtypeStruct((n_lat_rows + (n_batch * ctx_len if with_ctx else 0), aw), BF16),
        compiler_params=_cparams(("arbitrary", "arbitrary")),
        name="attention",
    )(sink, qr, kr, kr, kr, kr, proj, proj, proj, proj)


def _mlstm_kernel(q_ref, k_ref, v_ref, g_ref, o_ref, *state, n_heads):
    c_scrs, nm_scrs = state[:n_heads], state[n_heads:]
    d = pl.program_id(0)
    c = pl.program_id(2)
    L = MLSTM_CHUNK
    dk = MLSTM_QK_DIM
    dv = MLSTM_V_DIM

    @pl.when(c == 0)
    def _():
        for scr in state:
            scr[...] = jnp.zeros_like(scr)

    r = lax.broadcasted_iota(jnp.int32, (L, L), 0)
    s = lax.broadcasted_iota(jnp.int32, (L, L), 1)
    fwd = d == 0
    lag = (r - s) * (1 - 2 * d)
    tri = lag >= 0
    tri_t = lag <= 0
    tri_b = tri.astype(BF16)
    tri_tb = tri_t.astype(BF16)

    gates = g_ref[...]
    logf = jnp.minimum(gates, 0.0) - jnp.log1p(jnp.exp(-jnp.abs(gates)))
    gates_t = gates.T
    logf_t = logf.T

    def split_dot_l(mat_b, x):
        hi = x.astype(BF16)
        lo = (x - hi.astype(F32)).astype(BF16)
        return jnp.dot(mat_b, hi, preferred_element_type=F32) + jnp.dot(mat_b, lo, preferred_element_type=F32)

    def split_dot_r(x, mat_b):
        hi = x.astype(BF16)
        lo = (x - hi.astype(F32)).astype(BF16)
        return jnp.dot(hi, mat_b, preferred_element_type=F32) + jnp.dot(lo, mat_b, preferred_element_type=F32)

    cum_col = split_dot_l(tri_b, logf)
    cum_row = split_dot_r(logf_t, tri_tb)
    end_col = jnp.where(fwd, cum_col[L - 1:L, :], cum_col[0:1, :])

    for h in range(n_heads):
        ci, cf = h, n_heads + h
        q = q_ref[:, h * dk:(h + 1) * dk]
        k = k_ref[:, h * dk:(h + 1) * dk]
        v = v_ref[:, h * dv:(h + 1) * dv]
        b_col = cum_col[:, cf:cf + 1]
        b_row = cum_row[cf:cf + 1, :]
        i_col = gates[:, ci:ci + 1]
        i_row = gates_t[ci:ci + 1, :]
        b_end = end_col[:, cf:cf + 1]
        c_scr, nm_scr = c_scrs[h], nm_scrs[h]
        n_prev = nm_scr[0:1, :]
        m_prev = nm_scr[1:2, 0:1]
        ct_prev = c_scr[...]

        dmat = jnp.where(tri, b_col - b_row + i_row, NEG)
        m_inter = b_col + m_prev
        m_t = jnp.maximum(m_inter, jnp.max(dmat, axis=-1, keepdims=True))
        p = jnp.exp(dmat - m_t)
        a = jnp.exp(m_inter - m_t)
        smat = lax.dot_general(q, k, (((1,), (1,)), ((), ())), preferred_element_type=F32) * p
        num = (jnp.dot(smat.astype(BF16), v, preferred_element_type=F32)
               + a * jnp.dot(q, ct_prev.astype(BF16), preferred_element_type=F32))
        qn = jnp.sum(smat, axis=-1, keepdims=True) + a * jnp.sum(q.astype(F32) * n_prev, axis=-1, keepdims=True)
        hout = num / jnp.maximum(jnp.abs(qn), jnp.exp(-m_t))
        o_ref[:, h * dv:(h + 1) * dv] = hout.astype(o_ref.dtype)

        g_row = b_end - b_row + i_row
        g_col = b_end - b_col + i_col
        m_new = jnp.maximum(b_end + m_prev, jnp.max(g_row, axis=-1, keepdims=True))
        w_col = jnp.exp(g_col - m_new)
        a_end = jnp.exp(b_end + m_prev - m_new)
        kw = k.astype(F32) * w_col
        c_scr[...] = a_end * ct_prev + jnp.dot(kw.T.astype(BF16), v, preferred_element_type=F32)
        nm_scr[0:1, :] = a_end * n_prev + jnp.sum(kw, axis=0, keepdims=True)
        nm_scr[1:2, :] = jnp.broadcast_to(m_new, (1, dk))


def _mlstm(qk, proj, gates, mqk, mw, v_off, n_batch, lat_len, ctx_len):
    t = proj.shape[0]
    L = MLSTM_CHUNK
    n_heads = mw // MLSTM_V_DIM
    assert v_off % mw == 0 and mqk == n_heads * MLSTM_QK_DIM
    vb = v_off // mw
    n_ctx = ctx_len // L
    n_lat = lat_len // L
    lat_blocks = n_batch * n_lat

    def row(d, b, c):
        cc = jnp.where(d == 0, c, n_ctx - 1 - c)
        lc = jnp.where(d == 0, c - n_ctx, n_lat - 1 - (c - n_ctx))
        return jnp.where(c < n_ctx, lat_blocks + b * n_ctx + cc, b * n_lat + lc)

    return pl.pallas_call(
        functools.partial(_mlstm_kernel, n_heads=n_heads),
        grid=(2, n_batch, n_ctx + n_lat),
        in_specs=[
            pl.BlockSpec((L, mqk), lambda d, b, c: (row(d, b, c), 0)),
            pl.BlockSpec((L, mqk), lambda d, b, c: (row(d, b, c), 1)),
            pl.BlockSpec((L, mw), lambda d, b, c: (row(d, b, c), vb)),
            pl.BlockSpec((None, L, LANES), lambda d, b, c: (d, row(d, b, c), 0)),
        ],
        out_specs=pl.BlockSpec((None, L, mw), lambda d, b, c: (d, row(d, b, c), 0)),
        out_shape=jax.ShapeDtypeStruct((2, t, mw), BF16),
        scratch_shapes=([pltpu.VMEM((MLSTM_QK_DIM, MLSTM_V_DIM), F32)] * n_heads
                        + [pltpu.VMEM((8, MLSTM_QK_DIM), F32)] * n_heads),
        compiler_params=_cparams(("arbitrary", "arbitrary", "arbitrary")),
        name="mlstm",
    )(qk, qk, proj, gates)


def _branch_kernel(attn_ref, hfb_ref, mo_ref, gmh_ref, wa_ref, wm_ref, ga_ref, gm_ref, o_ref, hm_scr, *, n_heads):
    j = pl.program_id(1)

    @pl.when(j == 0)
    def _():
        dv = MLSTM_V_DIM
        for h in range(n_heads):
            sl = slice(h * dv, (h + 1) * dv)
            hsum = hfb_ref[0, :, sl].astype(F32) + hfb_ref[1, :, sl].astype(F32)
            x = jax.nn.sigmoid(mo_ref[:, sl].astype(F32)) * hsum
            y = x * lax.rsqrt(jnp.mean(x * x, axis=-1, keepdims=True) + EPS) * gmh_ref[:, sl]
            hm_scr[:, sl] = y.astype(BF16)

    ya = jnp.dot(attn_ref[...], wa_ref[...], preferred_element_type=F32)
    ym = jnp.dot(hm_scr[...], wm_ref[...], preferred_element_type=F32)
    u = jax.nn.sigmoid(ga_ref[...].astype(F32)) * ya + jax.nn.sigmoid(gm_ref[...].astype(F32)) * ym
    o_ref[...] = u.astype(o_ref.dtype)


def _branch(attn, hfb, proj, g_mh, wa, wm, mo_off, ga_off, gm_off, rows, tm):
    aw = attn.shape[1]
    mw = hfb.shape[2]
    d = wa.shape[1]
    tn = _pick(d, (1024, 512, 256, 128))
    assert mo_off % mw == 0 and ga_off % tn == 0 and gm_off % tn == 0
    mob, gab, gmb = mo_off // mw, ga_off // tn, gm_off // tn
    return pl.pallas_call(
        functools.partial(_branch_kernel, n_heads=mw // MLSTM_V_DIM),
        grid=(rows // tm, d // tn),
        in_specs=[
            pl.BlockSpec((tm, aw), lambda i, j: (i, 0)),
            pl.BlockSpec((2, tm, mw), lambda i, j: (0, i, 0)),
            pl.BlockSpec((tm, mw), lambda i, j: (i, mob)),
            pl.BlockSpec((1, mw), lambda i, j: (0, 0)),
            pl.BlockSpec((aw, tn), lambda i, j: (0, j)),
            pl.BlockSpec((mw, tn), lambda i, j: (0, j)),
            pl.BlockSpec((tm, tn), lambda i, j: (i, gab + j)),
            pl.BlockSpec((tm, tn), lambda i, j: (i, gmb + j)),
        ],
        out_specs=pl.BlockSpec((tm, tn), lambda i, j: (i, j)),
        out_shape=jax.ShapeDtypeStruct((rows, d), BF16),
        scratch_shapes=[pltpu.VMEM((tm, mw), BF16)],
        compiler_params=_cparams(("arbitrary", "arbitrary")),
        name="branch_merge",
    )(attn, hfb, proj, g_mh.reshape(1, mw), wa, wm, proj, proj)


def _outproj_kernel(u_ref, w_ref, x_ref, gt_ref, o_ref, *, seg_args):
    seg = _seg_of_block(pl.program_id(0), *seg_args)
    y = jnp.dot(u_ref[...], w_ref[...], preferred_element_type=F32)
    o_ref[...] = x_ref[...] + gt_ref[pl.ds(seg, 1), :] * y


def _outproj(u, w_out, xs, mod, layer, tm, seg_args):
    rows, d = u.shape
    tn = _pick(d, (1024, 512, 256, 128))
    gate_blk = 2 * (d // tn)
    return pl.pallas_call(
        functools.partial(_outproj_kernel, seg_args=seg_args),
        grid=(rows // tm, d // tn),
        in_specs=[
            pl.BlockSpec((tm, d), lambda i, j: (i, 0)),
            pl.BlockSpec((d, tn), lambda i, j: (0, j)),
            pl.BlockSpec((tm, tn), lambda i, j: (i, j)),
            pl.BlockSpec((None, MOD_ROWS, tn), lambda i, j: (layer, 0, gate_blk + j)),
        ],
        out_specs=pl.BlockSpec((tm, tn), lambda i, j: (i, j)),
        out_shape=jax.ShapeDtypeStruct(xs.shape, F32),
        input_output_aliases={2: 0},
        compiler_params=_cparams(("arbitrary", "arbitrary")),
        name="outproj_residual",
    )(u, w_out, xs, mod)


def _pack_bf16_pairs(h):
    half = h.shape[1] // 2
    hi = pltpu.bitcast(h[:, :half].astype(BF16).astype(F32), jnp.uint32)
    lo = pltpu.bitcast(h[:, half:].astype(BF16).astype(F32), jnp.uint32)
    return hi | (lo >> 16)


def _unpack_bf16_pairs(p):
    hi = pltpu.bitcast(p & jnp.uint32(0xFFFF0000), F32)
    lo = pltpu.bitcast(p << 16, F32)
    return hi, lo


ROUTER_ROWS = 256


def _router_kernel(x_ref, g_ref, sh_ref, sc_ref, wr_ref, br_ref, hp_ref, idx_ref, wt_ref, *, seg_args, n_experts):
    seg = _seg_of_block(pl.program_id(0), *seg_args)
    h = _modulated(x_ref[...], g_ref[...], sc_ref[pl.ds(seg, 1), :], sh_ref[pl.ds(seg, 1), :])
    hp_ref[...] = _pack_bf16_pairs(h)

    wr = wr_ref[...]
    h_hi = h.astype(BF16)
    h_lo = (h - h_hi.astype(F32)).astype(BF16)
    w_hi = wr.astype(BF16)
    w_lo = (wr - w_hi.astype(F32)).astype(BF16)
    nt = (((1,), (1,)), ((), ()))
    logits = (lax.dot_general(w_hi, h_hi, nt, preferred_element_type=F32)
              + lax.dot_general(w_hi, h_lo, nt, preferred_element_type=F32)
              + lax.dot_general(w_lo, h_hi, nt, preferred_element_type=F32))
    aff = jax.nn.sigmoid(logits)
    biased = aff + br_ref[...]
    rb = [biased[e:e + 1, :] for e in range(n_experts)]
    ra = [aff[e:e + 1, :] for e in range(n_experts)]

    epg = EXPERTS_PER_GROUP
    scores = []
    for g in range(N_GROUPS):
        a, b, c, d = rb[epg * g:epg * g + epg]
        hi1, lo1 = jnp.maximum(a, b), jnp.minimum(a, b)
        hi2, lo2 = jnp.maximum(c, d), jnp.minimum(c, d)
        scores.append(jnp.maximum(hi1, hi2) + jnp.maximum(jnp.minimum(hi1, hi2), jnp.maximum(lo1, lo2)))
    best = jnp.zeros(scores[0].shape, jnp.int32)
    best_s = scores[0]
    for g in range(1, N_GROUPS):
        upd = scores[g] > best_s
        best = jnp.where(upd, g, best)
        best_s = jnp.where(upd, scores[g], best_s)

    vb, va = [], []
    for j in range(epg):
        xb, xa = rb[j], ra[j]
        for g in range(1, N_GROUPS):
            sel = best == g
            xb = jnp.where(sel, rb[epg * g + j], xb)
            xa = jnp.where(sel, ra[epg * g + j], xa)
        vb.append(xb)
        va.append(xa)

    i1 = jnp.zeros_like(best)
    m1, a1 = vb[0], va[0]
    for j in range(1, epg):
        upd = vb[j] > m1
        i1 = jnp.where(upd, j, i1)
        m1 = jnp.where(upd, vb[j], m1)
        a1 = jnp.where(upd, va[j], a1)
    i2 = jnp.zeros_like(best)
    m2 = jnp.full_like(m1, -jnp.inf)
    a2 = jnp.zeros_like(a1)
    for j in range(epg):
        upd = (i1 != j) & (vb[j] > m2)
        i2 = jnp.where(upd, j, i2)
        m2 = jnp.where(upd, vb[j], m2)
        a2 = jnp.where(upd, va[j], a2)

    idx_ref[0:1, :] = best * epg + i1
    idx_ref[1:2, :] = best * epg + i2
    tot = a1 + a2
    wt_ref[0:1, :] = a1 / tot
    wt_ref[1:2, :] = a2 / tot


def _router(xs, g, mod, layer, w_router_t, b_router, rows, tm, seg_args):
    d = xs.shape[1]
    e = w_router_t.shape[0]
    assert e == N_GROUPS * EXPERTS_PER_GROUP
    return pl.pallas_call(
        functools.partial(_router_kernel, seg_args=seg_args, n_experts=e),
        grid=(rows // tm,),
        in_specs=[
            pl.BlockSpec((tm, d), lambda i: (i, 0)),
            pl.BlockSpec((1, d), lambda i: (0, 0)),
            pl.BlockSpec((None, MOD_ROWS, d), lambda i: (layer, 0, 3)),
            pl.BlockSpec((None, MOD_ROWS, d), lambda i: (layer, 0, 4)),
            pl.BlockSpec((e, d), lambda i: (0, 0)),
            pl.BlockSpec((e, 1), lambda i: (0, 0)),
        ],
        out_specs=[
            pl.BlockSpec((tm, d // 2), lambda i: (i, 0)),
            pl.BlockSpec((2, tm), lambda i: (0, i)),
            pl.BlockSpec((2, tm), lambda i: (0, i)),
        ],
        out_shape=[
            jax.ShapeDtypeStruct((rows, d // 2), jnp.uint32),
            jax.ShapeDtypeStruct((2, rows), jnp.int32),
            jax.ShapeDtypeStruct((2, rows), F32),
        ],
        compiler_params=_cparams(("arbitrary",)),
        name="ffn_modulate_route",
    )(xs, g, mod, mod, w_router_t, b_router.reshape(e, 1))


def _route_kernel(idx_ref, pos_ref, te_ref, nu_ref, *, n_experts, tm, n_chunks):
    e_iota = lax.broadcasted_iota(jnp.int32, (n_experts, LANES), 0)
    idx_all = idx_ref[...]

    def count_col(k):
        col = jnp.zeros((n_experts, LANES), F32)
        for e in range(n_experts):
            col = jnp.where(e_iota == e, jnp.sum((idx_all[k] == e).astype(F32)), col)
        return col

    c0 = count_col(0)
    counts = c0 + count_col(1)
    tiles_per = jnp.floor((counts + (tm - 1)) * (1.0 / tm))
    tile_end = tiles_per
    s = 1
    while s < n_experts:
        tile_end = tile_end + jnp.where(e_iota >= s, pltpu.roll(tile_end, s, 0), 0.0)
        s *= 2
    row_off = (tile_end - tiles_per) * tm

    r = lax.broadcasted_iota(jnp.int32, (LANES, LANES), 0)
    c = lax.broadcasted_iota(jnp.int32, (LANES, LANES), 1)
    triu = (r <= c).astype(BF16)

    def body(ch, carry):
        new = []
        for k in range(2):
            onehot = (e_iota == idx_ref[k, pl.ds(ch, 1), :]).astype(F32)
            csum = jnp.dot(onehot.astype(BF16), triu, preferred_element_type=F32)
            posv = jnp.sum(onehot * (row_off + carry[k] + csum - 1.0), axis=0, keepdims=True)
            pos_ref[k, pl.ds(ch, 1), :] = posv.astype(jnp.int32)
            new.append(carry[k] + csum[:, LANES - 1:LANES])
        return tuple(new)

    lax.fori_loop(0, n_chunks, body, (jnp.zeros((n_experts, LANES), F32), c0))

    t_iota = lax.broadcasted_iota(jnp.int32, (n_experts, LANES), 1).astype(F32)
    te = jnp.sum((tile_end <= t_iota).astype(F32), axis=0, keepdims=True)
    te_ref[...] = jnp.minimum(te, n_experts - 1.0).astype(jnp.int32)
    nu_ref[...] = tile_end[n_experts - 1:n_experts, :].astype(jnp.int32)


def _route(idx, n_experts, tm):
    k, rows = idx.shape
    assert k == 2 and rows % LANES == 0 and tm & (tm - 1) == 0
    n_chunks = rows // LANES
    n_tiles = (k * rows) // tm + n_experts
    assert n_tiles <= LANES
    pos, te, nu = pl.pallas_call(
        functools.partial(_route_kernel, n_experts=n_experts, tm=tm, n_chunks=n_chunks),
        out_shape=[
            jax.ShapeDtypeStruct((k, n_chunks, LANES), jnp.int32),
            jax.ShapeDtypeStruct((1, LANES), jnp.int32),
            jax.ShapeDtypeStruct((1, LANES), jnp.int32),
        ],
        compiler_params=pltpu.CompilerParams(vmem_limit_bytes=VMEM_LIMIT),
        name="moe_route",
    )(idx.reshape(k, n_chunks, LANES))
    return pos.reshape(k * rows), te[0, :n_tiles], nu[0, :1], n_tiles


DISPATCH_ROWS = 256


def _dispatch_kernel(pos_ref, hp_ref, init_ref, hs_ref, sem, *, n_rows):
    del init_ref
    base = pl.program_id(0) * DISPATCH_ROWS

    def row_copy(k, r):
        return pltpu.make_async_copy(hp_ref.at[pl.ds(r, 1)], hs_ref.at[pl.ds(pos_ref[k * n_rows + base + r], 1)], sem)

    def start(r, carry):
        row_copy(0, r).start()
        row_copy(1, r).start()
        return carry

    def wait(r, carry):
        row_copy(0, r).wait()
        row_copy(1, r).wait()
        return carry

    lax.fori_loop(0, DISPATCH_ROWS, start, 0)
    lax.fori_loop(0, DISPATCH_ROWS, wait, 0)


def _dispatch(pos, hp, n_sorted_rows):
    rows, half = hp.shape
    return pl.pallas_call(
        functools.partial(_dispatch_kernel, n_rows=rows),
        grid_spec=pltpu.PrefetchScalarGridSpec(
            num_scalar_prefetch=1,
            grid=(rows // DISPATCH_ROWS,),
            in_specs=[
                pl.BlockSpec((DISPATCH_ROWS, half), lambda i, pos: (i, 0)),
                pl.BlockSpec(memory_space=pl.ANY),
            ],
            out_specs=pl.BlockSpec(memory_space=pl.ANY),
            scratch_shapes=[pltpu.SemaphoreType.DMA],
        ),
        out_shape=jax.ShapeDtypeStruct((n_sorted_rows, half), hp.dtype),
        input_output_aliases={2: 0},
        compiler_params=_cparams(("arbitrary",)),
        name="moe_dispatch",
    )(pos, hp, jnp.zeros((n_sorted_rows, half), hp.dtype))


def _expert_kernel(te_ref, nused_ref, x_ref, wg_ref, wu_ref, wd_ref, o_ref, xs_scr, acc_scr):
    i = pl.program_id(0)
    f = pl.program_id(1)

    @pl.when((i >= nused_ref[0]) & (f == 0))
    def _():
        o_ref[...] = jnp.zeros_like(o_ref)

    @pl.when(i < nused_ref[0])
    def _():
        half = x_ref.shape[1]

        @pl.when(f == 0)
        def _():
            def rows_body(r, carry):
                rs = pl.ds(pl.multiple_of(r * ROW_CHUNK, ROW_CHUNK), ROW_CHUNK)
                hi, lo = _unpack_bf16_pairs(x_ref[rs, :])
                xs_scr[rs, pl.ds(0, half)] = hi.astype(BF16)
                xs_scr[rs, pl.ds(half, half)] = lo.astype(BF16)
                return carry

            lax.fori_loop(0, x_ref.shape[0] // ROW_CHUNK, rows_body, 0)
            acc_scr[...] = jnp.zeros_like(acc_scr)

        xs = xs_scr[...]
        gate = jnp.dot(xs, wg_ref[...], preferred_element_type=F32)
        up = jnp.dot(xs, wu_ref[...], preferred_element_type=F32)
        act = (gate * jax.nn.sigmoid(gate) * up).astype(BF16)
        acc_scr[...] += jnp.dot(act, wd_ref[...], preferred_element_type=F32)

        @pl.when(f == pl.num_programs(1) - 1)
        def _():
            def rows_body(r, carry):
                rs = pl.ds(pl.multiple_of(r * ROW_CHUNK, ROW_CHUNK), ROW_CHUNK)
                o_ref[rs, :] = _pack_bf16_pairs(acc_scr[rs, :])
                return carry

            lax.fori_loop(0, o_ref.shape[0] // ROW_CHUNK, rows_body, 0)


def _experts(tile_expert, n_used, hs, wg, wu, wd, tm):
    p, half = hs.shape
    e, d, ff = wg.shape
    fc = _pick(ff, (256, 128))
    n_tiles = p // tm

    def row(i, f, te, nu):
        return (jnp.minimum(i, nu[0] - 1), 0)

    return pl.pallas_call(
        _expert_kernel,
        grid_spec=pltpu.PrefetchScalarGridSpec(
            num_scalar_prefetch=2,
            grid=(n_tiles, ff // fc),
            in_specs=[
                pl.BlockSpec((tm, half), row),
                pl.BlockSpec((None, d, fc), lambda i, f, te, nu: (te[i], 0, f)),
                pl.BlockSpec((None, d, fc), lambda i, f, te, nu: (te[i], 0, f)),
                pl.BlockSpec((None, fc, d), lambda i, f, te, nu: (te[i], f, 0)),
            ],
            out_specs=pl.BlockSpec((tm, half), lambda i, f, te, nu: (i, 0)),
            scratch_shapes=[pltpu.VMEM((tm, d), BF16), pltpu.VMEM((tm, d), F32)],
        ),
        out_shape=jax.ShapeDtypeStruct((p, half), jnp.uint32),
        compiler_params=_cparams(("arbitrary", "arbitrary")),
        name="moe_experts",
    )(tile_expert, n_used, hs, wg, wu, wd)


COMBINE_ROWS = 256


def _combine_kernel(pos_ref, x_ref, wt_ref, gt_ref, ys_ref, o_ref, buf, sem, *, seg_args, n_rows):
    i = pl.program_id(0)
    base = i * COMBINE_ROWS
    seg = _seg_of_block(i, *seg_args)

    def row_copy(k, r):
        return pltpu.make_async_copy(ys_ref.at[pl.ds(pos_ref[k * n_rows + base + r], 1)],
                                     buf.at[k, pl.ds(r, 1)], sem)

    def start(r, carry):
        row_copy(0, r).start()
        row_copy(1, r).start()
        return carry

    def wait(r, carry):
        row_copy(0, r).wait()
        row_copy(1, r).wait()
        return carry

    lax.fori_loop(0, COMBINE_ROWS, start, 0)
    lax.fori_loop(0, COMBINE_ROWS, wait, 0)

    half = buf.shape[2]
    hi0, lo0 = _unpack_bf16_pairs(buf[0])
    hi1, lo1 = _unpack_bf16_pairs(buf[1])
    w0 = wt_ref[:, 0:1]
    w1 = wt_ref[:, 1:2]
    gt = gt_ref[pl.ds(seg, 1), :]
    o_ref[:, :half] = x_ref[:, :half] + gt[:, :half] * (w0 * hi0 + w1 * hi1)
    o_ref[:, half:] = x_ref[:, half:] + gt[:, half:] * (w0 * lo0 + w1 * lo1)


def _combine(pos, xs, wts_t, mod, layer, ys, rows, seg_args):
    d = xs.shape[1]
    half = d // 2
    return pl.pallas_call(
        functools.partial(_combine_kernel, seg_args=seg_args, n_rows=rows),
        grid_spec=pltpu.PrefetchScalarGridSpec(
            num_scalar_prefetch=1,
            grid=(rows // COMBINE_ROWS,),
            in_specs=[
                pl.BlockSpec((COMBINE_ROWS, d), lambda i, pos: (i, 0)),
                pl.BlockSpec((COMBINE_ROWS, 2), lambda i, pos: (i, 0)),
                pl.BlockSpec((None, MOD_ROWS, d), lambda i, pos: (layer, 0, 5)),
                pl.BlockSpec(memory_space=pl.ANY),
            ],
            out_specs=pl.BlockSpec((COMBINE_ROWS, d), lambda i, pos: (i, 0)),
            scratch_shapes=[pltpu.VMEM((2, COMBINE_ROWS, half), jnp.uint32), pltpu.SemaphoreType.DMA],
        ),
        out_shape=jax.ShapeDtypeStruct((rows, d), F32),
        compiler_params=_cparams(("arbitrary",)),
        name="moe_combine",
    )(pos, xs, wts_t, mod, ys)


def kernel(x, c, ctx, c_ctx, w_ada, b_ada, g_mix, g_ffn, w_in, b_in, g_q, g_k, sink, conv_w, conv_b, g_mh,
           w_br_attn, w_br_mlstm, w_out, w_router, b_router, w_gate, w_up, w_down):
    n_batch, lat_len, d = x.shape
    ctx_len = ctx.shape[1]
    depth = w_ada.shape[0]
    d_in = w_in.shape[2]
    aw = w_br_attn.shape[1]
    mw = w_br_mlstm.shape[1]
    mqk = conv_w.shape[2] // 2
    n_mh = mw // MLSTM_V_DIM
    kvw = (d_in - aw - 2 * mqk - 2 * mw - 4 * n_mh - 2 * d) // 2
    n_experts = w_router.shape[1]
    assert n_batch + 1 <= MOD_ROWS and 2 * n_mh <= LANES

    n_lat_rows = n_batch * lat_len
    n_ctx_rows = n_batch * ctx_len
    tm = _pick(n_ctx_rows, (512, 256))
    assert lat_len % tm == 0
    seg_args = (n_lat_rows // tm, lat_len // tm, n_batch)
    tm_e = 512

    o_aq = 0
    o_ak = o_aq + aw
    o_av = o_ak + kvw
    o_mq = o_av + kvw
    o_mk = o_mq + mqk
    o_mv = o_mk + mqk
    o_mo = o_mv + mw
    o_g = o_mo + mw
    o_ga = o_g + 4 * n_mh
    o_gm = o_ga + d
    order = [(o_aq, aw), (o_mv, mw), (o_mo, mw), (o_ga, d), (o_gm, d), (o_ak, kvw), (o_av, kvw), (o_mq, mqk), (o_mk, mqk)]
    starts = [sum(w for _, w in order[:k]) for k in range(len(order))]
    _, n_mv, n_mo, n_ga, n_gm, n_ak, n_av, n_mq, _ = starts

    xs = jnp.concatenate([x.reshape(n_lat_rows, d), ctx.reshape(n_ctx_rows, d)], axis=0)
    cvec = jnp.zeros((MOD_ROWS, d), F32).at[:n_batch].set(c).at[n_batch].set(c_ctx)
    mod = _adaln(cvec, w_ada, b_ada)
    tabs = _rope_tables(lat_len)
    w_router_t = w_router.T

    for l in range(depth):
        need_ctx = l < depth - 1
        rows = n_lat_rows + (n_ctx_rows if need_ctx else 0)

        w_main = jnp.concatenate([w_in[l][:, o:o + w] for o, w in order], axis=1).astype(BF16)
        b_main = jnp.concatenate([b_in[l][o:o + w] for o, w in order]).reshape(1, -1)
        wg = jnp.zeros((2, d, LANES), F32)
        bg = jnp.zeros((2, 1, LANES), F32)
        for dr in range(2):
            wg = wg.at[dr, :, :2 * n_mh].set(w_in[l][:, o_g + 2 * n_mh * dr:o_g + 2 * n_mh * (dr + 1)])
            bg = bg.at[dr, 0, :2 * n_mh].set(b_in[l][o_g + 2 * n_mh * dr:o_g + 2 * n_mh * (dr + 1)])

        proj, gates = _inproj(xs, g_mix[l].reshape(1, d), mod, l, w_main, b_main, wg.astype(BF16), bg, tm, seg_args)
        qk = _conv(proj, conv_w[l], conv_b[l], n_mq, n_lat_rows, lat_len, ctx_len)
        hfb = _mlstm(qk, proj, gates, mqk, mw, n_mv, n_batch, lat_len, ctx_len)
        qr, kr = _rope(proj, tabs, g_q[l], g_k[l], aw, kvw, n_ak, n_lat_rows)
        attn = _attention(sink[l], qr, kr, proj, aw, kvw, n_av, n_batch, lat_len, ctx_len, need_ctx)
        u = _branch(attn, hfb, proj, g_mh[l], w_br_attn[l].astype(BF16), w_br_mlstm[l].astype(BF16),
                    n_mo, n_ga, n_gm, rows, tm)
        xs = _outproj(u, w_out[l].astype(BF16), xs, mod, l, tm, seg_args)

        hp, idx, wts = _router(xs, g_ffn[l].reshape(1, d), mod, l, w_router_t, b_router, rows, ROUTER_ROWS,
                               (n_lat_rows // ROUTER_ROWS, lat_len // ROUTER_ROWS, n_batch))
        pos, tile_expert, n_used, n_tiles = _route(idx, n_experts, tm_e)
        hs = _dispatch(pos, hp, n_tiles * tm_e)
        ys = _experts(tile_expert, n_used, hs, w_gate[l].astype(BF16), w_up[l].astype(BF16),
                      w_down[l].astype(BF16), tm_e)
        xs = _combine(pos, xs, wts.T, mod, l, ys, rows, (n_lat_rows // COMBINE_ROWS, lat_len // COMBINE_ROWS, n_batch))

    return xs[:n_lat_rows].reshape(n_batch, lat_len, d)
```

```python
import functools

import jax
import jax.numpy as jnp
from jax import lax
from jax.experimental import pallas as pl
from jax.experimental.pallas import tpu as pltpu

GRID_W = 64
HEAD_DIM = 128
WINDOW = 128
QBLK = 128
ROPE_THETA = 10000.0
ROPE_PAIRS = HEAD_DIM // 4
ATTN_SCALE = HEAD_DIM ** -0.5
MLSTM_QK_DIM = 128
MLSTM_V_DIM = 256
MLSTM_CHUNK = 128
N_GROUPS = 4
EXPERTS_PER_GROUP = 4
EPS = 1e-6
NEG = -1e30

LANES = 128
MOD_ROWS = 8
ROW_CHUNK = 64
VMEM_LIMIT = 56 << 20

F32 = jnp.float32
BF16 = jnp.bfloat16


def _pick(n, cands):
    for c in cands:
        if n % c == 0:
            return c
    raise ValueError(f"no tile in {cands} divides {n}")


def _cparams(sem, vmem=VMEM_LIMIT):
    return pltpu.CompilerParams(dimension_semantics=sem, vmem_limit_bytes=vmem)


def _seg_of_block(i, n_lat_blocks, blocks_per_batch, n_batch):
    return jnp.where(i < n_lat_blocks, i // blocks_per_batch, n_batch)


def _modulated(x, g, sc, sh):
    ms = jnp.mean(x * x, axis=-1, keepdims=True)
    y = x * lax.rsqrt(ms + EPS) * g
    return y * (1.0 + sc) + sh


def _adaln_kernel(c_ref, w_ref, b_ref, o_ref):
    c = c_ref[...]
    cs = (c * jax.nn.sigmoid(c)).astype(BF16)
    o_ref[...] = jnp.dot(cs, w_ref[...].astype(BF16), preferred_element_type=F32) + b_ref[...]


def _adaln(cvec, w_ada, b_ada):
    depth, d, n6 = w_ada.shape
    tn = _pick(n6, (512, 256, 128))
    return pl.pallas_call(
        _adaln_kernel,
        grid=(depth, n6 // tn),
        in_specs=[
            pl.BlockSpec((MOD_ROWS, d), lambda l, j: (0, 0)),
            pl.BlockSpec((None, d, tn), lambda l, j: (l, 0, j)),
            pl.BlockSpec((None, 1, tn), lambda l, j: (l, 0, j)),
        ],
        out_specs=pl.BlockSpec((None, MOD_ROWS, tn), lambda l, j: (l, 0, j)),
        out_shape=jax.ShapeDtypeStruct((depth, MOD_ROWS, n6), F32),
        compiler_params=_cparams(("arbitrary", "arbitrary")),
        name="adaln",
    )(cvec, w_ada, b_ada.reshape(depth, 1, n6))


def _inproj_kernel(x_ref, g_ref, sh_ref, sc_ref, w_ref, b_ref, wg_ref, bg_ref, o_ref, og_ref, h_scr, *, seg_args):
    i = pl.program_id(0)
    j = pl.program_id(1)

    @pl.when(j == 0)
    def _():
        seg = _seg_of_block(i, *seg_args)
        g = g_ref[...]
        sc = sc_ref[pl.ds(seg, 1), :]
        sh = sh_ref[pl.ds(seg, 1), :]

        def rows_body(r, carry):
            rs = pl.ds(pl.multiple_of(r * ROW_CHUNK, ROW_CHUNK), ROW_CHUNK)
            h_scr[rs, :] = _modulated(x_ref[rs, :], g, sc, sh).astype(BF16)
            return carry

        lax.fori_loop(0, x_ref.shape[0] // ROW_CHUNK, rows_body, 0)
        for d in range(2):
            og_ref[d] = jnp.dot(h_scr[...], wg_ref[d], preferred_element_type=F32) + bg_ref[d]

    o_ref[...] = (jnp.dot(h_scr[...], w_ref[...], preferred_element_type=F32) + b_ref[...]).astype(o_ref.dtype)


def _inproj(xs, g, mod, layer, w_main, b_main, w_gates, b_gates, tm, seg_args):
    t, d = xs.shape
    nc = w_main.shape[2]
    tn = _pick(nc, (1024, 512, 256, 128))
    return pl.pallas_call(
        functools.partial(_inproj_kernel, seg_args=seg_args),
        grid=(t // tm, nc // tn),
        in_specs=[
            pl.BlockSpec((tm, d), lambda i, j: (i, 0)),
            pl.BlockSpec((1, d), lambda i, j: (0, 0)),
            pl.BlockSpec((None, MOD_ROWS, d), lambda i, j: (layer, 0, 0)),
            pl.BlockSpec((None, MOD_ROWS, d), lambda i, j: (layer, 0, 1)),
            pl.BlockSpec((None, d, tn), lambda i, j: (layer, 0, j)),
            pl.BlockSpec((None, 1, tn), lambda i, j: (layer, 0, j)),
            pl.BlockSpec((None, 2, d, LANES), lambda i, j: (layer, 0, 0, 0)),
            pl.BlockSpec((None, 2, 1, LANES), lambda i, j: (layer, 0, 0, 0)),
        ],
        out_specs=[
            pl.BlockSpec((tm, tn), lambda i, j: (i, j)),
            pl.BlockSpec((2, tm, LANES), lambda i, j: (0, i, 0)),
        ],
        out_shape=[
            jax.ShapeDtypeStruct((t, nc), BF16),
            jax.ShapeDtypeStruct((2, t, LANES), F32),
        ],
        scratch_shapes=[pltpu.VMEM((tm, d), BF16)],
        compiler_params=_cparams(("arbitrary", "arbitrary")),
        name="inproj",
    )(xs, g, mod, mod, w_main, b_main, w_gates, b_gates)


CONV_ROWS = 256
HALO_ROWS = 16


def _conv_kernel(cur_ref, prev_ref, next_ref, w_ref, b_ref, o_ref, *, n_lat_rows, lat_len, ctx_len, k_col_block):
    i = pl.program_id(0)
    j = pl.program_id(1)
    row0 = i * CONV_ROWS
    in_lat = row0 < n_lat_rows
    seg_len = jnp.where(in_lat, lat_len, ctx_len)
    off = jnp.where(in_lat, row0, row0 - n_lat_rows) % seg_len
    has_prev = (off != 0).astype(F32)
    has_next = (off + CONV_ROWS != seg_len).astype(F32)

    x = cur_ref[...].astype(F32)
    prev_row = prev_ref[HALO_ROWS - 1:HALO_ROWS, :].astype(F32) * has_prev
    next_row = next_ref[0:1, :].astype(F32) * has_next
    rows = lax.broadcasted_iota(jnp.int32, x.shape, 0)
    xm1 = jnp.where(rows == 0, prev_row, pltpu.roll(x, 1, 0))
    xp1 = jnp.where(rows == CONV_ROWS - 1, next_row, pltpu.roll(x, CONV_ROWS - 1, 0))
    w = w_ref[...]
    y = w[0:1, :] * xm1 + w[1:2, :] * x + w[2:3, :] * xp1 + b_ref[...]
    y = y * jax.nn.sigmoid(y)
    scale = jnp.where(j >= k_col_block, MLSTM_QK_DIM ** -0.5, 1.0).astype(F32)
    o_ref[...] = (y * scale).astype(o_ref.dtype)


def _conv(proj, conv_w, conv_b, qk_off, n_lat_rows, lat_len, ctx_len):
    t = proj.shape[0]
    width = conv_w.shape[1]
    tc = _pick(width // 2, (512, 256, 128))
    assert qk_off % tc == 0 and lat_len % CONV_ROWS == 0 and ctx_len % CONV_ROWS == 0
    cb = qk_off // tc
    halo_per_blk = CONV_ROWS // HALO_ROWS
    n_halo = t // HALO_ROWS
    return pl.pallas_call(
        functools.partial(_conv_kernel, n_lat_rows=n_lat_rows, lat_len=lat_len, ctx_len=ctx_len,
                          k_col_block=(width // 2) // tc),
        grid=(t // CONV_ROWS, width // tc),
        in_specs=[
            pl.BlockSpec((CONV_ROWS, tc), lambda i, j: (i, cb + j)),
            pl.BlockSpec((HALO_ROWS, tc), lambda i, j: (jnp.maximum(i * halo_per_blk - 1, 0), cb + j)),
            pl.BlockSpec((HALO_ROWS, tc), lambda i, j: (jnp.minimum((i + 1) * halo_per_blk, n_halo - 1), cb + j)),
            pl.BlockSpec((3, tc), lambda i, j: (0, j)),
            pl.BlockSpec((1, tc), lambda i, j: (0, j)),
        ],
        out_specs=pl.BlockSpec((CONV_ROWS, tc), lambda i, j: (i, j)),
        out_shape=jax.ShapeDtypeStruct((t, width), BF16),
        compiler_params=_cparams(("arbitrary", "arbitrary")),
        name="qk_conv",
    )(proj, proj, proj, conv_w, conv_b.reshape(1, width))


ROPE_ROWS = 256
ROPE_HEAD_GROUP = 4


def _rope_kernel(q_ref, k_ref, cos_ref, s1_ref, s2_ref, gq_ref, gk_ref, qo_ref, ko_ref, *, n_q_heads, n_k_heads):
    cos = cos_ref[...]
    s1 = s1_ref[...]
    s2 = s2_ref[...]

    def prep_heads(src_ref, dst_ref, g, n_heads, scale):
        for h0 in range(0, n_heads, ROPE_HEAD_GROUP):
            sls = [slice(h * HEAD_DIM, (h + 1) * HEAD_DIM) for h in range(h0, min(h0 + ROPE_HEAD_GROUP, n_heads))]
            xs = [src_ref[:, sl].astype(F32) for sl in sls]
            inv = [lax.rsqrt(jnp.mean(x * x, axis=-1, keepdims=True) + EPS) for x in xs]
            xn = [x * r * g for x, r in zip(xs, inv)]
            up = [pltpu.roll(x, HEAD_DIM - ROPE_PAIRS, 1) for x in xn]
            dn = [pltpu.roll(x, ROPE_PAIRS, 1) for x in xn]
            for sl, x, u, dwn in zip(sls, xn, up, dn):
                y = x * cos + u * s1 + dwn * s2
                dst_ref[:, sl] = (y * scale if scale != 1.0 else y).astype(dst_ref.dtype)

    prep_heads(q_ref, qo_ref, gq_ref[...], n_q_heads, ATTN_SCALE)
    prep_heads(k_ref, ko_ref, gk_ref[...], n_k_heads, 1.0)


def _rope(proj, tabs, g_q, g_k, aw, kvw, k_off, n_lat_rows):
    t = proj.shape[0]
    assert k_off % kvw == 0
    kb = k_off // kvw
    n_lat_blk = n_lat_rows // ROPE_ROWS
    lat_blk_per_batch = (tabs[0].shape[0] - ROPE_ROWS) // ROPE_ROWS

    def tab_map(i):
        return (jnp.where(i < n_lat_blk, i % lat_blk_per_batch, lat_blk_per_batch), 0)

    tab_spec = pl.BlockSpec((ROPE_ROWS, HEAD_DIM), tab_map)
    return pl.pallas_call(
        functools.partial(_rope_kernel, n_q_heads=aw // HEAD_DIM, n_k_heads=kvw // HEAD_DIM),
        grid=(t // ROPE_ROWS,),
        in_specs=[
            pl.BlockSpec((ROPE_ROWS, aw), lambda i: (i, 0)),
            pl.BlockSpec((ROPE_ROWS, kvw), lambda i: (i, kb)),
            tab_spec, tab_spec, tab_spec,
            pl.BlockSpec((1, HEAD_DIM), lambda i: (0, 0)),
            pl.BlockSpec((1, HEAD_DIM), lambda i: (0, 0)),
        ],
        out_specs=[
            pl.BlockSpec((ROPE_ROWS, aw), lambda i: (i, 0)),
            pl.BlockSpec((ROPE_ROWS, kvw), lambda i: (i, 0)),
        ],
        out_shape=[jax.ShapeDtypeStruct((t, aw), BF16), jax.ShapeDtypeStruct((t, kvw), BF16)],
        compiler_params=_cparams(("arbitrary",)),
        name="qk_norm_rope",
    )(proj, proj, tabs[0], tabs[1], tabs[2], g_q.reshape(1, HEAD_DIM), g_k.reshape(1, HEAD_DIM))


def _rope_tables(n_lat):
    rows = n_lat // GRID_W
    inv_freq = ROPE_THETA ** (-jnp.arange(ROPE_PAIRS, dtype=F32) / ROPE_PAIRS)
    row_pos = jnp.repeat(jnp.arange(rows, dtype=F32), GRID_W)
    col_pos = jnp.tile(jnp.arange(GRID_W, dtype=F32), rows)
    ang_r = row_pos[:, None] * inv_freq
    ang_c = col_pos[:, None] * inv_freq
    zeros = jnp.zeros_like(ang_r)
    cos = jnp.concatenate([jnp.cos(ang_r), jnp.cos(ang_r), jnp.cos(ang_c), jnp.cos(ang_c)], axis=-1)
    s1 = jnp.concatenate([-jnp.sin(ang_r), zeros, -jnp.sin(ang_c), zeros], axis=-1)
    s2 = jnp.concatenate([zeros, jnp.sin(ang_r), zeros, jnp.sin(ang_c)], axis=-1)
    ident = jnp.ones((ROPE_ROWS, HEAD_DIM), F32)
    zpad = jnp.zeros((ROPE_ROWS, HEAD_DIM), F32)
    return (jnp.concatenate([cos, ident], 0), jnp.concatenate([s1, zpad], 0), jnp.concatenate([s2, zpad], 0))


def _attn_kernel(sink_ref, q_ref, kp_ref, kc_ref, kn_ref, kx_ref, vp_ref, vc_ref, vn_ref, vx_ref, o_ref,
                 *, n_lat_blk, n_kv, group, ctx_len):
    n = pl.program_id(1)
    is_ctx = n >= n_lat_blk
    n_band = 3 * QBLK
    n_keys = n_band + ctx_len
    qi = lax.broadcasted_iota(jnp.int32, (QBLK, n_keys), 0)
    kj = lax.broadcasted_iota(jnp.int32, (QBLK, n_keys), 1)
    rel = kj - QBLK - qi
    kpos = n * QBLK + kj - QBLK
    band_ok = (jnp.abs(rel) <= WINDOW) & (kpos >= 0) & (kpos < n_lat_blk * QBLK) & jnp.logical_not(is_ctx)
    valid = band_ok | (kj >= n_band)

    for hk in range(n_kv):
        ksl = slice(hk * HEAD_DIM, (hk + 1) * HEAD_DIM)
        k_all = jnp.concatenate([kp_ref[:, ksl], kc_ref[:, ksl], kn_ref[:, ksl], kx_ref[:, ksl]], axis=0)
        v_all = jnp.concatenate([vp_ref[:, ksl], vc_ref[:, ksl], vn_ref[:, ksl], vx_ref[:, ksl]], axis=0)
        heads = [hk * group + g for g in range(group)]
        qsl = {h: slice(h * HEAD_DIM, (h + 1) * HEAD_DIM) for h in heads}
        s = {h: jnp.where(valid, lax.dot_general(q_ref[:, qsl[h]], k_all, (((1,), (1,)), ((), ())),
                                                 preferred_element_type=F32), NEG) for h in heads}
        m = {h: jnp.maximum(jnp.max(s[h], axis=-1, keepdims=True), sink_ref[h]) for h in heads}
        p = {h: jnp.exp(s[h] - m[h]) for h in heads}
        denom = {h: jnp.sum(p[h], axis=-1, keepdims=True) + jnp.exp(sink_ref[h] - m[h]) for h in heads}
        o = {h: jnp.dot(p[h].astype(BF16), v_all, preferred_element_type=F32) for h in heads}
        for h in heads:
            o_ref[:, qsl[h]] = (o[h] / denom[h]).astype(o_ref.dtype)


def _attention(sink, qr, kr, proj, aw, kvw, v_off, n_batch, lat_len, ctx_len, with_ctx):
    t = proj.shape[0]
    assert v_off % kvw == 0
    vb = v_off // kvw
    n_lat_blk = lat_len // QBLK
    n_ctx_blk = ctx_len // QBLK
    n_lat_rows = n_batch * lat_len
    nblk = n_lat_blk + (n_ctx_blk if with_ctx else 0)

    def qrow(b, n):
        return jnp.where(n < n_lat_blk, b * n_lat_blk + n, n_lat_rows // QBLK + b * n_ctx_blk + (n - n_lat_blk))

    def band(delta):
        def f(b, n):
            nn = jnp.clip(jnp.minimum(n, n_lat_blk - 1) + delta, 0, n_lat_blk - 1)
            return b * n_lat_blk + nn
        return f

    def ctx_row(b, n):
        return n_lat_rows // ctx_len + b

    def kspec(rowf):
        return pl.BlockSpec((QBLK, kvw), lambda b, n: (rowf(b, n), 0))

    def vspec(rowf):
        return pl.BlockSpec((QBLK, kvw), lambda b, n: (rowf(b, n), vb))

    return pl.pallas_call(
        functools.partial(_attn_kernel, n_lat_blk=n_lat_blk, n_kv=kvw // HEAD_DIM,
                          group=aw // kvw, ctx_len=ctx_len),
        grid=(n_batch, nblk),
        in_specs=[
            pl.BlockSpec(memory_space=pltpu.SMEM),
            pl.BlockSpec((QBLK, aw), lambda b, n: (qrow(b, n), 0)),
            kspec(band(-1)), kspec(band(0)), kspec(band(1)),
            pl.BlockSpec((ctx_len, kvw), lambda b, n: (ctx_row(b, n), 0)),
            vspec(band(-1)), vspec(band(0)), vspec(band(1)),
            pl.BlockSpec((ctx_len, kvw), lambda b, n: (ctx_row(b, n), vb)),
        ],
        out_specs=pl.BlockSpec((QBLK, aw), lambda b, n: (qrow(b, n), 0)),
        out_shape=jax.ShapeDtypeStruct((n_lat_rows + (n_batch * ctx_len if with_ctx else 0), aw), BF16),
        compiler_params=_cparams(("arbitrary", "arbitrary")),
        name="attention",
    )(sink, qr, kr, kr, kr, kr, proj, proj, proj, proj)


HEAD_GROUP = 8


def _mlstm_kernel(q_ref, k_ref, v_ref, g_ref, o_ref, *state, n_heads):
    c_scrs, nm_scrs = state[:n_heads], state[n_heads:]
    d = pl.program_id(0)
    c = pl.program_id(2)
    L = MLSTM_CHUNK
    dk = MLSTM_QK_DIM
    dv = MLSTM_V_DIM

    @pl.when(c == 0)
    def _():
        for scr in state:
            scr[...] = jnp.zeros_like(scr)

    r = lax.broadcasted_iota(jnp.int32, (L, L), 0)
    s = lax.broadcasted_iota(jnp.int32, (L, L), 1)
    fwd = d == 0
    lag = (r - s) * (1 - 2 * d)
    tri = lag >= 0
    tri_t = lag <= 0
    tri_b = tri.astype(BF16)
    tri_tb = tri_t.astype(BF16)

    gates = g_ref[...]
    logf = jnp.minimum(gates, 0.0) - jnp.log1p(jnp.exp(-jnp.abs(gates)))
    gates_t = gates.T
    logf_t = logf.T

    def split_dot_l(mat_b, x):
        hi = x.astype(BF16)
        lo = (x - hi.astype(F32)).astype(BF16)
        return jnp.dot(mat_b, hi, preferred_element_type=F32) + jnp.dot(mat_b, lo, preferred_element_type=F32)

    def split_dot_r(x, mat_b):
        hi = x.astype(BF16)
        lo = (x - hi.astype(F32)).astype(BF16)
        return jnp.dot(hi, mat_b, preferred_element_type=F32) + jnp.dot(lo, mat_b, preferred_element_type=F32)

    cum_col = split_dot_l(tri_b, logf)
    cum_row = split_dot_r(logf_t, tri_tb)
    end_col = jnp.where(fwd, cum_col[L - 1:L, :], cum_col[0:1, :])

    nt = (((1,), (1,)), ((), ()))
    for h0 in range(0, n_heads, HEAD_GROUP):
        hs = range(h0, min(h0 + HEAD_GROUP, n_heads))
        qs = {h: q_ref[:, h * dk:(h + 1) * dk] for h in hs}
        ks = {h: k_ref[:, h * dk:(h + 1) * dk] for h in hs}
        vs = {h: v_ref[:, h * dv:(h + 1) * dv] for h in hs}
        b_col = {h: cum_col[:, n_heads + h:n_heads + h + 1] for h in hs}
        b_row = {h: cum_row[n_heads + h:n_heads + h + 1, :] for h in hs}
        i_col = {h: gates[:, h:h + 1] for h in hs}
        i_row = {h: gates_t[h:h + 1, :] for h in hs}
        b_end = {h: end_col[:, n_heads + h:n_heads + h + 1] for h in hs}
        n_prev = {h: nm_scrs[h][0:1, :] for h in hs}
        m_prev = {h: nm_scrs[h][1:2, 0:1] for h in hs}
        ct_prev = {h: c_scrs[h][...] for h in hs}

        dmat = {h: jnp.where(tri, b_col[h] - b_row[h] + i_row[h], NEG) for h in hs}
        m_inter = {h: b_col[h] + m_prev[h] for h in hs}
        m_t = {h: jnp.maximum(m_inter[h], jnp.max(dmat[h], axis=-1, keepdims=True)) for h in hs}
        qk = {h: lax.dot_general(qs[h], ks[h], nt, preferred_element_type=F32) for h in hs}
        qc = {h: jnp.dot(qs[h], ct_prev[h].astype(BF16), preferred_element_type=F32) for h in hs}
        qn_prev = {h: jnp.sum(qs[h].astype(F32) * n_prev[h], axis=-1, keepdims=True) for h in hs}
        a = {h: jnp.exp(m_inter[h] - m_t[h]) for h in hs}
        smat = {h: qk[h] * jnp.exp(dmat[h] - m_t[h]) for h in hs}
        sv = {h: jnp.dot(smat[h].astype(BF16), vs[h], preferred_element_type=F32) for h in hs}
        qn = {h: jnp.sum(smat[h], axis=-1, keepdims=True) + a[h] * qn_prev[h] for h in hs}

        g_row = {h: b_end[h] - b_row[h] + i_row[h] for h in hs}
        m_new = {h: jnp.maximum(b_end[h] + m_prev[h], jnp.max(g_row[h], axis=-1, keepdims=True)) for h in hs}
        kw = {h: ks[h].astype(F32) * jnp.exp(b_end[h] - b_col[h] + i_col[h] - m_new[h]) for h in hs}
        a_end = {h: jnp.exp(b_end[h] + m_prev[h] - m_new[h]) for h in hs}
        kv = {h: jnp.dot(kw[h].T.astype(BF16), vs[h], preferred_element_type=F32) for h in hs}

        for h in hs:
            hout = (sv[h] + a[h] * qc[h]) / jnp.maximum(jnp.abs(qn[h]), jnp.exp(-m_t[h]))
            o_ref[:, h * dv:(h + 1) * dv] = hout.astype(o_ref.dtype)
        for h in hs:
            c_scrs[h][...] = a_end[h] * ct_prev[h] + kv[h]
            nm_scrs[h][0:1, :] = a_end[h] * n_prev[h] + jnp.sum(kw[h], axis=0, keepdims=True)
            nm_scrs[h][1:2, :] = jnp.broadcast_to(m_new[h], (1, dk))


def _mlstm(qk, proj, gates, mqk, mw, v_off, n_batch, lat_len, ctx_len):
    t = proj.shape[0]
    L = MLSTM_CHUNK
    n_heads = mw // MLSTM_V_DIM
    assert v_off % mw == 0 and mqk == n_heads * MLSTM_QK_DIM
    vb = v_off // mw
    n_ctx = ctx_len // L
    n_lat = lat_len // L
    lat_blocks = n_batch * n_lat

    def row(d, b, c):
        cc = jnp.where(d == 0, c, n_ctx - 1 - c)
        lc = jnp.where(d == 0, c - n_ctx, n_lat - 1 - (c - n_ctx))
        return jnp.where(c < n_ctx, lat_blocks + b * n_ctx + cc, b * n_lat + lc)

    return pl.pallas_call(
        functools.partial(_mlstm_kernel, n_heads=n_heads),
        grid=(2, n_batch, n_ctx + n_lat),
        in_specs=[
            pl.BlockSpec((L, mqk), lambda d, b, c: (row(d, b, c), 0)),
            pl.BlockSpec((L, mqk), lambda d, b, c: (row(d, b, c), 1)),
            pl.BlockSpec((L, mw), lambda d, b, c: (row(d, b, c), vb)),
            pl.BlockSpec((None, L, LANES), lambda d, b, c: (d, row(d, b, c), 0)),
        ],
        out_specs=pl.BlockSpec((None, L, mw), lambda d, b, c: (d, row(d, b, c), 0)),
        out_shape=jax.ShapeDtypeStruct((2, t, mw), BF16),
        scratch_shapes=([pltpu.VMEM((MLSTM_QK_DIM, MLSTM_V_DIM), F32)] * n_heads
                        + [pltpu.VMEM((8, MLSTM_QK_DIM), F32)] * n_heads),
        compiler_params=_cparams(("arbitrary", "arbitrary", "arbitrary")),
        name="mlstm",
    )(qk, qk, proj, gates)


def _branch_kernel(attn_ref, hfb_ref, mo_ref, gmh_ref, wa_ref, wm_ref, ga_ref, gm_ref, o_ref, hm_scr, *, n_heads):
    j = pl.program_id(1)

    @pl.when(j == 0)
    def _():
        dv = MLSTM_V_DIM
        for h in range(n_heads):
            sl = slice(h * dv, (h + 1) * dv)
            hsum = hfb_ref[0, :, sl].astype(F32) + hfb_ref[1, :, sl].astype(F32)
            x = jax.nn.sigmoid(mo_ref[:, sl].astype(F32)) * hsum
            y = x * lax.rsqrt(jnp.mean(x * x, axis=-1, keepdims=True) + EPS) * gmh_ref[:, sl]
            hm_scr[:, sl] = y.astype(BF16)

    ya = jnp.dot(attn_ref[...], wa_ref[...], preferred_element_type=F32)
    ym = jnp.dot(hm_scr[...], wm_ref[...], preferred_element_type=F32)
    u = jax.nn.sigmoid(ga_ref[...].astype(F32)) * ya + jax.nn.sigmoid(gm_ref[...].astype(F32)) * ym
    o_ref[...] = u.astype(o_ref.dtype)


def _branch(attn, hfb, proj, g_mh, wa, wm, layer, mo_off, ga_off, gm_off, rows, tm):
    aw = attn.shape[1]
    mw = hfb.shape[2]
    d = wa.shape[2]
    tn = _pick(d, (1024, 512, 256, 128))
    assert mo_off % mw == 0 and ga_off % tn == 0 and gm_off % tn == 0
    mob, gab, gmb = mo_off // mw, ga_off // tn, gm_off // tn
    return pl.pallas_call(
        functools.partial(_branch_kernel, n_heads=mw // MLSTM_V_DIM),
        grid=(rows // tm, d // tn),
        in_specs=[
            pl.BlockSpec((tm, aw), lambda i, j: (i, 0)),
            pl.BlockSpec((2, tm, mw), lambda i, j: (0, i, 0)),
            pl.BlockSpec((tm, mw), lambda i, j: (i, mob)),
            pl.BlockSpec((1, mw), lambda i, j: (0, 0)),
            pl.BlockSpec((None, aw, tn), lambda i, j: (layer, 0, j)),
            pl.BlockSpec((None, mw, tn), lambda i, j: (layer, 0, j)),
            pl.BlockSpec((tm, tn), lambda i, j: (i, gab + j)),
            pl.BlockSpec((tm, tn), lambda i, j: (i, gmb + j)),
        ],
        out_specs=pl.BlockSpec((tm, tn), lambda i, j: (i, j)),
        out_shape=jax.ShapeDtypeStruct((rows, d), BF16),
        scratch_shapes=[pltpu.VMEM((tm, mw), BF16)],
        compiler_params=_cparams(("arbitrary", "arbitrary")),
        name="branch_merge",
    )(attn, hfb, proj, g_mh.reshape(1, mw), wa, wm, proj, proj)


def _outproj_kernel(u_ref, w_ref, x_ref, gt_ref, o_ref, *, seg_args):
    seg = _seg_of_block(pl.program_id(0), *seg_args)
    y = jnp.dot(u_ref[...], w_ref[...], preferred_element_type=F32)
    o_ref[...] = x_ref[...] + gt_ref[pl.ds(seg, 1), :] * y


def _outproj(u, w_out, xs, mod, layer, tm, seg_args):
    rows, d = u.shape
    tn = _pick(d, (1024, 512, 256, 128))
    gate_blk = 2 * (d // tn)
    return pl.pallas_call(
        functools.partial(_outproj_kernel, seg_args=seg_args),
        grid=(rows // tm, d // tn),
        in_specs=[
            pl.BlockSpec((tm, d), lambda i, j: (i, 0)),
            pl.BlockSpec((None, d, tn), lambda i, j: (layer, 0, j)),
            pl.BlockSpec((tm, tn), lambda i, j: (i, j)),
            pl.BlockSpec((None, MOD_ROWS, tn), lambda i, j: (layer, 0, gate_blk + j)),
        ],
        out_specs=pl.BlockSpec((tm, tn), lambda i, j: (i, j)),
        out_shape=jax.ShapeDtypeStruct(xs.shape, F32),
        input_output_aliases={2: 0},
        compiler_params=_cparams(("arbitrary", "arbitrary")),
        name="outproj_residual",
    )(u, w_out, xs, mod)


def _pack_bf16_pairs(h):
    half = h.shape[1] // 2
    hi = pltpu.bitcast(h[:, :half].astype(BF16).astype(F32), jnp.uint32)
    lo = pltpu.bitcast(h[:, half:].astype(BF16).astype(F32), jnp.uint32)
    return hi | (lo >> 16)


def _unpack_bf16_pairs(p):
    hi = pltpu.bitcast(p & jnp.uint32(0xFFFF0000), F32)
    lo = pltpu.bitcast(p << 16, F32)
    return hi, lo


ROUTER_ROWS = 256


def _router_kernel(x_ref, g_ref, sh_ref, sc_ref, wr_ref, br_ref, hp_ref, idx_ref, wt_ref, *, seg_args, n_experts):
    seg = _seg_of_block(pl.program_id(0), *seg_args)
    h = _modulated(x_ref[...], g_ref[...], sc_ref[pl.ds(seg, 1), :], sh_ref[pl.ds(seg, 1), :])
    hp_ref[...] = _pack_bf16_pairs(h)

    wr = wr_ref[...]
    h_hi = h.astype(BF16)
    h_lo = (h - h_hi.astype(F32)).astype(BF16)
    w_hi = wr.astype(BF16)
    w_lo = (wr - w_hi.astype(F32)).astype(BF16)
    nt = (((1,), (1,)), ((), ()))
    logits = (lax.dot_general(w_hi, h_hi, nt, preferred_element_type=F32)
              + lax.dot_general(w_hi, h_lo, nt, preferred_element_type=F32)
              + lax.dot_general(w_lo, h_hi, nt, preferred_element_type=F32))
    aff = jax.nn.sigmoid(logits)
    biased = aff + br_ref[...]
    rb = [biased[e:e + 1, :] for e in range(n_experts)]
    ra = [aff[e:e + 1, :] for e in range(n_experts)]

    epg = EXPERTS_PER_GROUP
    scores = []
    for g in range(N_GROUPS):
        a, b, c, d = rb[epg * g:epg * g + epg]
        hi1, lo1 = jnp.maximum(a, b), jnp.minimum(a, b)
        hi2, lo2 = jnp.maximum(c, d), jnp.minimum(c, d)
        scores.append(jnp.maximum(hi1, hi2) + jnp.maximum(jnp.minimum(hi1, hi2), jnp.maximum(lo1, lo2)))
    best = jnp.zeros(scores[0].shape, jnp.int32)
    best_s = scores[0]
    for g in range(1, N_GROUPS):
        upd = scores[g] > best_s
        best = jnp.where(upd, g, best)
        best_s = jnp.where(upd, scores[g], best_s)

    vb, va = [], []
    for j in range(epg):
        xb, xa = rb[j], ra[j]
        for g in range(1, N_GROUPS):
            sel = best == g
            xb = jnp.where(sel, rb[epg * g + j], xb)
            xa = jnp.where(sel, ra[epg * g + j], xa)
        vb.append(xb)
        va.append(xa)

    i1 = jnp.zeros_like(best)
    m1, a1 = vb[0], va[0]
    for j in range(1, epg):
        upd = vb[j] > m1
        i1 = jnp.where(upd, j, i1)
        m1 = jnp.where(upd, vb[j], m1)
        a1 = jnp.where(upd, va[j], a1)
    i2 = jnp.zeros_like(best)
    m2 = jnp.full_like(m1, -jnp.inf)
    a2 = jnp.zeros_like(a1)
    for j in range(epg):
        upd = (i1 != j) & (vb[j] > m2)
        i2 = jnp.where(upd, j, i2)
        m2 = jnp.where(upd, vb[j], m2)
        a2 = jnp.where(upd, va[j], a2)

    idx_ref[0:1, :] = best * epg + i1
    idx_ref[1:2, :] = best * epg + i2
    tot = a1 + a2
    wt_ref[0:1, :] = a1 / tot
    wt_ref[1:2, :] = a2 / tot


def _router(xs, g, mod, layer, w_router_t, b_router, rows, tm, seg_args):
    d = xs.shape[1]
    e = w_router_t.shape[0]
    assert e == N_GROUPS * EXPERTS_PER_GROUP
    return pl.pallas_call(
        functools.partial(_router_kernel, seg_args=seg_args, n_experts=e),
        grid=(rows // tm,),
        in_specs=[
            pl.BlockSpec((tm, d), lambda i: (i, 0)),
            pl.BlockSpec((1, d), lambda i: (0, 0)),
            pl.BlockSpec((None, MOD_ROWS, d), lambda i: (layer, 0, 3)),
            pl.BlockSpec((None, MOD_ROWS, d), lambda i: (layer, 0, 4)),
            pl.BlockSpec((e, d), lambda i: (0, 0)),
            pl.BlockSpec((e, 1), lambda i: (0, 0)),
        ],
        out_specs=[
            pl.BlockSpec((tm, d // 2), lambda i: (i, 0)),
            pl.BlockSpec((2, tm), lambda i: (0, i)),
            pl.BlockSpec((2, tm), lambda i: (0, i)),
        ],
        out_shape=[
            jax.ShapeDtypeStruct((rows, d // 2), jnp.uint32),
            jax.ShapeDtypeStruct((2, rows), jnp.int32),
            jax.ShapeDtypeStruct((2, rows), F32),
        ],
        compiler_params=_cparams(("arbitrary",)),
        name="ffn_modulate_route",
    )(xs, g, mod, mod, w_router_t, b_router.reshape(e, 1))


def _route_kernel(idx_ref, pos_ref, te_ref, nu_ref, *, n_experts, tm, n_chunks):
    e_iota = lax.broadcasted_iota(jnp.int32, (n_experts, LANES), 0)
    idx_all = idx_ref[...]

    def count_col(k):
        col = jnp.zeros((n_experts, LANES), F32)
        for e in range(n_experts):
            col = jnp.where(e_iota == e, jnp.sum((idx_all[k] == e).astype(F32)), col)
        return col

    c0 = count_col(0)
    counts = c0 + count_col(1)
    tiles_per = jnp.floor((counts + (tm - 1)) * (1.0 / tm))
    tile_end = tiles_per
    s = 1
    while s < n_experts:
        tile_end = tile_end + jnp.where(e_iota >= s, pltpu.roll(tile_end, s, 0), 0.0)
        s *= 2
    row_off = (tile_end - tiles_per) * tm

    r = lax.broadcasted_iota(jnp.int32, (LANES, LANES), 0)
    c = lax.broadcasted_iota(jnp.int32, (LANES, LANES), 1)
    triu = (r <= c).astype(BF16)

    def body(ch, carry):
        new = []
        for k in range(2):
            onehot = (e_iota == idx_ref[k, pl.ds(ch, 1), :]).astype(F32)
            csum = jnp.dot(onehot.astype(BF16), triu, preferred_element_type=F32)
            posv = jnp.sum(onehot * (row_off + carry[k] + csum - 1.0), axis=0, keepdims=True)
            pos_ref[k, pl.ds(ch, 1), :] = posv.astype(jnp.int32)
            new.append(carry[k] + csum[:, LANES - 1:LANES])
        return tuple(new)

    lax.fori_loop(0, n_chunks, body, (jnp.zeros((n_experts, LANES), F32), c0))

    t_iota = lax.broadcasted_iota(jnp.int32, (n_experts, LANES), 1).astype(F32)
    te = jnp.sum((tile_end <= t_iota).astype(F32), axis=0, keepdims=True)
    te_ref[...] = jnp.minimum(te, n_experts - 1.0).astype(jnp.int32)
    nu_ref[...] = tile_end[n_experts - 1:n_experts, :].astype(jnp.int32)


def _route(idx, n_experts, tm):
    k, rows = idx.shape
    assert k == 2 and rows % LANES == 0 and tm & (tm - 1) == 0
    n_chunks = rows // LANES
    n_tiles = (k * rows) // tm + n_experts
    assert n_tiles <= LANES
    pos, te, nu = pl.pallas_call(
        functools.partial(_route_kernel, n_experts=n_experts, tm=tm, n_chunks=n_chunks),
        out_shape=[
            jax.ShapeDtypeStruct((k, n_chunks, LANES), jnp.int32),
            jax.ShapeDtypeStruct((1, LANES), jnp.int32),
            jax.ShapeDtypeStruct((1, LANES), jnp.int32),
        ],
        compiler_params=pltpu.CompilerParams(vmem_limit_bytes=VMEM_LIMIT),
        name="moe_route",
    )(idx.reshape(k, n_chunks, LANES))
    return pos.reshape(k * rows), te[0, :n_tiles], nu[0, :1], n_tiles


DISPATCH_ROWS = 256


def _dispatch_kernel(pos_ref, hp_ref, init_ref, hs_ref, sem, *, n_rows):
    del init_ref
    base = pl.program_id(0) * DISPATCH_ROWS

    def row_copy(k, r):
        return pltpu.make_async_copy(hp_ref.at[pl.ds(r, 1)], hs_ref.at[pl.ds(pos_ref[k * n_rows + base + r], 1)], sem)

    def start(r, carry):
        row_copy(0, r).start()
        row_copy(1, r).start()
        return carry

    def wait(r, carry):
        row_copy(0, r).wait()
        row_copy(1, r).wait()
        return carry

    lax.fori_loop(0, DISPATCH_ROWS, start, 0)
    lax.fori_loop(0, DISPATCH_ROWS, wait, 0)


def _dispatch(pos, hp, n_sorted_rows):
    rows, half = hp.shape
    return pl.pallas_call(
        functools.partial(_dispatch_kernel, n_rows=rows),
        grid_spec=pltpu.PrefetchScalarGridSpec(
            num_scalar_prefetch=1,
            grid=(rows // DISPATCH_ROWS,),
            in_specs=[
                pl.BlockSpec((DISPATCH_ROWS, half), lambda i, pos: (i, 0)),
                pl.BlockSpec(memory_space=pl.ANY),
            ],
            out_specs=pl.BlockSpec(memory_space=pl.ANY),
            scratch_shapes=[pltpu.SemaphoreType.DMA],
        ),
        out_shape=jax.ShapeDtypeStruct((n_sorted_rows, half), hp.dtype),
        input_output_aliases={2: 0},
        compiler_params=_cparams(("arbitrary",)),
        name="moe_dispatch",
    )(pos, hp, jnp.zeros((n_sorted_rows, half), hp.dtype))


def _expert_kernel(te_ref, nused_ref, x_ref, wg_ref, wu_ref, wd_ref, o_ref, xs_scr, acc_scr):
    i = pl.program_id(0)
    f = pl.program_id(1)

    @pl.when((i >= nused_ref[0]) & (f == 0))
    def _():
        o_ref[...] = jnp.zeros_like(o_ref)

    @pl.when(i < nused_ref[0])
    def _():
        half = x_ref.shape[1]

        @pl.when(f == 0)
        def _():
            def rows_body(r, carry):
                rs = pl.ds(pl.multiple_of(r * ROW_CHUNK, ROW_CHUNK), ROW_CHUNK)
                hi, lo = _unpack_bf16_pairs(x_ref[rs, :])
                xs_scr[rs, pl.ds(0, half)] = hi.astype(BF16)
                xs_scr[rs, pl.ds(half, half)] = lo.astype(BF16)
                return carry

            lax.fori_loop(0, x_ref.shape[0] // ROW_CHUNK, rows_body, 0)
            acc_scr[...] = jnp.zeros_like(acc_scr)

        xs = xs_scr[...]
        gate = jnp.dot(xs, wg_ref[...], preferred_element_type=F32)
        up = jnp.dot(xs, wu_ref[...], preferred_element_type=F32)
        act = (gate * jax.nn.sigmoid(gate) * up).astype(BF16)
        acc_scr[...] += jnp.dot(act, wd_ref[...], preferred_element_type=F32)

        @pl.when(f == pl.num_programs(1) - 1)
        def _():
            def rows_body(r, carry):
                rs = pl.ds(pl.multiple_of(r * ROW_CHUNK, ROW_CHUNK), ROW_CHUNK)
                o_ref[rs, :] = _pack_bf16_pairs(acc_scr[rs, :])
                return carry

            lax.fori_loop(0, o_ref.shape[0] // ROW_CHUNK, rows_body, 0)


def _experts(tile_expert, n_used, hs, wg, wu, wd, layer, tm):
    p, half = hs.shape
    _, e, d, ff = wg.shape
    fc = _pick(ff, (256, 128))
    n_tiles = p // tm

    def row(i, f, te, nu):
        return (jnp.minimum(i, nu[0] - 1), 0)

    return pl.pallas_call(
        _expert_kernel,
        grid_spec=pltpu.PrefetchScalarGridSpec(
            num_scalar_prefetch=2,
            grid=(n_tiles, ff // fc),
            in_specs=[
                pl.BlockSpec((tm, half), row),
                pl.BlockSpec((None, None, d, fc), lambda i, f, te, nu: (layer, te[i], 0, f)),
                pl.BlockSpec((None, None, d, fc), lambda i, f, te, nu: (layer, te[i], 0, f)),
                pl.BlockSpec((None, None, fc, d), lambda i, f, te, nu: (layer, te[i], f, 0)),
            ],
            out_specs=pl.BlockSpec((tm, half), lambda i, f, te, nu: (i, 0)),
            scratch_shapes=[pltpu.VMEM((tm, d), BF16), pltpu.VMEM((tm, d), F32)],
        ),
        out_shape=jax.ShapeDtypeStruct((p, half), jnp.uint32),
        compiler_params=_cparams(("arbitrary", "arbitrary")),
        name="moe_experts",
    )(tile_expert, n_used, hs, wg, wu, wd)


COMBINE_ROWS = 256


def _combine_kernel(pos_ref, x_ref, wt_ref, gt_ref, ys_ref, o_ref, buf, sems, *, seg_args, n_rows):
    i = pl.program_id(0)
    slot = i % 2
    seg = _seg_of_block(i, *seg_args)

    def row_copy(blk, sl, k, r):
        return pltpu.make_async_copy(ys_ref.at[pl.ds(pos_ref[k * n_rows + blk * COMBINE_ROWS + r], 1)],
                                     buf.at[sl, k, pl.ds(r, 1)], sems.at[sl])

    def issue(blk, sl):
        def body(r, carry):
            row_copy(blk, sl, 0, r).start()
            row_copy(blk, sl, 1, r).start()
            return carry

        lax.fori_loop(0, COMBINE_ROWS, body, 0)

    @pl.when(i == 0)
    def _():
        issue(0, 0)

    @pl.when(i + 1 < pl.num_programs(0))
    def _():
        issue(i + 1, 1 - slot)

    def wait(r, carry):
        row_copy(i, slot, 0, r).wait()
        row_copy(i, slot, 1, r).wait()
        return carry

    lax.fori_loop(0, COMBINE_ROWS, wait, 0)

    half = buf.shape[3]
    hi0, lo0 = _unpack_bf16_pairs(buf[slot, 0])
    hi1, lo1 = _unpack_bf16_pairs(buf[slot, 1])
    w0 = wt_ref[:, 0:1]
    w1 = wt_ref[:, 1:2]
    gt = gt_ref[pl.ds(seg, 1), :]
    o_ref[:, :half] = x_ref[:, :half] + gt[:, :half] * (w0 * hi0 + w1 * hi1)
    o_ref[:, half:] = x_ref[:, half:] + gt[:, half:] * (w0 * lo0 + w1 * lo1)


def _combine(pos, xs, wts_t, mod, layer, ys, rows, seg_args):
    d = xs.shape[1]
    half = d // 2
    return pl.pallas_call(
        functools.partial(_combine_kernel, seg_args=seg_args, n_rows=rows),
        grid_spec=pltpu.PrefetchScalarGridSpec(
            num_scalar_prefetch=1,
            grid=(rows // COMBINE_ROWS,),
            in_specs=[
                pl.BlockSpec((COMBINE_ROWS, d), lambda i, pos: (i, 0)),
                pl.BlockSpec((COMBINE_ROWS, 2), lambda i, pos: (i, 0)),
                pl.BlockSpec((None, MOD_ROWS, d), lambda i, pos: (layer, 0, 5)),
                pl.BlockSpec(memory_space=pl.ANY),
            ],
            out_specs=pl.BlockSpec((COMBINE_ROWS, d), lambda i, pos: (i, 0)),
            scratch_shapes=[pltpu.VMEM((2, 2, COMBINE_ROWS, half), jnp.uint32), pltpu.SemaphoreType.DMA((2,))],
        ),
        out_shape=jax.ShapeDtypeStruct((rows, d), F32),
        compiler_params=_cparams(("arbitrary",)),
        name="moe_combine",
    )(pos, xs, wts_t, mod, ys)


def kernel(x, c, ctx, c_ctx, w_ada, b_ada, g_mix, g_ffn, w_in, b_in, g_q, g_k, sink, conv_w, conv_b, g_mh,
           w_br_attn, w_br_mlstm, w_out, w_router, b_router, w_gate, w_up, w_down):
    n_batch, lat_len, d = x.shape
    ctx_len = ctx.shape[1]
    depth = w_ada.shape[0]
    d_in = w_in.shape[2]
    aw = w_br_attn.shape[1]
    mw = w_br_mlstm.shape[1]
    mqk = conv_w.shape[2] // 2
    n_mh = mw // MLSTM_V_DIM
    kvw = (d_in - aw - 2 * mqk - 2 * mw - 4 * n_mh - 2 * d) // 2
    n_experts = w_router.shape[1]
    assert n_batch + 1 <= MOD_ROWS and 2 * n_mh <= LANES

    n_lat_rows = n_batch * lat_len
    n_ctx_rows = n_batch * ctx_len
    tm = _pick(n_ctx_rows, (512, 256))
    assert lat_len % tm == 0
    seg_args = (n_lat_rows // tm, lat_len // tm, n_batch)
    tm_e = 512

    o_aq = 0
    o_ak = o_aq + aw
    o_av = o_ak + kvw
    o_mq = o_av + kvw
    o_mk = o_mq + mqk
    o_mv = o_mk + mqk
    o_mo = o_mv + mw
    o_g = o_mo + mw
    o_ga = o_g + 4 * n_mh
    o_gm = o_ga + d
    order = [(o_aq, aw), (o_mv, mw), (o_mo, mw), (o_ga, d), (o_gm, d), (o_ak, kvw), (o_av, kvw), (o_mq, mqk), (o_mk, mqk)]
    starts = [sum(w for _, w in order[:k]) for k in range(len(order))]
    _, n_mv, n_mo, n_ga, n_gm, n_ak, n_av, n_mq, _ = starts

    xs = jnp.concatenate([x.reshape(n_lat_rows, d), ctx.reshape(n_ctx_rows, d)], axis=0)
    cvec = jnp.zeros((MOD_ROWS, d), F32).at[:n_batch].set(c).at[n_batch].set(c_ctx)
    mod = _adaln(cvec, w_ada, b_ada)
    tabs = _rope_tables(lat_len)
    w_router_t = w_router.T

    w_main = jnp.concatenate([w_in[:, :, o:o + w] for o, w in order], axis=2).astype(BF16)
    b_main = jnp.concatenate([b_in[:, o:o + w] for o, w in order], axis=1).reshape(depth, 1, -1)
    wg = jnp.zeros((depth, 2, d, LANES), F32)
    bg = jnp.zeros((depth, 2, 1, LANES), F32)
    for dr in range(2):
        gsl = slice(o_g + 2 * n_mh * dr, o_g + 2 * n_mh * (dr + 1))
        wg = wg.at[:, dr, :, :2 * n_mh].set(w_in[:, :, gsl])
        bg = bg.at[:, dr, 0, :2 * n_mh].set(b_in[:, gsl])
    wg = wg.astype(BF16)
    wa_b, wm_b, wo_b = w_br_attn.astype(BF16), w_br_mlstm.astype(BF16), w_out.astype(BF16)
    wgate_b, wup_b, wdown_b = w_gate.astype(BF16), w_up.astype(BF16), w_down.astype(BF16)

    for l in range(depth):
        need_ctx = l < depth - 1
        rows = n_lat_rows + (n_ctx_rows if need_ctx else 0)

        proj, gates = _inproj(xs, g_mix[l].reshape(1, d), mod, l, w_main, b_main, wg, bg, tm, seg_args)
        qk = _conv(proj, conv_w[l], conv_b[l], n_mq, n_lat_rows, lat_len, ctx_len)
        hfb = _mlstm(qk, proj, gates, mqk, mw, n_mv, n_batch, lat_len, ctx_len)
        qr, kr = _rope(proj, tabs, g_q[l], g_k[l], aw, kvw, n_ak, n_lat_rows)
        attn = _attention(sink[l], qr, kr, proj, aw, kvw, n_av, n_batch, lat_len, ctx_len, need_ctx)
        u = _branch(attn, hfb, proj, g_mh[l], wa_b, wm_b, l, n_mo, n_ga, n_gm, rows, tm)
        xs = _outproj(u, wo_b, xs, mod, l, tm, seg_args)

        hp, idx, wts = _router(xs, g_ffn[l].reshape(1, d), mod, l, w_router_t, b_router, rows, ROUTER_ROWS,
                               (n_lat_rows // ROUTER_ROWS, lat_len // ROUTER_ROWS, n_batch))
        pos, tile_expert, n_used, n_tiles = _route(idx, n_experts, tm_e)
        hs = _dispatch(pos, hp, n_tiles * tm_e)
        ys = _experts(tile_expert, n_used, hs, wgate_b, wup_b, wdown_b, l, tm_e)
        xs = _combine(pos, xs, wts.T, mod, l, ys, rows, (n_lat_rows // COMBINE_ROWS, lat_len // COMBINE_ROWS, n_batch))

    return xs[:n_lat_rows].reshape(n_batch, lat_len, d)
```

```python
import functools

import jax
import jax.numpy as jnp
from jax import lax
from jax.experimental import pallas as pl
from jax.experimental.pallas import tpu as pltpu

GRID_W = 64
HEAD_DIM = 128
WINDOW = 128
QBLK = 128
ROPE_THETA = 10000.0
ROPE_PAIRS = HEAD_DIM // 4
ATTN_SCALE = HEAD_DIM ** -0.5
MLSTM_QK_DIM = 128
MLSTM_V_DIM = 256
MLSTM_CHUNK = 128
N_GROUPS = 4
EXPERTS_PER_GROUP = 4
EPS = 1e-6
NEG = -1e30

LANES = 128
MOD_ROWS = 8
ROW_CHUNK = 64
VMEM_LIMIT = 56 << 20

F32 = jnp.float32
BF16 = jnp.bfloat16


def _pick(n, cands):
    for c in cands:
        if n % c == 0:
            return c
    raise ValueError(f"no tile in {cands} divides {n}")


def _cparams(sem, vmem=VMEM_LIMIT):
    return pltpu.CompilerParams(dimension_semantics=sem, vmem_limit_bytes=vmem)


def _seg_of_block(i, n_lat_blocks, blocks_per_batch, n_batch):
    return jnp.where(i < n_lat_blocks, i // blocks_per_batch, n_batch)


def _modulated(x, g, sc, sh):
    ms = jnp.mean(x * x, axis=-1, keepdims=True)
    y = x * lax.rsqrt(ms + EPS) * g
    return y * (1.0 + sc) + sh


def _adaln_kernel(c_ref, w_ref, b_ref, o_ref):
    c = c_ref[...]
    cs = (c * jax.nn.sigmoid(c)).astype(BF16)
    o_ref[...] = jnp.dot(cs, w_ref[...].astype(BF16), preferred_element_type=F32) + b_ref[...]


def _adaln(cvec, w_ada, b_ada):
    depth, d, n6 = w_ada.shape
    tn = _pick(n6, (512, 256, 128))
    return pl.pallas_call(
        _adaln_kernel,
        grid=(depth, n6 // tn),
        in_specs=[
            pl.BlockSpec((MOD_ROWS, d), lambda l, j: (0, 0)),
            pl.BlockSpec((None, d, tn), lambda l, j: (l, 0, j)),
            pl.BlockSpec((None, 1, tn), lambda l, j: (l, 0, j)),
        ],
        out_specs=pl.BlockSpec((None, MOD_ROWS, tn), lambda l, j: (l, 0, j)),
        out_shape=jax.ShapeDtypeStruct((depth, MOD_ROWS, n6), F32),
        compiler_params=_cparams(("arbitrary", "arbitrary")),
        name="adaln",
    )(cvec, w_ada, b_ada.reshape(depth, 1, n6))


def _inproj_kernel(x_ref, g_ref, sh_ref, sc_ref, w_ref, b_ref, wg_ref, bg_ref, o_ref, og_ref, h_scr, *, seg_args):
    i = pl.program_id(0)
    j = pl.program_id(1)

    @pl.when(j == 0)
    def _():
        seg = _seg_of_block(i, *seg_args)
        g = g_ref[...]
        sc = sc_ref[pl.ds(seg, 1), :]
        sh = sh_ref[pl.ds(seg, 1), :]

        def rows_body(r, carry):
            rs = pl.ds(pl.multiple_of(r * ROW_CHUNK, ROW_CHUNK), ROW_CHUNK)
            h_scr[rs, :] = _modulated(x_ref[rs, :], g, sc, sh).astype(BF16)
            return carry

        lax.fori_loop(0, x_ref.shape[0] // ROW_CHUNK, rows_body, 0)
        for d in range(2):
            og_ref[d] = jnp.dot(h_scr[...], wg_ref[d], preferred_element_type=F32) + bg_ref[d]

    o_ref[...] = (jnp.dot(h_scr[...], w_ref[...], preferred_element_type=F32) + b_ref[...]).astype(o_ref.dtype)


def _inproj(xs, g, mod, layer, w_main, b_main, w_gates, b_gates, tm, seg_args):
    t, d = xs.shape
    nc = w_main.shape[2]
    tn = _pick(nc, (1024, 512, 256, 128))
    return pl.pallas_call(
        functools.partial(_inproj_kernel, seg_args=seg_args),
        grid=(t // tm, nc // tn),
        in_specs=[
            pl.BlockSpec((tm, d), lambda i, j: (i, 0)),
            pl.BlockSpec((1, d), lambda i, j: (0, 0)),
            pl.BlockSpec((None, MOD_ROWS, d), lambda i, j: (layer, 0, 0)),
            pl.BlockSpec((None, MOD_ROWS, d), lambda i, j: (layer, 0, 1)),
            pl.BlockSpec((None, d, tn), lambda i, j: (0, 0, j)),
            pl.BlockSpec((None, 1, tn), lambda i, j: (layer, 0, j)),
            pl.BlockSpec((None, 2, d, LANES), lambda i, j: (layer, 0, 0, 0)),
            pl.BlockSpec((None, 2, 1, LANES), lambda i, j: (layer, 0, 0, 0)),
        ],
        out_specs=[
            pl.BlockSpec((tm, tn), lambda i, j: (i, j)),
            pl.BlockSpec((2, tm, LANES), lambda i, j: (0, i, 0)),
        ],
        out_shape=[
            jax.ShapeDtypeStruct((t, nc), BF16),
            jax.ShapeDtypeStruct((2, t, LANES), F32),
        ],
        scratch_shapes=[pltpu.VMEM((tm, d), BF16)],
        compiler_params=_cparams(("arbitrary", "arbitrary")),
        name="inproj",
    )(xs, g, mod, mod, w_main, b_main, w_gates, b_gates)


CONV_ROWS = 256
HALO_ROWS = 16


def _conv_kernel(cur_ref, prev_ref, next_ref, w_ref, b_ref, o_ref, *, n_lat_rows, lat_len, ctx_len, k_col_block):
    i = pl.program_id(0)
    j = pl.program_id(1)
    row0 = i * CONV_ROWS
    in_lat = row0 < n_lat_rows
    seg_len = jnp.where(in_lat, lat_len, ctx_len)
    off = jnp.where(in_lat, row0, row0 - n_lat_rows) % seg_len
    has_prev = (off != 0).astype(F32)
    has_next = (off + CONV_ROWS != seg_len).astype(F32)

    x = cur_ref[...].astype(F32)
    prev_row = prev_ref[HALO_ROWS - 1:HALO_ROWS, :].astype(F32) * has_prev
    next_row = next_ref[0:1, :].astype(F32) * has_next
    rows = lax.broadcasted_iota(jnp.int32, x.shape, 0)
    xm1 = jnp.where(rows == 0, prev_row, pltpu.roll(x, 1, 0))
    xp1 = jnp.where(rows == CONV_ROWS - 1, next_row, pltpu.roll(x, CONV_ROWS - 1, 0))
    w = w_ref[...]
    y = w[0:1, :] * xm1 + w[1:2, :] * x + w[2:3, :] * xp1 + b_ref[...]
    y = y * jax.nn.sigmoid(y)
    scale = jnp.where(j >= k_col_block, MLSTM_QK_DIM ** -0.5, 1.0).astype(F32)
    o_ref[...] = (y * scale).astype(o_ref.dtype)


def _conv(proj, conv_w, conv_b, qk_off, n_lat_rows, lat_len, ctx_len):
    t = proj.shape[0]
    width = conv_w.shape[1]
    tc = _pick(width // 2, (512, 256, 128))
    assert qk_off % tc == 0 and lat_len % CONV_ROWS == 0 and ctx_len % CONV_ROWS == 0
    cb = qk_off // tc
    halo_per_blk = CONV_ROWS // HALO_ROWS
    n_halo = t // HALO_ROWS
    return pl.pallas_call(
        functools.partial(_conv_kernel, n_lat_rows=n_lat_rows, lat_len=lat_len, ctx_len=ctx_len,
                          k_col_block=(width // 2) // tc),
        grid=(t // CONV_ROWS, width // tc),
        in_specs=[
            pl.BlockSpec((CONV_ROWS, tc), lambda i, j: (i, cb + j)),
            pl.BlockSpec((HALO_ROWS, tc), lambda i, j: (jnp.maximum(i * halo_per_blk - 1, 0), cb + j)),
            pl.BlockSpec((HALO_ROWS, tc), lambda i, j: (jnp.minimum((i + 1) * halo_per_blk, n_halo - 1), cb + j)),
            pl.BlockSpec((3, tc), lambda i, j: (0, j)),
            pl.BlockSpec((1, tc), lambda i, j: (0, j)),
        ],
        out_specs=pl.BlockSpec((CONV_ROWS, tc), lambda i, j: (i, j)),
        out_shape=jax.ShapeDtypeStruct((t, width), BF16),
        compiler_params=_cparams(("arbitrary", "arbitrary")),
        name="qk_conv",
    )(proj, proj, proj, conv_w, conv_b.reshape(1, width))


ROPE_ROWS = 256
ROPE_HEAD_GROUP = 4


def _rope_kernel(q_ref, k_ref, cos_ref, s1_ref, s2_ref, gq_ref, gk_ref, qo_ref, ko_ref, *, n_q_heads, n_k_heads):
    cos = cos_ref[...]
    s1 = s1_ref[...]
    s2 = s2_ref[...]

    def prep_heads(src_ref, dst_ref, g, n_heads, scale):
        for h0 in range(0, n_heads, ROPE_HEAD_GROUP):
            sls = [slice(h * HEAD_DIM, (h + 1) * HEAD_DIM) for h in range(h0, min(h0 + ROPE_HEAD_GROUP, n_heads))]
            xs = [src_ref[:, sl].astype(F32) for sl in sls]
            inv = [lax.rsqrt(jnp.mean(x * x, axis=-1, keepdims=True) + EPS) for x in xs]
            xn = [x * r * g for x, r in zip(xs, inv)]
            up = [pltpu.roll(x, HEAD_DIM - ROPE_PAIRS, 1) for x in xn]
            dn = [pltpu.roll(x, ROPE_PAIRS, 1) for x in xn]
            for sl, x, u, dwn in zip(sls, xn, up, dn):
                y = x * cos + u * s1 + dwn * s2
                dst_ref[:, sl] = (y * scale if scale != 1.0 else y).astype(dst_ref.dtype)

    prep_heads(q_ref, qo_ref, gq_ref[...], n_q_heads, ATTN_SCALE)
    prep_heads(k_ref, ko_ref, gk_ref[...], n_k_heads, 1.0)


def _rope(proj, tabs, g_q, g_k, aw, kvw, k_off, n_lat_rows):
    t = proj.shape[0]
    assert k_off % kvw == 0
    kb = k_off // kvw
    n_lat_blk = n_lat_rows // ROPE_ROWS
    lat_blk_per_batch = (tabs[0].shape[0] - ROPE_ROWS) // ROPE_ROWS

    def tab_map(i):
        return (jnp.where(i < n_lat_blk, i % lat_blk_per_batch, lat_blk_per_batch), 0)

    tab_spec = pl.BlockSpec((ROPE_ROWS, HEAD_DIM), tab_map)
    return pl.pallas_call(
        functools.partial(_rope_kernel, n_q_heads=aw // HEAD_DIM, n_k_heads=kvw // HEAD_DIM),
        grid=(t // ROPE_ROWS,),
        in_specs=[
            pl.BlockSpec((ROPE_ROWS, aw), lambda i: (i, 0)),
            pl.BlockSpec((ROPE_ROWS, kvw), lambda i: (i, kb)),
            tab_spec, tab_spec, tab_spec,
            pl.BlockSpec((1, HEAD_DIM), lambda i: (0, 0)),
            pl.BlockSpec((1, HEAD_DIM), lambda i: (0, 0)),
        ],
        out_specs=[
            pl.BlockSpec((ROPE_ROWS, aw), lambda i: (i, 0)),
            pl.BlockSpec((ROPE_ROWS, kvw), lambda i: (i, 0)),
        ],
        out_shape=[jax.ShapeDtypeStruct((t, aw), BF16), jax.ShapeDtypeStruct((t, kvw), BF16)],
        compiler_params=_cparams(("arbitrary",)),
        name="qk_norm_rope",
    )(proj, proj, tabs[0], tabs[1], tabs[2], g_q.reshape(1, HEAD_DIM), g_k.reshape(1, HEAD_DIM))


def _rope_tables(n_lat):
    rows = n_lat // GRID_W
    inv_freq = ROPE_THETA ** (-jnp.arange(ROPE_PAIRS, dtype=F32) / ROPE_PAIRS)
    row_pos = jnp.repeat(jnp.arange(rows, dtype=F32), GRID_W)
    col_pos = jnp.tile(jnp.arange(GRID_W, dtype=F32), rows)
    ang_r = row_pos[:, None] * inv_freq
    ang_c = col_pos[:, None] * inv_freq
    zeros = jnp.zeros_like(ang_r)
    cos = jnp.concatenate([jnp.cos(ang_r), jnp.cos(ang_r), jnp.cos(ang_c), jnp.cos(ang_c)], axis=-1)
    s1 = jnp.concatenate([-jnp.sin(ang_r), zeros, -jnp.sin(ang_c), zeros], axis=-1)
    s2 = jnp.concatenate([zeros, jnp.sin(ang_r), zeros, jnp.sin(ang_c)], axis=-1)
    ident = jnp.ones((ROPE_ROWS, HEAD_DIM), F32)
    zpad = jnp.zeros((ROPE_ROWS, HEAD_DIM), F32)
    return (jnp.concatenate([cos, ident], 0), jnp.concatenate([s1, zpad], 0), jnp.concatenate([s2, zpad], 0))


def _attn_kernel(sink_ref, q_ref, kp_ref, kc_ref, kn_ref, kx_ref, vp_ref, vc_ref, vn_ref, vx_ref, o_ref,
                 *, n_lat_blk, n_kv, group, ctx_len):
    n = pl.program_id(1)
    is_ctx = n >= n_lat_blk
    n_band = 3 * QBLK
    n_keys = n_band + ctx_len
    qi = lax.broadcasted_iota(jnp.int32, (QBLK, n_keys), 0)
    kj = lax.broadcasted_iota(jnp.int32, (QBLK, n_keys), 1)
    rel = kj - QBLK - qi
    kpos = n * QBLK + kj - QBLK
    band_ok = (jnp.abs(rel) <= WINDOW) & (kpos >= 0) & (kpos < n_lat_blk * QBLK) & jnp.logical_not(is_ctx)
    valid = band_ok | (kj >= n_band)

    for hk in range(n_kv):
        ksl = slice(hk * HEAD_DIM, (hk + 1) * HEAD_DIM)
        k_all = jnp.concatenate([kp_ref[:, ksl], kc_ref[:, ksl], kn_ref[:, ksl], kx_ref[:, ksl]], axis=0)
        v_all = jnp.concatenate([vp_ref[:, ksl], vc_ref[:, ksl], vn_ref[:, ksl], vx_ref[:, ksl]], axis=0)
        heads = [hk * group + g for g in range(group)]
        qsl = {h: slice(h * HEAD_DIM, (h + 1) * HEAD_DIM) for h in heads}
        s = {h: jnp.where(valid, lax.dot_general(q_ref[:, qsl[h]], k_all, (((1,), (1,)), ((), ())),
                                                 preferred_element_type=F32), NEG) for h in heads}
        m = {h: jnp.maximum(jnp.max(s[h], axis=-1, keepdims=True), sink_ref[h]) for h in heads}
        p = {h: jnp.exp(s[h] - m[h]) for h in heads}
        denom = {h: jnp.sum(p[h], axis=-1, keepdims=True) + jnp.exp(sink_ref[h] - m[h]) for h in heads}
        o = {h: jnp.dot(p[h].astype(BF16), v_all, preferred_element_type=F32) for h in heads}
        for h in heads:
            o_ref[:, qsl[h]] = (o[h] / denom[h]).astype(o_ref.dtype)


def _attention(sink, qr, kr, proj, aw, kvw, v_off, n_batch, lat_len, ctx_len, with_ctx):
    t = proj.shape[0]
    assert v_off % kvw == 0
    vb = v_off // kvw
    n_lat_blk = lat_len // QBLK
    n_ctx_blk = ctx_len // QBLK
    n_lat_rows = n_batch * lat_len
    nblk = n_lat_blk + (n_ctx_blk if with_ctx else 0)

    def qrow(b, n):
        return jnp.where(n < n_lat_blk, b * n_lat_blk + n, n_lat_rows // QBLK + b * n_ctx_blk + (n - n_lat_blk))

    def band(delta):
        def f(b, n):
            nn = jnp.clip(jnp.minimum(n, n_lat_blk - 1) + delta, 0, n_lat_blk - 1)
            return b * n_lat_blk + nn
        return f

    def ctx_row(b, n):
        return n_lat_rows // ctx_len + b

    def kspec(rowf):
        return pl.BlockSpec((QBLK, kvw), lambda b, n: (rowf(b, n), 0))

    def vspec(rowf):
        return pl.BlockSpec((QBLK, kvw), lambda b, n: (rowf(b, n), vb))

    return pl.pallas_call(
        functools.partial(_attn_kernel, n_lat_blk=n_lat_blk, n_kv=kvw // HEAD_DIM,
                          group=aw // kvw, ctx_len=ctx_len),
        grid=(n_batch, nblk),
        in_specs=[
            pl.BlockSpec(memory_space=pltpu.SMEM),
            pl.BlockSpec((QBLK, aw), lambda b, n: (qrow(b, n), 0)),
            kspec(band(-1)), kspec(band(0)), kspec(band(1)),
            pl.BlockSpec((ctx_len, kvw), lambda b, n: (ctx_row(b, n), 0)),
            vspec(band(-1)), vspec(band(0)), vspec(band(1)),
            pl.BlockSpec((ctx_len, kvw), lambda b, n: (ctx_row(b, n), vb)),
        ],
        out_specs=pl.BlockSpec((QBLK, aw), lambda b, n: (qrow(b, n), 0)),
        out_shape=jax.ShapeDtypeStruct((n_lat_rows + (n_batch * ctx_len if with_ctx else 0), aw), BF16),
        compiler_params=_cparams(("arbitrary", "arbitrary")),
        name="attention",
    )(sink, qr, kr, kr, kr, kr, proj, proj, proj, proj)


HEAD_GROUP = 8


def _mlstm_kernel(q_ref, k_ref, v_ref, g_ref, o_ref, *state, n_heads):
    c_scrs, nm_scrs = state[:n_heads], state[n_heads:]
    d = pl.program_id(0)
    c = pl.program_id(2)
    L = MLSTM_CHUNK
    dk = MLSTM_QK_DIM
    dv = MLSTM_V_DIM

    @pl.when(c == 0)
    def _():
        for scr in state:
            scr[...] = jnp.zeros_like(scr)

    r = lax.broadcasted_iota(jnp.int32, (L, L), 0)
    s = lax.broadcasted_iota(jnp.int32, (L, L), 1)
    fwd = d == 0
    lag = (r - s) * (1 - 2 * d)
    tri = lag >= 0
    tri_t = lag <= 0
    tri_b = tri.astype(BF16)
    tri_tb = tri_t.astype(BF16)

    gates = g_ref[...]
    logf = jnp.minimum(gates, 0.0) - jnp.log1p(jnp.exp(-jnp.abs(gates)))
    gates_t = gates.T
    logf_t = logf.T

    def split_dot_l(mat_b, x):
        hi = x.astype(BF16)
        lo = (x - hi.astype(F32)).astype(BF16)
        return jnp.dot(mat_b, hi, preferred_element_type=F32) + jnp.dot(mat_b, lo, preferred_element_type=F32)

    def split_dot_r(x, mat_b):
        hi = x.astype(BF16)
        lo = (x - hi.astype(F32)).astype(BF16)
        return jnp.dot(hi, mat_b, preferred_element_type=F32) + jnp.dot(lo, mat_b, preferred_element_type=F32)

    cum_col = split_dot_l(tri_b, logf)
    cum_row = split_dot_r(logf_t, tri_tb)
    end_col = jnp.where(fwd, cum_col[L - 1:L, :], cum_col[0:1, :])

    nt = (((1,), (1,)), ((), ()))
    for h0 in range(0, n_heads, HEAD_GROUP):
        hs = range(h0, min(h0 + HEAD_GROUP, n_heads))
        qs = {h: q_ref[:, h * dk:(h + 1) * dk] for h in hs}
        ks = {h: k_ref[:, h * dk:(h + 1) * dk] for h in hs}
        vs = {h: v_ref[:, h * dv:(h + 1) * dv] for h in hs}
        b_col = {h: cum_col[:, n_heads + h:n_heads + h + 1] for h in hs}
        b_row = {h: cum_row[n_heads + h:n_heads + h + 1, :] for h in hs}
        i_col = {h: gates[:, h:h + 1] for h in hs}
        i_row = {h: gates_t[h:h + 1, :] for h in hs}
        b_end = {h: end_col[:, n_heads + h:n_heads + h + 1] for h in hs}
        n_prev = {h: nm_scrs[h][0:1, :] for h in hs}
        m_prev = {h: nm_scrs[h][1:2, 0:1] for h in hs}
        ct_prev = {h: c_scrs[h][...] for h in hs}

        dmat = {h: jnp.where(tri, b_col[h] - b_row[h] + i_row[h], NEG) for h in hs}
        m_inter = {h: b_col[h] + m_prev[h] for h in hs}
        m_t = {h: jnp.maximum(m_inter[h], jnp.max(dmat[h], axis=-1, keepdims=True)) for h in hs}
        qk = {h: lax.dot_general(qs[h], ks[h], nt, preferred_element_type=F32) for h in hs}
        qc = {h: jnp.dot(qs[h], ct_prev[h].astype(BF16), preferred_element_type=F32) for h in hs}
        qn_prev = {h: jnp.sum(qs[h].astype(F32) * n_prev[h], axis=-1, keepdims=True) for h in hs}
        a = {h: jnp.exp(m_inter[h] - m_t[h]) for h in hs}
        smat = {h: qk[h] * jnp.exp(dmat[h] - m_t[h]) for h in hs}
        sv = {h: jnp.dot(smat[h].astype(BF16), vs[h], preferred_element_type=F32) for h in hs}
        qn = {h: jnp.sum(smat[h], axis=-1, keepdims=True) + a[h] * qn_prev[h] for h in hs}

        g_row = {h: b_end[h] - b_row[h] + i_row[h] for h in hs}
        m_new = {h: jnp.maximum(b_end[h] + m_prev[h], jnp.max(g_row[h], axis=-1, keepdims=True)) for h in hs}
        kw = {h: ks[h].astype(F32) * jnp.exp(b_end[h] - b_col[h] + i_col[h] - m_new[h]) for h in hs}
        a_end = {h: jnp.exp(b_end[h] + m_prev[h] - m_new[h]) for h in hs}
        kv = {h: jnp.dot(kw[h].T.astype(BF16), vs[h], preferred_element_type=F32) for h in hs}

        for h in hs:
            hout = (sv[h] + a[h] * qc[h]) / jnp.maximum(jnp.abs(qn[h]), jnp.exp(-m_t[h]))
            o_ref[:, h * dv:(h + 1) * dv] = hout.astype(o_ref.dtype)
        for h in hs:
            c_scrs[h][...] = a_end[h] * ct_prev[h] + kv[h]
            nm_scrs[h][0:1, :] = a_end[h] * n_prev[h] + jnp.sum(kw[h], axis=0, keepdims=True)
            nm_scrs[h][1:2, :] = jnp.broadcast_to(m_new[h], (1, dk))


def _mlstm(qk, proj, gates, mqk, mw, v_off, n_batch, lat_len, ctx_len):
    t = proj.shape[0]
    L = MLSTM_CHUNK
    n_heads = mw // MLSTM_V_DIM
    assert v_off % mw == 0 and mqk == n_heads * MLSTM_QK_DIM
    vb = v_off // mw
    n_ctx = ctx_len // L
    n_lat = lat_len // L
    lat_blocks = n_batch * n_lat

    def row(d, b, c):
        cc = jnp.where(d == 0, c, n_ctx - 1 - c)
        lc = jnp.where(d == 0, c - n_ctx, n_lat - 1 - (c - n_ctx))
        return jnp.where(c < n_ctx, lat_blocks + b * n_ctx + cc, b * n_lat + lc)

    return pl.pallas_call(
        functools.partial(_mlstm_kernel, n_heads=n_heads),
        grid=(2, n_batch, n_ctx + n_lat),
        in_specs=[
            pl.BlockSpec((L, mqk), lambda d, b, c: (row(d, b, c), 0)),
            pl.BlockSpec((L, mqk), lambda d, b, c: (row(d, b, c), 1)),
            pl.BlockSpec((L, mw), lambda d, b, c: (row(d, b, c), vb)),
            pl.BlockSpec((None, L, LANES), lambda d, b, c: (d, row(d, b, c), 0)),
        ],
        out_specs=pl.BlockSpec((None, L, mw), lambda d, b, c: (d, row(d, b, c), 0)),
        out_shape=jax.ShapeDtypeStruct((2, t, mw), BF16),
        scratch_shapes=([pltpu.VMEM((MLSTM_QK_DIM, MLSTM_V_DIM), F32)] * n_heads
                        + [pltpu.VMEM((8, MLSTM_QK_DIM), F32)] * n_heads),
        compiler_params=_cparams(("arbitrary", "arbitrary", "arbitrary")),
        name="mlstm",
    )(qk, qk, proj, gates)


def _branch_kernel(attn_ref, hfb_ref, mo_ref, gmh_ref, wa_ref, wm_ref, ga_ref, gm_ref, o_ref, hm_scr, *, n_heads):
    j = pl.program_id(1)

    @pl.when(j == 0)
    def _():
        dv = MLSTM_V_DIM
        for h in range(n_heads):
            sl = slice(h * dv, (h + 1) * dv)
            hsum = hfb_ref[0, :, sl].astype(F32) + hfb_ref[1, :, sl].astype(F32)
            x = jax.nn.sigmoid(mo_ref[:, sl].astype(F32)) * hsum
            y = x * lax.rsqrt(jnp.mean(x * x, axis=-1, keepdims=True) + EPS) * gmh_ref[:, sl]
            hm_scr[:, sl] = y.astype(BF16)

    ya = jnp.dot(attn_ref[...], wa_ref[...], preferred_element_type=F32)
    ym = jnp.dot(hm_scr[...], wm_ref[...], preferred_element_type=F32)
    u = jax.nn.sigmoid(ga_ref[...].astype(F32)) * ya + jax.nn.sigmoid(gm_ref[...].astype(F32)) * ym
    o_ref[...] = u.astype(o_ref.dtype)


def _branch(attn, hfb, proj, g_mh, wa, wm, layer, mo_off, ga_off, gm_off, rows, tm):
    aw = attn.shape[1]
    mw = hfb.shape[2]
    d = wa.shape[2]
    tn = _pick(d, (1024, 512, 256, 128))
    assert mo_off % mw == 0 and ga_off % tn == 0 and gm_off % tn == 0
    mob, gab, gmb = mo_off // mw, ga_off // tn, gm_off // tn
    return pl.pallas_call(
        functools.partial(_branch_kernel, n_heads=mw // MLSTM_V_DIM),
        grid=(rows // tm, d // tn),
        in_specs=[
            pl.BlockSpec((tm, aw), lambda i, j: (i, 0)),
            pl.BlockSpec((2, tm, mw), lambda i, j: (0, i, 0)),
            pl.BlockSpec((tm, mw), lambda i, j: (i, mob)),
            pl.BlockSpec((1, mw), lambda i, j: (0, 0)),
            pl.BlockSpec((None, aw, tn), lambda i, j: (layer, 0, j)),
            pl.BlockSpec((None, mw, tn), lambda i, j: (layer, 0, j)),
            pl.BlockSpec((tm, tn), lambda i, j: (i, gab + j)),
            pl.BlockSpec((tm, tn), lambda i, j: (i, gmb + j)),
        ],
        out_specs=pl.BlockSpec((tm, tn), lambda i, j: (i, j)),
        out_shape=jax.ShapeDtypeStruct((rows, d), BF16),
        scratch_shapes=[pltpu.VMEM((tm, mw), BF16)],
        compiler_params=_cparams(("arbitrary", "arbitrary")),
        name="branch_merge",
    )(attn, hfb, proj, g_mh.reshape(1, mw), wa, wm, proj, proj)


def _outproj_kernel(u_ref, w_ref, x_ref, gt_ref, o_ref, *, seg_args):
    seg = _seg_of_block(pl.program_id(0), *seg_args)
    y = jnp.dot(u_ref[...], w_ref[...], preferred_element_type=F32)
    o_ref[...] = x_ref[...] + gt_ref[pl.ds(seg, 1), :] * y


def _outproj(u, w_out, xs, mod, layer, tm, seg_args):
    rows, d = u.shape
    tn = _pick(d, (1024, 512, 256, 128))
    gate_blk = 2 * (d // tn)
    return pl.pallas_call(
        functools.partial(_outproj_kernel, seg_args=seg_args),
        grid=(rows // tm, d // tn),
        in_specs=[
            pl.BlockSpec((tm, d), lambda i, j: (i, 0)),
            pl.BlockSpec((None, d, tn), lambda i, j: (layer, 0, j)),
            pl.BlockSpec((tm, tn), lambda i, j: (i, j)),
            pl.BlockSpec((None, MOD_ROWS, tn), lambda i, j: (layer, 0, gate_blk + j)),
        ],
        out_specs=pl.BlockSpec((tm, tn), lambda i, j: (i, j)),
        out_shape=jax.ShapeDtypeStruct(xs.shape, F32),
        input_output_aliases={2: 0},
        compiler_params=_cparams(("arbitrary", "arbitrary")),
        name="outproj_residual",
    )(u, w_out, xs, mod)


PAIR_BLOCK = 2 * LANES


def _pack_bf16_pairs(h):
    blocks = []
    for b in range(h.shape[1] // PAIR_BLOCK):
        hi = pltpu.bitcast(h[:, b * PAIR_BLOCK:b * PAIR_BLOCK + LANES].astype(BF16).astype(F32), jnp.uint32)
        lo = pltpu.bitcast(h[:, b * PAIR_BLOCK + LANES:(b + 1) * PAIR_BLOCK].astype(BF16).astype(F32), jnp.uint32)
        blocks.append(hi | (lo >> 16))
    return blocks[0] if len(blocks) == 1 else jnp.concatenate(blocks, axis=1)


def _unpack_bf16_pairs(p):
    blocks = []
    for b in range(p.shape[1] // LANES):
        w = p[:, b * LANES:(b + 1) * LANES]
        blocks.append(pltpu.bitcast(w & jnp.uint32(0xFFFF0000), F32))
        blocks.append(pltpu.bitcast(w << 16, F32))
    return jnp.concatenate(blocks, axis=1)


ROUTER_ROWS = 256


def _router_kernel(x_ref, g_ref, sh_ref, sc_ref, wr_ref, br_ref, hp_ref, idx_ref, wt_ref, *, seg_args, n_experts):
    seg = _seg_of_block(pl.program_id(0), *seg_args)
    h = _modulated(x_ref[...], g_ref[...], sc_ref[pl.ds(seg, 1), :], sh_ref[pl.ds(seg, 1), :])
    hp_ref[...] = _pack_bf16_pairs(h)

    wr = wr_ref[...]
    h_hi = h.astype(BF16)
    h_lo = (h - h_hi.astype(F32)).astype(BF16)
    w_hi = wr.astype(BF16)
    w_lo = (wr - w_hi.astype(F32)).astype(BF16)
    nt = (((1,), (1,)), ((), ()))
    logits = (lax.dot_general(w_hi, h_hi, nt, preferred_element_type=F32)
              + lax.dot_general(w_hi, h_lo, nt, preferred_element_type=F32)
              + lax.dot_general(w_lo, h_hi, nt, preferred_element_type=F32))
    aff = jax.nn.sigmoid(logits)
    biased = aff + br_ref[...]
    rb = [biased[e:e + 1, :] for e in range(n_experts)]
    ra = [aff[e:e + 1, :] for e in range(n_experts)]

    epg = EXPERTS_PER_GROUP
    scores = []
    for g in range(N_GROUPS):
        a, b, c, d = rb[epg * g:epg * g + epg]
        hi1, lo1 = jnp.maximum(a, b), jnp.minimum(a, b)
        hi2, lo2 = jnp.maximum(c, d), jnp.minimum(c, d)
        scores.append(jnp.maximum(hi1, hi2) + jnp.maximum(jnp.minimum(hi1, hi2), jnp.maximum(lo1, lo2)))
    best = jnp.zeros(scores[0].shape, jnp.int32)
    best_s = scores[0]
    for g in range(1, N_GROUPS):
        upd = scores[g] > best_s
        best = jnp.where(upd, g, best)
        best_s = jnp.where(upd, scores[g], best_s)

    vb, va = [], []
    for j in range(epg):
        xb, xa = rb[j], ra[j]
        for g in range(1, N_GROUPS):
            sel = best == g
            xb = jnp.where(sel, rb[epg * g + j], xb)
            xa = jnp.where(sel, ra[epg * g + j], xa)
        vb.append(xb)
        va.append(xa)

    i1 = jnp.zeros_like(best)
    m1, a1 = vb[0], va[0]
    for j in range(1, epg):
        upd = vb[j] > m1
        i1 = jnp.where(upd, j, i1)
        m1 = jnp.where(upd, vb[j], m1)
        a1 = jnp.where(upd, va[j], a1)
    i2 = jnp.zeros_like(best)
    m2 = jnp.full_like(m1, -jnp.inf)
    a2 = jnp.zeros_like(a1)
    for j in range(epg):
        upd = (i1 != j) & (vb[j] > m2)
        i2 = jnp.where(upd, j, i2)
        m2 = jnp.where(upd, vb[j], m2)
        a2 = jnp.where(upd, va[j], a2)

    idx_ref[0:1, :] = best * epg + i1
    idx_ref[1:2, :] = best * epg + i2
    tot = a1 + a2
    wt_ref[0:1, :] = a1 / tot
    wt_ref[1:2, :] = a2 / tot


def _router(xs, g, mod, layer, w_router_t, b_router, rows, tm, seg_args):
    d = xs.shape[1]
    e = w_router_t.shape[0]
    assert e == N_GROUPS * EXPERTS_PER_GROUP
    return pl.pallas_call(
        functools.partial(_router_kernel, seg_args=seg_args, n_experts=e),
        grid=(rows // tm,),
        in_specs=[
            pl.BlockSpec((tm, d), lambda i: (i, 0)),
            pl.BlockSpec((1, d), lambda i: (0, 0)),
            pl.BlockSpec((None, MOD_ROWS, d), lambda i: (layer, 0, 3)),
            pl.BlockSpec((None, MOD_ROWS, d), lambda i: (layer, 0, 4)),
            pl.BlockSpec((e, d), lambda i: (0, 0)),
            pl.BlockSpec((e, 1), lambda i: (0, 0)),
        ],
        out_specs=[
            pl.BlockSpec((tm, d // 2), lambda i: (i, 0)),
            pl.BlockSpec((2, tm), lambda i: (0, i)),
            pl.BlockSpec((2, tm), lambda i: (0, i)),
        ],
        out_shape=[
            jax.ShapeDtypeStruct((rows, d // 2), jnp.uint32),
            jax.ShapeDtypeStruct((2, rows), jnp.int32),
            jax.ShapeDtypeStruct((2, rows), F32),
        ],
        compiler_params=_cparams(("arbitrary",)),
        name="ffn_modulate_route",
    )(xs, g, mod, mod, w_router_t, b_router.reshape(e, 1))


def _route_kernel(idx_ref, pos_ref, te_ref, nu_ref, *, n_experts, tm, n_chunks):
    e_iota = lax.broadcasted_iota(jnp.int32, (n_experts, LANES), 0)
    idx_all = idx_ref[...]

    def count_col(k):
        col = jnp.zeros((n_experts, LANES), F32)
        for e in range(n_experts):
            col = jnp.where(e_iota == e, jnp.sum((idx_all[k] == e).astype(F32)), col)
        return col

    c0 = count_col(0)
    counts = c0 + count_col(1)
    tiles_per = jnp.floor((counts + (tm - 1)) * (1.0 / tm))
    tile_end = tiles_per
    s = 1
    while s < n_experts:
        tile_end = tile_end + jnp.where(e_iota >= s, pltpu.roll(tile_end, s, 0), 0.0)
        s *= 2
    row_off = (tile_end - tiles_per) * tm

    r = lax.broadcasted_iota(jnp.int32, (LANES, LANES), 0)
    c = lax.broadcasted_iota(jnp.int32, (LANES, LANES), 1)
    triu = (r <= c).astype(BF16)

    def body(ch, carry):
        new = []
        for k in range(2):
            onehot = (e_iota == idx_ref[k, pl.ds(ch, 1), :]).astype(F32)
            csum = jnp.dot(onehot.astype(BF16), triu, preferred_element_type=F32)
            posv = jnp.sum(onehot * (row_off + carry[k] + csum - 1.0), axis=0, keepdims=True)
            pos_ref[k, pl.ds(ch, 1), :] = posv.astype(jnp.int32)
            new.append(carry[k] + csum[:, LANES - 1:LANES])
        return tuple(new)

    lax.fori_loop(0, n_chunks, body, (jnp.zeros((n_experts, LANES), F32), c0))

    t_iota = lax.broadcasted_iota(jnp.int32, (n_experts, LANES), 1).astype(F32)
    te = jnp.sum((tile_end <= t_iota).astype(F32), axis=0, keepdims=True)
    te_ref[...] = jnp.minimum(te, n_experts - 1.0).astype(jnp.int32)
    nu_ref[...] = tile_end[n_experts - 1:n_experts, :].astype(jnp.int32)


def _route(idx, n_experts, tm):
    k, rows = idx.shape
    assert k == 2 and rows % LANES == 0 and tm & (tm - 1) == 0
    n_chunks = rows // LANES
    n_tiles = (k * rows) // tm + n_experts
    assert n_tiles <= LANES
    pos, te, nu = pl.pallas_call(
        functools.partial(_route_kernel, n_experts=n_experts, tm=tm, n_chunks=n_chunks),
        out_shape=[
            jax.ShapeDtypeStruct((k, n_chunks, LANES), jnp.int32),
            jax.ShapeDtypeStruct((1, LANES), jnp.int32),
            jax.ShapeDtypeStruct((1, LANES), jnp.int32),
        ],
        compiler_params=pltpu.CompilerParams(vmem_limit_bytes=VMEM_LIMIT),
        name="moe_route",
    )(idx.reshape(k, n_chunks, LANES))
    return pos.reshape(k * rows), te[0, :n_tiles], nu[0, :1], n_tiles


DISPATCH_ROWS = 256


def _dispatch_kernel(pos_ref, hp_ref, init_ref, hs_ref, sem, *, n_rows):
    del init_ref
    base = pl.program_id(0) * DISPATCH_ROWS

    def row_copy(k, r):
        return pltpu.make_async_copy(hp_ref.at[pl.ds(r, 1)], hs_ref.at[pl.ds(pos_ref[k * n_rows + base + r], 1)], sem)

    def start(r, carry):
        row_copy(0, r).start()
        row_copy(1, r).start()
        return carry

    def wait(r, carry):
        row_copy(0, r).wait()
        row_copy(1, r).wait()
        return carry

    lax.fori_loop(0, DISPATCH_ROWS, start, 0)
    lax.fori_loop(0, DISPATCH_ROWS, wait, 0)


def _dispatch(pos, hp, n_sorted_rows):
    rows, half = hp.shape
    return pl.pallas_call(
        functools.partial(_dispatch_kernel, n_rows=rows),
        grid_spec=pltpu.PrefetchScalarGridSpec(
            num_scalar_prefetch=1,
            grid=(rows // DISPATCH_ROWS,),
            in_specs=[
                pl.BlockSpec((DISPATCH_ROWS, half), lambda i, pos: (i, 0)),
                pl.BlockSpec(memory_space=pl.ANY),
            ],
            out_specs=pl.BlockSpec(memory_space=pl.ANY),
            scratch_shapes=[pltpu.SemaphoreType.DMA],
        ),
        out_shape=jax.ShapeDtypeStruct((n_sorted_rows, half), hp.dtype),
        input_output_aliases={2: 0},
        compiler_params=_cparams(("arbitrary",)),
        name="moe_dispatch",
    )(pos, hp, jnp.zeros((n_sorted_rows, half), hp.dtype))


def _expert_kernel(te_ref, nused_ref, x_ref, wg_ref, wu_ref, wd_ref, o_ref, xs_scr, act_scr):
    i = pl.program_id(0)

    @pl.when(i >= nused_ref[0])
    def _():
        o_ref[...] = jnp.zeros_like(o_ref)

    @pl.when(i < nused_ref[0])
    def _():
        def rows_body(r, carry):
            rs = pl.ds(pl.multiple_of(r * ROW_CHUNK, ROW_CHUNK), ROW_CHUNK)
            xs_scr[rs, :] = _unpack_bf16_pairs(x_ref[rs, :]).astype(BF16)
            return carry

        lax.fori_loop(0, x_ref.shape[0] // ROW_CHUNK, rows_body, 0)

        xs = xs_scr[...]
        for c in range(act_scr.shape[1] // PAIR_BLOCK):
            cs = slice(c * PAIR_BLOCK, (c + 1) * PAIR_BLOCK)
            gate = jnp.dot(xs, wg_ref[:, cs], preferred_element_type=F32)
            up = jnp.dot(xs, wu_ref[:, cs], preferred_element_type=F32)
            act_scr[:, cs] = (gate * jax.nn.sigmoid(gate) * up).astype(BF16)
        act = act_scr[...]
        for c in range(wd_ref.shape[1] // PAIR_BLOCK):
            y = jnp.dot(act, wd_ref[:, c * PAIR_BLOCK:(c + 1) * PAIR_BLOCK], preferred_element_type=F32)
            o_ref[:, c * LANES:(c + 1) * LANES] = _pack_bf16_pairs(y)


def _experts(tile_expert, n_used, hs, wg, wu, wd, layer, tm):
    p, half = hs.shape
    _, e, d, ff = wg.shape
    assert ff % PAIR_BLOCK == 0
    n_tiles = p // tm

    def row(i, te, nu):
        return (jnp.minimum(i, nu[0] - 1), 0)

    def wspec(shape):
        return pl.BlockSpec((None, None) + shape, lambda i, te, nu: (layer, te[i], 0, 0), pipeline_mode=pl.Buffered(1))

    return pl.pallas_call(
        _expert_kernel,
        grid_spec=pltpu.PrefetchScalarGridSpec(
            num_scalar_prefetch=2,
            grid=(n_tiles,),
            in_specs=[pl.BlockSpec((tm, half), row), wspec((d, ff)), wspec((d, ff)), wspec((ff, d))],
            out_specs=pl.BlockSpec((tm, half), lambda i, te, nu: (i, 0)),
            scratch_shapes=[pltpu.VMEM((tm, d), BF16), pltpu.VMEM((tm, ff), BF16)],
        ),
        out_shape=jax.ShapeDtypeStruct((p, half), jnp.uint32),
        compiler_params=_cparams(("arbitrary",)),
        name="moe_experts",
    )(tile_expert, n_used, hs, wg, wu, wd)


COMBINE_ROWS = 256
COMBINE_CHUNK = 8


def _combine_kernel(pos_ref, x_ref, wt_ref, gt_ref, ys_ref, o_ref, buf, sems, *, seg_args, n_rows):
    i = pl.program_id(0)
    slot = i % 2
    seg = _seg_of_block(i, *seg_args)

    def row_copy(blk, sl, k, r):
        return pltpu.make_async_copy(ys_ref.at[pl.ds(pos_ref[k * n_rows + blk * COMBINE_ROWS + r], 1)],
                                     buf.at[sl, k, pl.ds(r, 1)], sems.at[sl])

    def issue_rows(blk, sl, r0):
        for rr in range(COMBINE_CHUNK):
            row_copy(blk, sl, 0, r0 + rr).start()
            row_copy(blk, sl, 1, r0 + rr).start()

    @pl.when(i == 0)
    def _():
        def body(c, carry):
            issue_rows(0, 0, c * COMBINE_CHUNK)
            return carry

        lax.fori_loop(0, COMBINE_ROWS // COMBINE_CHUNK, body, 0)

    def wait(r, carry):
        row_copy(i, slot, 0, r).wait()
        row_copy(i, slot, 1, r).wait()
        return carry

    lax.fori_loop(0, COMBINE_ROWS, wait, 0)

    last = pl.num_programs(0) - 1
    nxt = jnp.minimum(i + 1, last)
    gt = gt_ref[pl.ds(seg, 1), :]

    def body(c, carry):
        r0 = pl.multiple_of(c * COMBINE_CHUNK, COMBINE_CHUNK)
        issue_rows(nxt, 1 - slot, r0)
        rs = pl.ds(r0, COMBINE_CHUNK)
        y0 = _unpack_bf16_pairs(buf[slot, 0, rs, :])
        y1 = _unpack_bf16_pairs(buf[slot, 1, rs, :])
        w = wt_ref[rs, :]
        o_ref[rs, :] = x_ref[rs, :] + gt * (w[:, 0:1] * y0 + w[:, 1:2] * y1)
        return carry

    lax.fori_loop(0, COMBINE_ROWS // COMBINE_CHUNK, body, 0)

    @pl.when(i == last)
    def _():
        def drain(r, carry):
            row_copy(last, 1 - slot, 0, r).wait()
            row_copy(last, 1 - slot, 1, r).wait()
            return carry

        lax.fori_loop(0, COMBINE_ROWS, drain, 0)


def _combine(pos, xs, wts_t, mod, layer, ys, rows, seg_args):
    d = xs.shape[1]
    half = d // 2
    return pl.pallas_call(
        functools.partial(_combine_kernel, seg_args=seg_args, n_rows=rows),
        grid_spec=pltpu.PrefetchScalarGridSpec(
            num_scalar_prefetch=1,
            grid=(rows // COMBINE_ROWS,),
            in_specs=[
                pl.BlockSpec((COMBINE_ROWS, d), lambda i, pos: (i, 0)),
                pl.BlockSpec((COMBINE_ROWS, 2), lambda i, pos: (i, 0)),
                pl.BlockSpec((None, MOD_ROWS, d), lambda i, pos: (layer, 0, 5)),
                pl.BlockSpec(memory_space=pl.ANY),
            ],
            out_specs=pl.BlockSpec((COMBINE_ROWS, d), lambda i, pos: (i, 0)),
            scratch_shapes=[pltpu.VMEM((2, 2, COMBINE_ROWS, half), jnp.uint32), pltpu.SemaphoreType.DMA((2,))],
        ),
        out_shape=jax.ShapeDtypeStruct((rows, d), F32),
        compiler_params=_cparams(("arbitrary",)),
        name="moe_combine",
    )(pos, xs, wts_t, mod, ys)


def kernel(x, c, ctx, c_ctx, w_ada, b_ada, g_mix, g_ffn, w_in, b_in, g_q, g_k, sink, conv_w, conv_b, g_mh,
           w_br_attn, w_br_mlstm, w_out, w_router, b_router, w_gate, w_up, w_down):
    n_batch, lat_len, d = x.shape
    ctx_len = ctx.shape[1]
    depth = w_ada.shape[0]
    d_in = w_in.shape[2]
    aw = w_br_attn.shape[1]
    mw = w_br_mlstm.shape[1]
    mqk = conv_w.shape[2] // 2
    n_mh = mw // MLSTM_V_DIM
    kvw = (d_in - aw - 2 * mqk - 2 * mw - 4 * n_mh - 2 * d) // 2
    n_experts = w_router.shape[1]
    assert n_batch + 1 <= MOD_ROWS and 2 * n_mh <= LANES

    n_lat_rows = n_batch * lat_len
    n_ctx_rows = n_batch * ctx_len
    tm = _pick(n_ctx_rows, (512, 256))
    assert lat_len % tm == 0
    seg_args = (n_lat_rows // tm, lat_len // tm, n_batch)
    tm_e = 512

    o_aq = 0
    o_ak = o_aq + aw
    o_av = o_ak + kvw
    o_mq = o_av + kvw
    o_mk = o_mq + mqk
    o_mv = o_mk + mqk
    o_mo = o_mv + mw
    o_g = o_mo + mw
    o_ga = o_g + 4 * n_mh
    o_gm = o_ga + d
    order = [(o_aq, aw), (o_mv, mw), (o_mo, mw), (o_ga, d), (o_gm, d), (o_ak, kvw), (o_av, kvw), (o_mq, mqk), (o_mk, mqk)]
    starts = [sum(w for _, w in order[:k]) for k in range(len(order))]
    _, n_mv, n_mo, n_ga, n_gm, n_ak, n_av, n_mq, _ = starts

    xs = jnp.concatenate([x.reshape(n_lat_rows, d), ctx.reshape(n_ctx_rows, d)], axis=0)
    cvec = jnp.zeros((MOD_ROWS, d), F32).at[:n_batch].set(c).at[n_batch].set(c_ctx)
    mod = _adaln(cvec, w_ada, b_ada)
    tabs = _rope_tables(lat_len)
    w_router_t = w_router.T

    w_main = [jnp.concatenate([w_in[l][:, o:o + w] for o, w in order], axis=1).astype(BF16)[None] for l in range(depth)]
    b_main = jnp.concatenate([b_in[:, o:o + w] for o, w in order], axis=1).reshape(depth, 1, -1)
    wg = jnp.zeros((depth, 2, d, LANES), F32)
    bg = jnp.zeros((depth, 2, 1, LANES), F32)
    for dr in range(2):
        gsl = slice(o_g + 2 * n_mh * dr, o_g + 2 * n_mh * (dr + 1))
        wg = wg.at[:, dr, :, :2 * n_mh].set(w_in[:, :, gsl])
        bg = bg.at[:, dr, 0, :2 * n_mh].set(b_in[:, gsl])
    wg = wg.astype(BF16)
    wa_b, wm_b, wo_b = w_br_attn.astype(BF16), w_br_mlstm.astype(BF16), w_out.astype(BF16)
    wgate_b, wup_b, wdown_b = w_gate.astype(BF16), w_up.astype(BF16), w_down.astype(BF16)

    for l in range(depth):
        need_ctx = l < depth - 1
        rows = n_lat_rows + (n_ctx_rows if need_ctx else 0)

        proj, gates = _inproj(xs, g_mix[l].reshape(1, d), mod, l, w_main[l], b_main, wg, bg, tm, seg_args)
        qk = _conv(proj, conv_w[l], conv_b[l], n_mq, n_lat_rows, lat_len, ctx_len)
        hfb = _mlstm(qk, proj, gates, mqk, mw, n_mv, n_batch, lat_len, ctx_len)
        qr, kr = _rope(proj, tabs, g_q[l], g_k[l], aw, kvw, n_ak, n_lat_rows)
        attn = _attention(sink[l], qr, kr, proj, aw, kvw, n_av, n_batch, lat_len, ctx_len, need_ctx)
        u = _branch(attn, hfb, proj, g_mh[l], wa_b, wm_b, l, n_mo, n_ga, n_gm, rows, tm)
        xs = _outproj(u, wo_b, xs, mod, l, tm, seg_args)

        hp, idx, wts = _router(xs, g_ffn[l].reshape(1, d), mod, l, w_router_t, b_router, rows, ROUTER_ROWS,
                               (n_lat_rows // ROUTER_ROWS, lat_len // ROUTER_ROWS, n_batch))
        pos, tile_expert, n_used, n_tiles = _route(idx, n_experts, tm_e)
        hs = _dispatch(pos, hp, n_tiles * tm_e)
        ys = _experts(tile_expert, n_used, hs, wgate_b, wup_b, wdown_b, l, tm_e)
        xs = _combine(pos, xs, wts.T, mod, l, ys, rows, (n_lat_rows // COMBINE_ROWS, lat_len // COMBINE_ROWS, n_batch))

    return xs[:n_lat_rows].reshape(n_batch, lat_len, d)
```

```python
import functools

import jax
import jax.numpy as jnp
from jax import lax
from jax.experimental import pallas as pl
from jax.experimental.pallas import tpu as pltpu

GRID_W = 64
HEAD_DIM = 128
WINDOW = 128
QBLK = 128
ROPE_THETA = 10000.0
ROPE_PAIRS = HEAD_DIM // 4
ATTN_SCALE = HEAD_DIM ** -0.5
MLSTM_QK_DIM = 128
MLSTM_V_DIM = 256
MLSTM_CHUNK = 128
N_GROUPS = 4
EXPERTS_PER_GROUP = 4
EPS = 1e-6
NEG = -1e30

LANES = 128
MOD_ROWS = 8
ROW_CHUNK = 64
VMEM_LIMIT = 56 << 20

F32 = jnp.float32
BF16 = jnp.bfloat16


def _pick(n, cands):
    for c in cands:
        if n % c == 0:
            return c
    raise ValueError(f"no tile in {cands} divides {n}")


def _cparams(sem, vmem=VMEM_LIMIT):
    return pltpu.CompilerParams(dimension_semantics=sem, vmem_limit_bytes=vmem)


def _seg_of_block(i, n_lat_blocks, blocks_per_batch, n_batch):
    return jnp.where(i < n_lat_blocks, i // blocks_per_batch, n_batch)


def _modulated(x, g, sc, sh):
    ms = jnp.mean(x * x, axis=-1, keepdims=True)
    y = x * lax.rsqrt(ms + EPS) * g
    return y * (1.0 + sc) + sh


def _adaln_kernel(c_ref, w_ref, b_ref, o_ref):
    c = c_ref[...]
    cs = (c * jax.nn.sigmoid(c)).astype(BF16)
    o_ref[...] = jnp.dot(cs, w_ref[...].astype(BF16), preferred_element_type=F32) + b_ref[...]


def _adaln(cvec, w_ada, b_ada):
    depth, d, n6 = w_ada.shape
    tn = _pick(n6, (512, 256, 128))
    return pl.pallas_call(
        _adaln_kernel,
        grid=(depth, n6 // tn),
        in_specs=[
            pl.BlockSpec((MOD_ROWS, d), lambda l, j: (0, 0)),
            pl.BlockSpec((None, d, tn), lambda l, j: (l, 0, j)),
            pl.BlockSpec((None, 1, tn), lambda l, j: (l, 0, j)),
        ],
        out_specs=pl.BlockSpec((None, MOD_ROWS, tn), lambda l, j: (l, 0, j)),
        out_shape=jax.ShapeDtypeStruct((depth, MOD_ROWS, n6), F32),
        compiler_params=_cparams(("arbitrary", "arbitrary")),
        name="adaln",
    )(cvec, w_ada, b_ada.reshape(depth, 1, n6))


def _inproj_kernel(x_ref, g_ref, sh_ref, sc_ref, w_ref, b_ref, wg_ref, bg_ref, o_ref, og_ref, h_scr, *, seg_args):
    i = pl.program_id(0)
    j = pl.program_id(1)

    @pl.when(j == 0)
    def _():
        seg = _seg_of_block(i, *seg_args)
        g = g_ref[...]
        sc = sc_ref[pl.ds(seg, 1), :]
        sh = sh_ref[pl.ds(seg, 1), :]

        def rows_body(r, carry):
            rs = pl.ds(pl.multiple_of(r * ROW_CHUNK, ROW_CHUNK), ROW_CHUNK)
            h_scr[rs, :] = _modulated(x_ref[rs, :], g, sc, sh).astype(BF16)
            return carry

        lax.fori_loop(0, x_ref.shape[0] // ROW_CHUNK, rows_body, 0)
        for d in range(2):
            og_ref[d] = jnp.dot(h_scr[...], wg_ref[d], preferred_element_type=F32) + bg_ref[d]

    o_ref[...] = (jnp.dot(h_scr[...], w_ref[...], preferred_element_type=F32) + b_ref[...]).astype(o_ref.dtype)


def _inproj(xs, g, mod, layer, w_main, b_main, w_gates, b_gates, tm, seg_args):
    t, d = xs.shape
    nc = w_main.shape[2]
    tn = _pick(nc, (1024, 512, 256, 128))
    return pl.pallas_call(
        functools.partial(_inproj_kernel, seg_args=seg_args),
        grid=(t // tm, nc // tn),
        in_specs=[
            pl.BlockSpec((tm, d), lambda i, j: (i, 0)),
            pl.BlockSpec((1, d), lambda i, j: (0, 0)),
            pl.BlockSpec((None, MOD_ROWS, d), lambda i, j: (layer, 0, 0)),
            pl.BlockSpec((None, MOD_ROWS, d), lambda i, j: (layer, 0, 1)),
            pl.BlockSpec((None, d, tn), lambda i, j: (0, 0, j)),
            pl.BlockSpec((None, 1, tn), lambda i, j: (layer, 0, j)),
            pl.BlockSpec((None, 2, d, LANES), lambda i, j: (layer, 0, 0, 0)),
            pl.BlockSpec((None, 2, 1, LANES), lambda i, j: (layer, 0, 0, 0)),
        ],
        out_specs=[
            pl.BlockSpec((tm, tn), lambda i, j: (i, j)),
            pl.BlockSpec((2, tm, LANES), lambda i, j: (0, i, 0)),
        ],
        out_shape=[
            jax.ShapeDtypeStruct((t, nc), BF16),
            jax.ShapeDtypeStruct((2, t, LANES), F32),
        ],
        scratch_shapes=[pltpu.VMEM((tm, d), BF16)],
        compiler_params=_cparams(("arbitrary", "arbitrary")),
        name="inproj",
    )(xs, g, mod, mod, w_main, b_main, w_gates, b_gates)


CONV_ROWS = 256
HALO_ROWS = 16


def _conv_kernel(cur_ref, prev_ref, next_ref, w_ref, b_ref, o_ref, *, n_lat_rows, lat_len, ctx_len, k_col_block):
    i = pl.program_id(0)
    j = pl.program_id(1)
    row0 = i * CONV_ROWS
    in_lat = row0 < n_lat_rows
    seg_len = jnp.where(in_lat, lat_len, ctx_len)
    off = jnp.where(in_lat, row0, row0 - n_lat_rows) % seg_len
    has_prev = (off != 0).astype(F32)
    has_next = (off + CONV_ROWS != seg_len).astype(F32)

    x = cur_ref[...].astype(F32)
    prev_row = prev_ref[HALO_ROWS - 1:HALO_ROWS, :].astype(F32) * has_prev
    next_row = next_ref[0:1, :].astype(F32) * has_next
    rows = lax.broadcasted_iota(jnp.int32, x.shape, 0)
    xm1 = jnp.where(rows == 0, prev_row, pltpu.roll(x, 1, 0))
    xp1 = jnp.where(rows == CONV_ROWS - 1, next_row, pltpu.roll(x, CONV_ROWS - 1, 0))
    w = w_ref[...]
    y = w[0:1, :] * xm1 + w[1:2, :] * x + w[2:3, :] * xp1 + b_ref[...]
    y = y * jax.nn.sigmoid(y)
    scale = jnp.where(j >= k_col_block, MLSTM_QK_DIM ** -0.5, 1.0).astype(F32)
    o_ref[...] = (y * scale).astype(o_ref.dtype)


def _conv(proj, conv_w, conv_b, qk_off, n_lat_rows, lat_len, ctx_len):
    t = proj.shape[0]
    width = conv_w.shape[1]
    tc = _pick(width // 2, (1024, 512, 256, 128))
    assert qk_off % tc == 0 and lat_len % CONV_ROWS == 0 and ctx_len % CONV_ROWS == 0
    cb = qk_off // tc
    halo_per_blk = CONV_ROWS // HALO_ROWS
    n_halo = t // HALO_ROWS
    return pl.pallas_call(
        functools.partial(_conv_kernel, n_lat_rows=n_lat_rows, lat_len=lat_len, ctx_len=ctx_len,
                          k_col_block=(width // 2) // tc),
        grid=(t // CONV_ROWS, width // tc),
        in_specs=[
            pl.BlockSpec((CONV_ROWS, tc), lambda i, j: (i, cb + j)),
            pl.BlockSpec((HALO_ROWS, tc), lambda i, j: (jnp.maximum(i * halo_per_blk - 1, 0), cb + j)),
            pl.BlockSpec((HALO_ROWS, tc), lambda i, j: (jnp.minimum((i + 1) * halo_per_blk, n_halo - 1), cb + j)),
            pl.BlockSpec((3, tc), lambda i, j: (0, j)),
            pl.BlockSpec((1, tc), lambda i, j: (0, j)),
        ],
        out_specs=pl.BlockSpec((CONV_ROWS, tc), lambda i, j: (i, j)),
        out_shape=jax.ShapeDtypeStruct((t, width), BF16),
        compiler_params=_cparams(("arbitrary", "arbitrary")),
        name="qk_conv",
    )(proj, proj, proj, conv_w, conv_b.reshape(1, width))


ROPE_ROWS = 256
ROPE_HEAD_GROUP = 4


def _rope_kernel(q_ref, k_ref, cos_ref, s1_ref, s2_ref, gq_ref, gk_ref, qo_ref, ko_ref, *, n_q_heads, n_k_heads):
    cos = cos_ref[...]
    s1 = s1_ref[...]
    s2 = s2_ref[...]

    def prep_heads(src_ref, dst_ref, g, n_heads, scale):
        for h0 in range(0, n_heads, ROPE_HEAD_GROUP):
            sls = [slice(h * HEAD_DIM, (h + 1) * HEAD_DIM) for h in range(h0, min(h0 + ROPE_HEAD_GROUP, n_heads))]
            xs = [src_ref[:, sl].astype(F32) for sl in sls]
            inv = [lax.rsqrt(jnp.mean(x * x, axis=-1, keepdims=True) + EPS) for x in xs]
            xn = [x * r * g for x, r in zip(xs, inv)]
            up = [pltpu.roll(x, HEAD_DIM - ROPE_PAIRS, 1) for x in xn]
            dn = [pltpu.roll(x, ROPE_PAIRS, 1) for x in xn]
            for sl, x, u, dwn in zip(sls, xn, up, dn):
                y = x * cos + u * s1 + dwn * s2
                dst_ref[:, sl] = (y * scale if scale != 1.0 else y).astype(dst_ref.dtype)

    prep_heads(q_ref, qo_ref, gq_ref[...], n_q_heads, ATTN_SCALE)
    prep_heads(k_ref, ko_ref, gk_ref[...], n_k_heads, 1.0)


def _rope(proj, tabs, g_q, g_k, aw, kvw, k_off, n_lat_rows):
    t = proj.shape[0]
    assert k_off % kvw == 0
    kb = k_off // kvw
    n_lat_blk = n_lat_rows // ROPE_ROWS
    lat_blk_per_batch = (tabs[0].shape[0] - ROPE_ROWS) // ROPE_ROWS

    def tab_map(i):
        return (jnp.where(i < n_lat_blk, i % lat_blk_per_batch, lat_blk_per_batch), 0)

    tab_spec = pl.BlockSpec((ROPE_ROWS, HEAD_DIM), tab_map)
    return pl.pallas_call(
        functools.partial(_rope_kernel, n_q_heads=aw // HEAD_DIM, n_k_heads=kvw // HEAD_DIM),
        grid=(t // ROPE_ROWS,),
        in_specs=[
            pl.BlockSpec((ROPE_ROWS, aw), lambda i: (i, 0)),
            pl.BlockSpec((ROPE_ROWS, kvw), lambda i: (i, kb)),
            tab_spec, tab_spec, tab_spec,
            pl.BlockSpec((1, HEAD_DIM), lambda i: (0, 0)),
            pl.BlockSpec((1, HEAD_DIM), lambda i: (0, 0)),
        ],
        out_specs=[
            pl.BlockSpec((ROPE_ROWS, aw), lambda i: (i, 0)),
            pl.BlockSpec((ROPE_ROWS, kvw), lambda i: (i, 0)),
        ],
        out_shape=[jax.ShapeDtypeStruct((t, aw), BF16), jax.ShapeDtypeStruct((t, kvw), BF16)],
        compiler_params=_cparams(("arbitrary",)),
        name="qk_norm_rope",
    )(proj, proj, tabs[0], tabs[1], tabs[2], g_q.reshape(1, HEAD_DIM), g_k.reshape(1, HEAD_DIM))


def _rope_tables(n_lat):
    rows = n_lat // GRID_W
    inv_freq = ROPE_THETA ** (-jnp.arange(ROPE_PAIRS, dtype=F32) / ROPE_PAIRS)
    row_pos = jnp.repeat(jnp.arange(rows, dtype=F32), GRID_W)
    col_pos = jnp.tile(jnp.arange(GRID_W, dtype=F32), rows)
    ang_r = row_pos[:, None] * inv_freq
    ang_c = col_pos[:, None] * inv_freq
    zeros = jnp.zeros_like(ang_r)
    cos = jnp.concatenate([jnp.cos(ang_r), jnp.cos(ang_r), jnp.cos(ang_c), jnp.cos(ang_c)], axis=-1)
    s1 = jnp.concatenate([-jnp.sin(ang_r), zeros, -jnp.sin(ang_c), zeros], axis=-1)
    s2 = jnp.concatenate([zeros, jnp.sin(ang_r), zeros, jnp.sin(ang_c)], axis=-1)
    ident = jnp.ones((ROPE_ROWS, HEAD_DIM), F32)
    zpad = jnp.zeros((ROPE_ROWS, HEAD_DIM), F32)
    return (jnp.concatenate([cos, ident], 0), jnp.concatenate([s1, zpad], 0), jnp.concatenate([s2, zpad], 0))


def _attn_kernel(sink_ref, q_ref, kp_ref, kc_ref, kn_ref, kx_ref, vp_ref, vc_ref, vn_ref, vx_ref, o_ref,
                 *, n_lat_blk, n_kv, group, ctx_len):
    n = pl.program_id(1)
    is_ctx = n >= n_lat_blk
    n_band = 3 * QBLK
    n_keys = n_band + ctx_len
    qi = lax.broadcasted_iota(jnp.int32, (QBLK, n_keys), 0)
    kj = lax.broadcasted_iota(jnp.int32, (QBLK, n_keys), 1)
    rel = kj - QBLK - qi
    kpos = n * QBLK + kj - QBLK
    band_ok = (jnp.abs(rel) <= WINDOW) & (kpos >= 0) & (kpos < n_lat_blk * QBLK) & jnp.logical_not(is_ctx)
    valid = band_ok | (kj >= n_band)

    for hk in range(n_kv):
        ksl = slice(hk * HEAD_DIM, (hk + 1) * HEAD_DIM)
        k_all = jnp.concatenate([kp_ref[:, ksl], kc_ref[:, ksl], kn_ref[:, ksl], kx_ref[:, ksl]], axis=0)
        v_all = jnp.concatenate([vp_ref[:, ksl], vc_ref[:, ksl], vn_ref[:, ksl], vx_ref[:, ksl]], axis=0)
        heads = [hk * group + g for g in range(group)]
        qsl = {h: slice(h * HEAD_DIM, (h + 1) * HEAD_DIM) for h in heads}
        s = {h: jnp.where(valid, lax.dot_general(q_ref[:, qsl[h]], k_all, (((1,), (1,)), ((), ())),
                                                 preferred_element_type=F32), NEG) for h in heads}
        m = {h: jnp.maximum(jnp.max(s[h], axis=-1, keepdims=True), sink_ref[h]) for h in heads}
        p = {h: jnp.exp(s[h] - m[h]) for h in heads}
        denom = {h: jnp.sum(p[h], axis=-1, keepdims=True) + jnp.exp(sink_ref[h] - m[h]) for h in heads}
        o = {h: jnp.dot(p[h].astype(BF16), v_all, preferred_element_type=F32) for h in heads}
        for h in heads:
            o_ref[:, qsl[h]] = (o[h] / denom[h]).astype(o_ref.dtype)


def _attention(sink, qr, kr, proj, aw, kvw, v_off, n_batch, lat_len, ctx_len, with_ctx):
    t = proj.shape[0]
    assert v_off % kvw == 0
    vb = v_off // kvw
    n_lat_blk = lat_len // QBLK
    n_ctx_blk = ctx_len // QBLK
    n_lat_rows = n_batch * lat_len
    nblk = n_lat_blk + (n_ctx_blk if with_ctx else 0)

    def qrow(b, n):
        return jnp.where(n < n_lat_blk, b * n_lat_blk + n, n_lat_rows // QBLK + b * n_ctx_blk + (n - n_lat_blk))

    def band(delta):
        def f(b, n):
            nn = jnp.clip(jnp.minimum(n, n_lat_blk - 1) + delta, 0, n_lat_blk - 1)
            return b * n_lat_blk + nn
        return f

    def ctx_row(b, n):
        return n_lat_rows // ctx_len + b

    def kspec(rowf):
        return pl.BlockSpec((QBLK, kvw), lambda b, n: (rowf(b, n), 0))

    def vspec(rowf):
        return pl.BlockSpec((QBLK, kvw), lambda b, n: (rowf(b, n), vb))

    return pl.pallas_call(
        functools.partial(_attn_kernel, n_lat_blk=n_lat_blk, n_kv=kvw // HEAD_DIM,
                          group=aw // kvw, ctx_len=ctx_len),
        grid=(n_batch, nblk),
        in_specs=[
            pl.BlockSpec(memory_space=pltpu.SMEM),
            pl.BlockSpec((QBLK, aw), lambda b, n: (qrow(b, n), 0)),
            kspec(band(-1)), kspec(band(0)), kspec(band(1)),
            pl.BlockSpec((ctx_len, kvw), lambda b, n: (ctx_row(b, n), 0)),
            vspec(band(-1)), vspec(band(0)), vspec(band(1)),
            pl.BlockSpec((ctx_len, kvw), lambda b, n: (ctx_row(b, n), vb)),
        ],
        out_specs=pl.BlockSpec((QBLK, aw), lambda b, n: (qrow(b, n), 0)),
        out_shape=jax.ShapeDtypeStruct((n_lat_rows + (n_batch * ctx_len if with_ctx else 0), aw), BF16),
        compiler_params=_cparams(("arbitrary", "arbitrary")),
        name="attention",
    )(sink, qr, kr, kr, kr, kr, proj, proj, proj, proj)


HEAD_GROUP = 8


def _mlstm_kernel(q_ref, k_ref, v_ref, g_ref, o_ref, *state, n_heads):
    c_scrs, nm_scrs = state[:n_heads], state[n_heads:]
    d = pl.program_id(0)
    c = pl.program_id(2)
    L = MLSTM_CHUNK
    dk = MLSTM_QK_DIM
    dv = MLSTM_V_DIM

    @pl.when(c == 0)
    def _():
        for scr in state:
            scr[...] = jnp.zeros_like(scr)

    r = lax.broadcasted_iota(jnp.int32, (L, L), 0)
    s = lax.broadcasted_iota(jnp.int32, (L, L), 1)
    fwd = d == 0
    lag = (r - s) * (1 - 2 * d)
    tri = lag >= 0
    tri_t = lag <= 0
    tri_b = tri.astype(BF16)
    tri_tb = tri_t.astype(BF16)

    gates = g_ref[...]
    logf = jnp.minimum(gates, 0.0) - jnp.log1p(jnp.exp(-jnp.abs(gates)))
    gates_t = gates.T
    logf_t = logf.T

    def split_dot_l(mat_b, x):
        hi = x.astype(BF16)
        lo = (x - hi.astype(F32)).astype(BF16)
        return jnp.dot(mat_b, hi, preferred_element_type=F32) + jnp.dot(mat_b, lo, preferred_element_type=F32)

    def split_dot_r(x, mat_b):
        hi = x.astype(BF16)
        lo = (x - hi.astype(F32)).astype(BF16)
        return jnp.dot(hi, mat_b, preferred_element_type=F32) + jnp.dot(lo, mat_b, preferred_element_type=F32)

    cum_col = split_dot_l(tri_b, logf)
    cum_row = split_dot_r(logf_t, tri_tb)
    end_col = jnp.where(fwd, cum_col[L - 1:L, :], cum_col[0:1, :])

    nt = (((1,), (1,)), ((), ()))
    for h0 in range(0, n_heads, HEAD_GROUP):
        hs = range(h0, min(h0 + HEAD_GROUP, n_heads))
        qs = {h: q_ref[:, h * dk:(h + 1) * dk] for h in hs}
        ks = {h: k_ref[:, h * dk:(h + 1) * dk] for h in hs}
        vs = {h: v_ref[:, h * dv:(h + 1) * dv] for h in hs}
        b_col = {h: cum_col[:, n_heads + h:n_heads + h + 1] for h in hs}
        b_row = {h: cum_row[n_heads + h:n_heads + h + 1, :] for h in hs}
        i_col = {h: gates[:, h:h + 1] for h in hs}
        i_row = {h: gates_t[h:h + 1, :] for h in hs}
        b_end = {h: end_col[:, n_heads + h:n_heads + h + 1] for h in hs}
        n_prev = {h: nm_scrs[h][0:1, :] for h in hs}
        m_prev = {h: nm_scrs[h][1:2, 0:1] for h in hs}
        ct_prev = {h: c_scrs[h][...] for h in hs}

        dmat = {h: jnp.where(tri, b_col[h] - b_row[h] + i_row[h], NEG) for h in hs}
        m_inter = {h: b_col[h] + m_prev[h] for h in hs}
        m_t = {h: jnp.maximum(m_inter[h], jnp.max(dmat[h], axis=-1, keepdims=True)) for h in hs}
        qk = {h: lax.dot_general(qs[h], ks[h], nt, preferred_element_type=F32) for h in hs}
        qc = {h: jnp.dot(qs[h], ct_prev[h].astype(BF16), preferred_element_type=F32) for h in hs}
        qn_prev = {h: jnp.sum(qs[h].astype(F32) * n_prev[h], axis=-1, keepdims=True) for h in hs}
        a = {h: jnp.exp(m_inter[h] - m_t[h]) for h in hs}
        smat = {h: qk[h] * jnp.exp(dmat[h] - m_t[h]) for h in hs}
        sv = {h: jnp.dot(smat[h].astype(BF16), vs[h], preferred_element_type=F32) for h in hs}
        qn = {h: jnp.sum(smat[h], axis=-1, keepdims=True) + a[h] * qn_prev[h] for h in hs}

        g_row = {h: b_end[h] - b_row[h] + i_row[h] for h in hs}
        m_new = {h: jnp.maximum(b_end[h] + m_prev[h], jnp.max(g_row[h], axis=-1, keepdims=True)) for h in hs}
        kw = {h: ks[h].astype(F32) * jnp.exp(b_end[h] - b_col[h] + i_col[h] - m_new[h]) for h in hs}
        a_end = {h: jnp.exp(b_end[h] + m_prev[h] - m_new[h]) for h in hs}
        kv = {h: jnp.dot(kw[h].T.astype(BF16), vs[h], preferred_element_type=F32) for h in hs}

        for h in hs:
            hout = (sv[h] + a[h] * qc[h]) / jnp.maximum(jnp.abs(qn[h]), jnp.exp(-m_t[h]))
            o_ref[:, h * dv:(h + 1) * dv] = hout.astype(o_ref.dtype)
        for h in hs:
            c_scrs[h][...] = a_end[h] * ct_prev[h] + kv[h]
            nm_scrs[h][0:1, :] = a_end[h] * n_prev[h] + jnp.sum(kw[h], axis=0, keepdims=True)
            nm_scrs[h][1:2, :] = jnp.broadcast_to(m_new[h], (1, dk))


def _mlstm(qk, proj, gates, mqk, mw, v_off, n_batch, lat_len, ctx_len):
    t = proj.shape[0]
    L = MLSTM_CHUNK
    n_heads = mw // MLSTM_V_DIM
    assert v_off % mw == 0 and mqk == n_heads * MLSTM_QK_DIM
    vb = v_off // mw
    n_ctx = ctx_len // L
    n_lat = lat_len // L
    lat_blocks = n_batch * n_lat

    def row(d, b, c):
        cc = jnp.where(d == 0, c, n_ctx - 1 - c)
        lc = jnp.where(d == 0, c - n_ctx, n_lat - 1 - (c - n_ctx))
        return jnp.where(c < n_ctx, lat_blocks + b * n_ctx + cc, b * n_lat + lc)

    return pl.pallas_call(
        functools.partial(_mlstm_kernel, n_heads=n_heads),
        grid=(2, n_batch, n_ctx + n_lat),
        in_specs=[
            pl.BlockSpec((L, mqk), lambda d, b, c: (row(d, b, c), 0)),
            pl.BlockSpec((L, mqk), lambda d, b, c: (row(d, b, c), 1)),
            pl.BlockSpec((L, mw), lambda d, b, c: (row(d, b, c), vb)),
            pl.BlockSpec((None, L, LANES), lambda d, b, c: (d, row(d, b, c), 0)),
        ],
        out_specs=pl.BlockSpec((None, L, mw), lambda d, b, c: (d, row(d, b, c), 0)),
        out_shape=jax.ShapeDtypeStruct((2, t, mw), BF16),
        scratch_shapes=([pltpu.VMEM((MLSTM_QK_DIM, MLSTM_V_DIM), F32)] * n_heads
                        + [pltpu.VMEM((8, MLSTM_QK_DIM), F32)] * n_heads),
        compiler_params=_cparams(("arbitrary", "arbitrary", "arbitrary")),
        name="mlstm",
    )(qk, qk, proj, gates)


def _branch_kernel(attn_ref, hfb_ref, mo_ref, gmh_ref, wa_ref, wm_ref, ga_ref, gm_ref, o_ref, hm_scr, *, n_heads):
    j = pl.program_id(1)

    @pl.when(j == 0)
    def _():
        dv = MLSTM_V_DIM
        for h in range(n_heads):
            sl = slice(h * dv, (h + 1) * dv)
            hsum = hfb_ref[0, :, sl].astype(F32) + hfb_ref[1, :, sl].astype(F32)
            x = jax.nn.sigmoid(mo_ref[:, sl].astype(F32)) * hsum
            y = x * lax.rsqrt(jnp.mean(x * x, axis=-1, keepdims=True) + EPS) * gmh_ref[:, sl]
            hm_scr[:, sl] = y.astype(BF16)

    ya = jnp.dot(attn_ref[...], wa_ref[...], preferred_element_type=F32)
    ym = jnp.dot(hm_scr[...], wm_ref[...], preferred_element_type=F32)
    u = jax.nn.sigmoid(ga_ref[...].astype(F32)) * ya + jax.nn.sigmoid(gm_ref[...].astype(F32)) * ym
    o_ref[...] = u.astype(o_ref.dtype)


def _branch(attn, hfb, proj, g_mh, wa, wm, layer, mo_off, ga_off, gm_off, rows, tm):
    aw = attn.shape[1]
    mw = hfb.shape[2]
    d = wa.shape[2]
    tn = _pick(d, (1024, 512, 256, 128))
    assert mo_off % mw == 0 and ga_off % tn == 0 and gm_off % tn == 0
    mob, gab, gmb = mo_off // mw, ga_off // tn, gm_off // tn
    return pl.pallas_call(
        functools.partial(_branch_kernel, n_heads=mw // MLSTM_V_DIM),
        grid=(rows // tm, d // tn),
        in_specs=[
            pl.BlockSpec((tm, aw), lambda i, j: (i, 0)),
            pl.BlockSpec((2, tm, mw), lambda i, j: (0, i, 0)),
            pl.BlockSpec((tm, mw), lambda i, j: (i, mob)),
            pl.BlockSpec((1, mw), lambda i, j: (0, 0)),
            pl.BlockSpec((None, aw, tn), lambda i, j: (layer, 0, j)),
            pl.BlockSpec((None, mw, tn), lambda i, j: (layer, 0, j)),
            pl.BlockSpec((tm, tn), lambda i, j: (i, gab + j)),
            pl.BlockSpec((tm, tn), lambda i, j: (i, gmb + j)),
        ],
        out_specs=pl.BlockSpec((tm, tn), lambda i, j: (i, j)),
        out_shape=jax.ShapeDtypeStruct((rows, d), BF16),
        scratch_shapes=[pltpu.VMEM((tm, mw), BF16)],
        compiler_params=_cparams(("arbitrary", "arbitrary")),
        name="branch_merge",
    )(attn, hfb, proj, g_mh.reshape(1, mw), wa, wm, proj, proj)


def _outproj_kernel(u_ref, w_ref, x_ref, gt_ref, o_ref, *, seg_args):
    seg = _seg_of_block(pl.program_id(0), *seg_args)
    y = jnp.dot(u_ref[...], w_ref[...], preferred_element_type=F32)
    o_ref[...] = x_ref[...] + gt_ref[pl.ds(seg, 1), :] * y


def _outproj(u, w_out, xs, mod, layer, tm, seg_args):
    rows, d = u.shape
    tn = _pick(d, (1024, 512, 256, 128))
    gate_blk = 2 * (d // tn)
    return pl.pallas_call(
        functools.partial(_outproj_kernel, seg_args=seg_args),
        grid=(rows // tm, d // tn),
        in_specs=[
            pl.BlockSpec((tm, d), lambda i, j: (i, 0)),
            pl.BlockSpec((None, d, tn), lambda i, j: (layer, 0, j)),
            pl.BlockSpec((tm, tn), lambda i, j: (i, j)),
            pl.BlockSpec((None, MOD_ROWS, tn), lambda i, j: (layer, 0, gate_blk + j)),
        ],
        out_specs=pl.BlockSpec((tm, tn), lambda i, j: (i, j)),
        out_shape=jax.ShapeDtypeStruct(xs.shape, F32),
        input_output_aliases={2: 0},
        compiler_params=_cparams(("arbitrary", "arbitrary")),
        name="outproj_residual",
    )(u, w_out, xs, mod)


PAIR_BLOCK = 2 * LANES


def _pack_bf16_pairs(h):
    blocks = []
    for b in range(h.shape[1] // PAIR_BLOCK):
        hi = pltpu.bitcast(h[:, b * PAIR_BLOCK:b * PAIR_BLOCK + LANES].astype(BF16).astype(F32), jnp.uint32)
        lo = pltpu.bitcast(h[:, b * PAIR_BLOCK + LANES:(b + 1) * PAIR_BLOCK].astype(BF16).astype(F32), jnp.uint32)
        blocks.append(hi | (lo >> 16))
    return blocks[0] if len(blocks) == 1 else jnp.concatenate(blocks, axis=1)


def _unpack_bf16_pairs(p):
    blocks = []
    for b in range(p.shape[1] // LANES):
        w = p[:, b * LANES:(b + 1) * LANES]
        blocks.append(pltpu.bitcast(w & jnp.uint32(0xFFFF0000), F32))
        blocks.append(pltpu.bitcast(w << 16, F32))
    return jnp.concatenate(blocks, axis=1)


ROUTER_ROWS = 256


def _router_kernel(x_ref, g_ref, sh_ref, sc_ref, wr_ref, br_ref, hp_ref, idx_ref, wt_ref, *, seg_args, n_experts):
    seg = _seg_of_block(pl.program_id(0), *seg_args)
    h = _modulated(x_ref[...], g_ref[...], sc_ref[pl.ds(seg, 1), :], sh_ref[pl.ds(seg, 1), :])
    hp_ref[...] = _pack_bf16_pairs(h)

    wr = wr_ref[...]
    h_hi = h.astype(BF16)
    h_lo = (h - h_hi.astype(F32)).astype(BF16)
    w_hi = wr.astype(BF16)
    w_lo = (wr - w_hi.astype(F32)).astype(BF16)
    nt = (((1,), (1,)), ((), ()))
    logits = (lax.dot_general(w_hi, h_hi, nt, preferred_element_type=F32)
              + lax.dot_general(w_hi, h_lo, nt, preferred_element_type=F32)
              + lax.dot_general(w_lo, h_hi, nt, preferred_element_type=F32))
    aff = jax.nn.sigmoid(logits)
    biased = aff + br_ref[...]
    rb = [biased[e:e + 1, :] for e in range(n_experts)]
    ra = [aff[e:e + 1, :] for e in range(n_experts)]

    epg = EXPERTS_PER_GROUP
    scores = []
    for g in range(N_GROUPS):
        a, b, c, d = rb[epg * g:epg * g + epg]
        hi1, lo1 = jnp.maximum(a, b), jnp.minimum(a, b)
        hi2, lo2 = jnp.maximum(c, d), jnp.minimum(c, d)
        scores.append(jnp.maximum(hi1, hi2) + jnp.maximum(jnp.minimum(hi1, hi2), jnp.maximum(lo1, lo2)))
    best = jnp.zeros(scores[0].shape, jnp.int32)
    best_s = scores[0]
    for g in range(1, N_GROUPS):
        upd = scores[g] > best_s
        best = jnp.where(upd, g, best)
        best_s = jnp.where(upd, scores[g], best_s)

    vb, va = [], []
    for j in range(epg):
        xb, xa = rb[j], ra[j]
        for g in range(1, N_GROUPS):
            sel = best == g
            xb = jnp.where(sel, rb[epg * g + j], xb)
            xa = jnp.where(sel, ra[epg * g + j], xa)
        vb.append(xb)
        va.append(xa)

    i1 = jnp.zeros_like(best)
    m1, a1 = vb[0], va[0]
    for j in range(1, epg):
        upd = vb[j] > m1
        i1 = jnp.where(upd, j, i1)
        m1 = jnp.where(upd, vb[j], m1)
        a1 = jnp.where(upd, va[j], a1)
    i2 = jnp.zeros_like(best)
    m2 = jnp.full_like(m1, -jnp.inf)
    a2 = jnp.zeros_like(a1)
    for j in range(epg):
        upd = (i1 != j) & (vb[j] > m2)
        i2 = jnp.where(upd, j, i2)
        m2 = jnp.where(upd, vb[j], m2)
        a2 = jnp.where(upd, va[j], a2)

    idx_ref[0:1, :] = best * epg + i1
    idx_ref[1:2, :] = best * epg + i2
    tot = a1 + a2
    wt_ref[0:1, :] = a1 / tot
    wt_ref[1:2, :] = a2 / tot


def _router(xs, g, mod, layer, w_router_t, b_router, rows, tm, seg_args):
    d = xs.shape[1]
    e = w_router_t.shape[0]
    assert e == N_GROUPS * EXPERTS_PER_GROUP
    return pl.pallas_call(
        functools.partial(_router_kernel, seg_args=seg_args, n_experts=e),
        grid=(rows // tm,),
        in_specs=[
            pl.BlockSpec((tm, d), lambda i: (i, 0)),
            pl.BlockSpec((1, d), lambda i: (0, 0)),
            pl.BlockSpec((None, MOD_ROWS, d), lambda i: (layer, 0, 3)),
            pl.BlockSpec((None, MOD_ROWS, d), lambda i: (layer, 0, 4)),
            pl.BlockSpec((e, d), lambda i: (0, 0)),
            pl.BlockSpec((e, 1), lambda i: (0, 0)),
        ],
        out_specs=[
            pl.BlockSpec((tm, d // 2), lambda i: (i, 0)),
            pl.BlockSpec((2, tm), lambda i: (0, i)),
            pl.BlockSpec((2, tm), lambda i: (0, i)),
        ],
        out_shape=[
            jax.ShapeDtypeStruct((rows, d // 2), jnp.uint32),
            jax.ShapeDtypeStruct((2, rows), jnp.int32),
            jax.ShapeDtypeStruct((2, rows), F32),
        ],
        compiler_params=_cparams(("arbitrary",)),
        name="ffn_modulate_route",
    )(xs, g, mod, mod, w_router_t, b_router.reshape(e, 1))


def _route_kernel(idx_ref, pos_ref, te_ref, nu_ref, *, n_experts, tm, n_chunks):
    e_iota = lax.broadcasted_iota(jnp.int32, (n_experts, LANES), 0)
    idx_all = idx_ref[...]

    def count_col(k):
        col = jnp.zeros((n_experts, LANES), F32)
        for e in range(n_experts):
            col = jnp.where(e_iota == e, jnp.sum((idx_all[k] == e).astype(F32)), col)
        return col

    c0 = count_col(0)
    counts = c0 + count_col(1)
    tiles_per = jnp.floor((counts + (tm - 1)) * (1.0 / tm))
    tile_end = tiles_per
    s = 1
    while s < n_experts:
        tile_end = tile_end + jnp.where(e_iota >= s, pltpu.roll(tile_end, s, 0), 0.0)
        s *= 2
    row_off = (tile_end - tiles_per) * tm

    r = lax.broadcasted_iota(jnp.int32, (LANES, LANES), 0)
    c = lax.broadcasted_iota(jnp.int32, (LANES, LANES), 1)
    triu = (r <= c).astype(BF16)

    def body(ch, carry):
        new = []
        for k in range(2):
            onehot = (e_iota == idx_ref[k, pl.ds(ch, 1), :]).astype(F32)
            csum = jnp.dot(onehot.astype(BF16), triu, preferred_element_type=F32)
            posv = jnp.sum(onehot * (row_off + carry[k] + csum - 1.0), axis=0, keepdims=True)
            pos_ref[k, pl.ds(ch, 1), :] = posv.astype(jnp.int32)
            new.append(carry[k] + csum[:, LANES - 1:LANES])
        return tuple(new)

    lax.fori_loop(0, n_chunks, body, (jnp.zeros((n_experts, LANES), F32), c0))

    t_iota = lax.broadcasted_iota(jnp.int32, (n_experts, te_ref.shape[1]), 1).astype(F32)
    te = jnp.sum((tile_end[:, 0:1] <= t_iota).astype(F32), axis=0, keepdims=True)
    te_ref[...] = jnp.minimum(te, n_experts - 1.0).astype(jnp.int32)
    nu_ref[...] = tile_end[n_experts - 1:n_experts, :].astype(jnp.int32)


def _route(idx, n_experts, tm):
    k, rows = idx.shape
    assert k == 2 and rows % LANES == 0 and tm & (tm - 1) == 0
    n_chunks = rows // LANES
    n_tiles = (k * rows) // tm + n_experts
    te_width = -(-n_tiles // LANES) * LANES
    pos, te, nu = pl.pallas_call(
        functools.partial(_route_kernel, n_experts=n_experts, tm=tm, n_chunks=n_chunks),
        out_shape=[
            jax.ShapeDtypeStruct((k, n_chunks, LANES), jnp.int32),
            jax.ShapeDtypeStruct((1, te_width), jnp.int32),
            jax.ShapeDtypeStruct((1, LANES), jnp.int32),
        ],
        compiler_params=pltpu.CompilerParams(vmem_limit_bytes=VMEM_LIMIT),
        name="moe_route",
    )(idx.reshape(k, n_chunks, LANES))
    return pos.reshape(k * rows), te[0, :n_tiles], nu[0, :1], n_tiles


DISPATCH_ROWS = 256


def _dispatch_kernel(pos_ref, hp_ref, init_ref, hs_ref, sem, *, n_rows):
    del init_ref
    base = pl.program_id(0) * DISPATCH_ROWS

    def row_copy(k, r):
        return pltpu.make_async_copy(hp_ref.at[pl.ds(r, 1)], hs_ref.at[pl.ds(pos_ref[k * n_rows + base + r], 1)], sem)

    def start(r, carry):
        row_copy(0, r).start()
        row_copy(1, r).start()
        return carry

    def wait(r, carry):
        row_copy(0, r).wait()
        row_copy(1, r).wait()
        return carry

    lax.fori_loop(0, DISPATCH_ROWS, start, 0, unroll=8)
    lax.fori_loop(0, DISPATCH_ROWS, wait, 0, unroll=8)


def _dispatch(pos, hp, n_sorted_rows):
    rows, half = hp.shape
    return pl.pallas_call(
        functools.partial(_dispatch_kernel, n_rows=rows),
        grid_spec=pltpu.PrefetchScalarGridSpec(
            num_scalar_prefetch=1,
            grid=(rows // DISPATCH_ROWS,),
            in_specs=[
                pl.BlockSpec((DISPATCH_ROWS, half), lambda i, pos: (i, 0)),
                pl.BlockSpec(memory_space=pl.ANY),
            ],
            out_specs=pl.BlockSpec(memory_space=pl.ANY),
            scratch_shapes=[pltpu.SemaphoreType.DMA],
        ),
        out_shape=jax.ShapeDtypeStruct((n_sorted_rows, half), hp.dtype),
        input_output_aliases={2: 0},
        compiler_params=_cparams(("arbitrary",)),
        name="moe_dispatch",
    )(pos, hp, jnp.zeros((n_sorted_rows, half), hp.dtype))


def _expert_kernel(te_ref, nused_ref, x_ref, wg_ref, wu_ref, wd_ref, o_ref, xs_scr, act_scr):
    i = pl.program_id(0)

    @pl.when(i >= nused_ref[0])
    def _():
        o_ref[...] = jnp.zeros_like(o_ref)

    @pl.when(i < nused_ref[0])
    def _():
        def rows_body(r, carry):
            rs = pl.ds(pl.multiple_of(r * ROW_CHUNK, ROW_CHUNK), ROW_CHUNK)
            xs_scr[rs, :] = _unpack_bf16_pairs(x_ref[rs, :]).astype(BF16)
            return carry

        lax.fori_loop(0, x_ref.shape[0] // ROW_CHUNK, rows_body, 0)

        xs = xs_scr[...]
        for c in range(act_scr.shape[1] // PAIR_BLOCK):
            cs = slice(c * PAIR_BLOCK, (c + 1) * PAIR_BLOCK)
            gate = jnp.dot(xs, wg_ref[:, cs], preferred_element_type=F32)
            up = jnp.dot(xs, wu_ref[:, cs], preferred_element_type=F32)
            act_scr[:, cs] = (gate * jax.nn.sigmoid(gate) * up).astype(BF16)
        act = act_scr[...]
        for c in range(wd_ref.shape[1] // PAIR_BLOCK):
            y = jnp.dot(act, wd_ref[:, c * PAIR_BLOCK:(c + 1) * PAIR_BLOCK], preferred_element_type=F32)
            o_ref[:, c * LANES:(c + 1) * LANES] = _pack_bf16_pairs(y)


def _experts(tile_expert, n_used, hs, wg, wu, wd, layer, tm):
    p, half = hs.shape
    _, e, d, ff = wg.shape
    assert ff % PAIR_BLOCK == 0
    n_tiles = p // tm

    def row(i, te, nu):
        return (jnp.minimum(i, nu[0] - 1), 0)

    def wspec(shape):
        return pl.BlockSpec((None, None) + shape, lambda i, te, nu: (layer, te[i], 0, 0), pipeline_mode=pl.Buffered(1))

    return pl.pallas_call(
        _expert_kernel,
        grid_spec=pltpu.PrefetchScalarGridSpec(
            num_scalar_prefetch=2,
            grid=(n_tiles,),
            in_specs=[pl.BlockSpec((tm, half), row), wspec((d, ff)), wspec((d, ff)), wspec((ff, d))],
            out_specs=pl.BlockSpec((tm, half), lambda i, te, nu: (i, 0)),
            scratch_shapes=[pltpu.VMEM((tm, d), BF16), pltpu.VMEM((tm, ff), BF16)],
        ),
        out_shape=jax.ShapeDtypeStruct((p, half), jnp.uint32),
        compiler_params=_cparams(("arbitrary",)),
        name="moe_experts",
    )(tile_expert, n_used, hs, wg, wu, wd)


COMBINE_ROWS = 256
COMBINE_CHUNK = 8


def _combine_kernel(pos_ref, x_ref, wt_ref, gt_ref, ys_ref, o_ref, buf, sems, *, seg_args, n_rows):
    i = pl.program_id(0)
    slot = i % 2
    seg = _seg_of_block(i, *seg_args)

    def row_copy(blk, sl, k, r):
        return pltpu.make_async_copy(ys_ref.at[pl.ds(pos_ref[k * n_rows + blk * COMBINE_ROWS + r], 1)],
                                     buf.at[sl, k, pl.ds(r, 1)], sems.at[sl])

    def issue_rows(blk, sl, r0):
        for rr in range(COMBINE_CHUNK):
            row_copy(blk, sl, 0, r0 + rr).start()
            row_copy(blk, sl, 1, r0 + rr).start()

    @pl.when(i == 0)
    def _():
        def body(c, carry):
            issue_rows(0, 0, c * COMBINE_CHUNK)
            return carry

        lax.fori_loop(0, COMBINE_ROWS // COMBINE_CHUNK, body, 0)

    def wait(r, carry):
        row_copy(i, slot, 0, r).wait()
        row_copy(i, slot, 1, r).wait()
        return carry

    lax.fori_loop(0, COMBINE_ROWS, wait, 0, unroll=8)

    last = pl.num_programs(0) - 1
    nxt = jnp.minimum(i + 1, last)
    gt = gt_ref[pl.ds(seg, 1), :]

    def body(c, carry):
        r0 = pl.multiple_of(c * COMBINE_CHUNK, COMBINE_CHUNK)
        issue_rows(nxt, 1 - slot, r0)
        rs = pl.ds(r0, COMBINE_CHUNK)
        y0 = _unpack_bf16_pairs(buf[slot, 0, rs, :])
        y1 = _unpack_bf16_pairs(buf[slot, 1, rs, :])
        w = wt_ref[rs, :]
        o_ref[rs, :] = x_ref[rs, :] + gt * (w[:, 0:1] * y0 + w[:, 1:2] * y1)
        return carry

    lax.fori_loop(0, COMBINE_ROWS // COMBINE_CHUNK, body, 0)

    @pl.when(i == last)
    def _():
        def drain(r, carry):
            row_copy(last, 1 - slot, 0, r).wait()
            row_copy(last, 1 - slot, 1, r).wait()
            return carry

        lax.fori_loop(0, COMBINE_ROWS, drain, 0, unroll=8)


def _combine(pos, xs, wts_t, mod, layer, ys, rows, seg_args):
    d = xs.shape[1]
    half = d // 2
    return pl.pallas_call(
        functools.partial(_combine_kernel, seg_args=seg_args, n_rows=rows),
        grid_spec=pltpu.PrefetchScalarGridSpec(
            num_scalar_prefetch=1,
            grid=(rows // COMBINE_ROWS,),
            in_specs=[
                pl.BlockSpec((COMBINE_ROWS, d), lambda i, pos: (i, 0)),
                pl.BlockSpec((COMBINE_ROWS, 2), lambda i, pos: (i, 0)),
                pl.BlockSpec((None, MOD_ROWS, d), lambda i, pos: (layer, 0, 5)),
                pl.BlockSpec(memory_space=pl.ANY),
            ],
            out_specs=pl.BlockSpec((COMBINE_ROWS, d), lambda i, pos: (i, 0)),
            scratch_shapes=[pltpu.VMEM((2, 2, COMBINE_ROWS, half), jnp.uint32), pltpu.SemaphoreType.DMA((2,))],
        ),
        out_shape=jax.ShapeDtypeStruct((rows, d), F32),
        compiler_params=_cparams(("arbitrary",)),
        name="moe_combine",
    )(pos, xs, wts_t, mod, ys)


def kernel(x, c, ctx, c_ctx, w_ada, b_ada, g_mix, g_ffn, w_in, b_in, g_q, g_k, sink, conv_w, conv_b, g_mh,
           w_br_attn, w_br_mlstm, w_out, w_router, b_router, w_gate, w_up, w_down):
    n_batch, lat_len, d = x.shape
    ctx_len = ctx.shape[1]
    depth = w_ada.shape[0]
    d_in = w_in.shape[2]
    aw = w_br_attn.shape[1]
    mw = w_br_mlstm.shape[1]
    mqk = conv_w.shape[2] // 2
    n_mh = mw // MLSTM_V_DIM
    kvw = (d_in - aw - 2 * mqk - 2 * mw - 4 * n_mh - 2 * d) // 2
    n_experts = w_router.shape[1]
    assert n_batch + 1 <= MOD_ROWS and 2 * n_mh <= LANES

    n_lat_rows = n_batch * lat_len
    n_ctx_rows = n_batch * ctx_len
    tm = _pick(n_ctx_rows, (512, 256))
    assert lat_len % tm == 0
    seg_args = (n_lat_rows // tm, lat_len // tm, n_batch)
    tm_e = 256

    o_aq = 0
    o_ak = o_aq + aw
    o_av = o_ak + kvw
    o_mq = o_av + kvw
    o_mk = o_mq + mqk
    o_mv = o_mk + mqk
    o_mo = o_mv + mw
    o_g = o_mo + mw
    o_ga = o_g + 4 * n_mh
    o_gm = o_ga + d
    order = [(o_aq, aw), (o_mv, mw), (o_mo, mw), (o_ga, d), (o_gm, d), (o_ak, kvw), (o_av, kvw), (o_mq, mqk), (o_mk, mqk)]
    starts = [sum(w for _, w in order[:k]) for k in range(len(order))]
    _, n_mv, n_mo, n_ga, n_gm, n_ak, n_av, n_mq, _ = starts

    xs = jnp.concatenate([x.reshape(n_lat_rows, d), ctx.reshape(n_ctx_rows, d)], axis=0)
    cvec = jnp.zeros((MOD_ROWS, d), F32).at[:n_batch].set(c).at[n_batch].set(c_ctx)
    mod = _adaln(cvec, w_ada, b_ada)
    tabs = _rope_tables(lat_len)
    w_router_t = w_router.T

    w_main = [jnp.concatenate([w_in[l][:, o:o + w] for o, w in order], axis=1).astype(BF16)[None] for l in range(depth)]
    b_main = jnp.concatenate([b_in[:, o:o + w] for o, w in order], axis=1).reshape(depth, 1, -1)
    wg = jnp.zeros((depth, 2, d, LANES), F32)
    bg = jnp.zeros((depth, 2, 1, LANES), F32)
    for dr in range(2):
        gsl = slice(o_g + 2 * n_mh * dr, o_g + 2 * n_mh * (dr + 1))
        wg = wg.at[:, dr, :, :2 * n_mh].set(w_in[:, :, gsl])
        bg = bg.at[:, dr, 0, :2 * n_mh].set(b_in[:, gsl])
    wg = wg.astype(BF16)
    wa_b, wm_b, wo_b = w_br_attn.astype(BF16), w_br_mlstm.astype(BF16), w_out.astype(BF16)
    wgate_b, wup_b, wdown_b = w_gate.astype(BF16), w_up.astype(BF16), w_down.astype(BF16)

    for l in range(depth):
        need_ctx = l < depth - 1
        rows = n_lat_rows + (n_ctx_rows if need_ctx else 0)

        proj, gates = _inproj(xs, g_mix[l].reshape(1, d), mod, l, w_main[l], b_main, wg, bg, tm, seg_args)
        qk = _conv(proj, conv_w[l], conv_b[l], n_mq, n_lat_rows, lat_len, ctx_len)
        hfb = _mlstm(qk, proj, gates, mqk, mw, n_mv, n_batch, lat_len, ctx_len)
        qr, kr = _rope(proj, tabs, g_q[l], g_k[l], aw, kvw, n_ak, n_lat_rows)
        attn = _attention(sink[l], qr, kr, proj, aw, kvw, n_av, n_batch, lat_len, ctx_len, need_ctx)
        u = _branch(attn, hfb, proj, g_mh[l], wa_b, wm_b, l, n_mo, n_ga, n_gm, rows, tm)
        xs = _outproj(u, wo_b, xs, mod, l, tm, seg_args)

        hp, idx, wts = _router(xs, g_ffn[l].reshape(1, d), mod, l, w_router_t, b_router, rows, ROUTER_ROWS,
                               (n_lat_rows // ROUTER_ROWS, lat_len // ROUTER_ROWS, n_batch))
        pos, tile_expert, n_used, n_tiles = _route(idx, n_experts, tm_e)
        hs = _dispatch(pos, hp, n_tiles * tm_e)
        ys = _experts(tile_expert, n_used, hs, wgate_b, wup_b, wdown_b, l, tm_e)
        xs = _combine(pos, xs, wts.T, mod, l, ys, rows, (n_lat_rows // COMBINE_ROWS, lat_len // COMBINE_ROWS, n_batch))

    return xs[:n_lat_rows].reshape(n_batch, lat_len, d)
```

```python
import functools

import jax
import jax.numpy as jnp
from jax import lax
from jax.experimental import pallas as pl
from jax.experimental.pallas import tpu as pltpu

GRID_W = 64
HEAD_DIM = 128
WINDOW = 128
QBLK = 128
ROPE_THETA = 10000.0
ROPE_PAIRS = HEAD_DIM // 4
ATTN_SCALE = HEAD_DIM ** -0.5
MLSTM_QK_DIM = 128
MLSTM_V_DIM = 256
MLSTM_CHUNK = 128
N_GROUPS = 4
EXPERTS_PER_GROUP = 4
EPS = 1e-6
NEG = -1e30

LANES = 128
MOD_ROWS = 8
ROW_CHUNK = 64
VMEM_LIMIT = 56 << 20

F32 = jnp.float32
BF16 = jnp.bfloat16


def _pick(n, cands):
    for c in cands:
        if n % c == 0:
            return c
    raise ValueError(f"no tile in {cands} divides {n}")


def _cparams(sem, vmem=VMEM_LIMIT):
    return pltpu.CompilerParams(dimension_semantics=sem, vmem_limit_bytes=vmem)


def _seg_of_block(i, n_lat_blocks, blocks_per_batch, n_batch):
    return jnp.where(i < n_lat_blocks, i // blocks_per_batch, n_batch)


def _modulated(x, g, sc, sh):
    ms = jnp.mean(x * x, axis=-1, keepdims=True)
    y = x * lax.rsqrt(ms + EPS) * g
    return y * (1.0 + sc) + sh


def _adaln_kernel(c_ref, w_ref, b_ref, o_ref):
    c = c_ref[...]
    cs = (c * jax.nn.sigmoid(c)).astype(BF16)
    o_ref[...] = jnp.dot(cs, w_ref[...].astype(BF16), preferred_element_type=F32) + b_ref[...]


def _adaln(cvec, w_ada, b_ada):
    depth, d, n6 = w_ada.shape
    tn = _pick(n6, (512, 256, 128))
    return pl.pallas_call(
        _adaln_kernel,
        grid=(depth, n6 // tn),
        in_specs=[
            pl.BlockSpec((MOD_ROWS, d), lambda l, j: (0, 0)),
            pl.BlockSpec((None, d, tn), lambda l, j: (l, 0, j)),
            pl.BlockSpec((None, 1, tn), lambda l, j: (l, 0, j)),
        ],
        out_specs=pl.BlockSpec((None, MOD_ROWS, tn), lambda l, j: (l, 0, j)),
        out_shape=jax.ShapeDtypeStruct((depth, MOD_ROWS, n6), F32),
        compiler_params=_cparams(("arbitrary", "arbitrary")),
        name="adaln",
    )(cvec, w_ada, b_ada.reshape(depth, 1, n6))


PERMUTE_ROWS = 512
INPROJ_TILES = (1024, 512, 256, 128)


def _permute_cast_kernel(src_blk_ref, nxt_blk_ref, shifted_ref, src_ref, nxt_ref, o_ref, *, shift):
    del src_blk_ref, nxt_blk_ref
    j = pl.program_id(2)
    tw = o_ref.shape[1]

    @pl.when(shifted_ref[j] == 0)
    def _():
        o_ref[...] = src_ref[...].astype(o_ref.dtype)

    if shift:
        @pl.when(shifted_ref[j] != 0)
        def _():
            cat = jnp.concatenate([src_ref[...], nxt_ref[...]], axis=1)
            o_ref[...] = pltpu.roll(cat, tw + LANES - shift, 1)[:, :tw].astype(o_ref.dtype)


def _permute_cast(w_in, order, tw):
    depth, d, d_in = w_in.shape
    src_col = [c for off, width in order for c in range(off, off + width)]
    nc = len(src_col)
    assert nc % tw == 0
    src_blk, nxt_blk, shifted, shifts = [], [], [], set()
    for j in range(nc // tw):
        cols = src_col[j * tw:(j + 1) * tw]
        assert cols == list(range(cols[0], cols[0] + tw)), "a column tile must be contiguous in the source"
        sh = cols[0] % tw
        assert sh < LANES and cols[0] + tw <= d_in
        src_blk.append(cols[0] // tw)
        nxt_blk.append((cols[0] - sh + tw) // LANES if sh else 0)
        shifted.append(1 if sh else 0)
        if sh:
            shifts.add(sh)
    assert len(shifts) <= 1
    shift = shifts.pop() if shifts else 0
    tk = _pick(d, (PERMUTE_ROWS, 256, 128))
    as_i32 = lambda v: jnp.asarray(v, jnp.int32)
    return pl.pallas_call(
        functools.partial(_permute_cast_kernel, shift=shift),
        grid_spec=pltpu.PrefetchScalarGridSpec(
            num_scalar_prefetch=3,
            grid=(depth, d // tk, nc // tw),
            in_specs=[
                pl.BlockSpec((None, tk, tw), lambda l, i, j, sb, nb, sf: (l, i, sb[j])),
                pl.BlockSpec((None, tk, LANES), lambda l, i, j, sb, nb, sf: (l, i, nb[j])),
            ],
            out_specs=pl.BlockSpec((None, tk, tw), lambda l, i, j, sb, nb, sf: (l, i, j)),
        ),
        out_shape=jax.ShapeDtypeStruct((depth, d, nc), BF16),
        compiler_params=_cparams(("arbitrary", "arbitrary", "arbitrary")),
        name="w_in_permute_cast",
    )(as_i32(src_blk), as_i32(nxt_blk), as_i32(shifted), w_in, w_in)


def _inproj_kernel(x_ref, g_ref, sh_ref, sc_ref, w_ref, b_ref, wg_ref, bg_ref, o_ref, og_ref, h_scr, *, seg_args):
    i = pl.program_id(0)
    j = pl.program_id(1)

    @pl.when(j == 0)
    def _():
        seg = _seg_of_block(i, *seg_args)
        g = g_ref[...]
        sc = sc_ref[pl.ds(seg, 1), :]
        sh = sh_ref[pl.ds(seg, 1), :]

        def rows_body(r, carry):
            rs = pl.ds(pl.multiple_of(r * ROW_CHUNK, ROW_CHUNK), ROW_CHUNK)
            h_scr[rs, :] = _modulated(x_ref[rs, :], g, sc, sh).astype(BF16)
            return carry

        lax.fori_loop(0, x_ref.shape[0] // ROW_CHUNK, rows_body, 0)
        for d in range(2):
            og_ref[d] = jnp.dot(h_scr[...], wg_ref[d], preferred_element_type=F32) + bg_ref[d]

    o_ref[...] = (jnp.dot(h_scr[...], w_ref[...], preferred_element_type=F32) + b_ref[...]).astype(o_ref.dtype)


def _inproj(xs, g, mod, layer, w_main, b_main, w_gates, b_gates, tm, seg_args):
    t, d = xs.shape
    nc = w_main.shape[2]
    tn = _pick(nc, INPROJ_TILES)
    return pl.pallas_call(
        functools.partial(_inproj_kernel, seg_args=seg_args),
        grid=(t // tm, nc // tn),
        in_specs=[
            pl.BlockSpec((tm, d), lambda i, j: (i, 0)),
            pl.BlockSpec((1, d), lambda i, j: (0, 0)),
            pl.BlockSpec((None, MOD_ROWS, d), lambda i, j: (layer, 0, 0)),
            pl.BlockSpec((None, MOD_ROWS, d), lambda i, j: (layer, 0, 1)),
            pl.BlockSpec((None, d, tn), lambda i, j: (layer, 0, j)),
            pl.BlockSpec((None, 1, tn), lambda i, j: (layer, 0, j)),
            pl.BlockSpec((None, 2, d, LANES), lambda i, j: (layer, 0, 0, 0)),
            pl.BlockSpec((None, 2, 1, LANES), lambda i, j: (layer, 0, 0, 0)),
        ],
        out_specs=[
            pl.BlockSpec((tm, tn), lambda i, j: (i, j)),
            pl.BlockSpec((2, tm, LANES), lambda i, j: (0, i, 0)),
        ],
        out_shape=[
            jax.ShapeDtypeStruct((t, nc), BF16),
            jax.ShapeDtypeStruct((2, t, LANES), F32),
        ],
        scratch_shapes=[pltpu.VMEM((tm, d), BF16)],
        compiler_params=_cparams(("arbitrary", "arbitrary")),
        name="inproj",
    )(xs, g, mod, mod, w_main, b_main, w_gates, b_gates)


CONV_ROWS = 256
HALO_ROWS = 16


def _conv_kernel(cur_ref, prev_ref, next_ref, w_ref, b_ref, o_ref, *, n_lat_rows, lat_len, ctx_len, k_col_block):
    i = pl.program_id(0)
    j = pl.program_id(1)
    row0 = i * CONV_ROWS
    in_lat = row0 < n_lat_rows
    seg_len = jnp.where(in_lat, lat_len, ctx_len)
    off = jnp.where(in_lat, row0, row0 - n_lat_rows) % seg_len
    has_prev = (off != 0).astype(F32)
    has_next = (off + CONV_ROWS != seg_len).astype(F32)

    x = cur_ref[...].astype(F32)
    prev_row = prev_ref[HALO_ROWS - 1:HALO_ROWS, :].astype(F32) * has_prev
    next_row = next_ref[0:1, :].astype(F32) * has_next
    rows = lax.broadcasted_iota(jnp.int32, x.shape, 0)
    xm1 = jnp.where(rows == 0, prev_row, pltpu.roll(x, 1, 0))
    xp1 = jnp.where(rows == CONV_ROWS - 1, next_row, pltpu.roll(x, CONV_ROWS - 1, 0))
    w = w_ref[...]
    y = w[0:1, :] * xm1 + w[1:2, :] * x + w[2:3, :] * xp1 + b_ref[...]
    y = y * jax.nn.sigmoid(y)
    scale = jnp.where(j >= k_col_block, MLSTM_QK_DIM ** -0.5, 1.0).astype(F32)
    o_ref[...] = (y * scale).astype(o_ref.dtype)


def _conv(proj, conv_w, conv_b, qk_off, n_lat_rows, lat_len, ctx_len):
    t = proj.shape[0]
    width = conv_w.shape[1]
    tc = _pick(width // 2, (1024, 512, 256, 128))
    assert qk_off % tc == 0 and lat_len % CONV_ROWS == 0 and ctx_len % CONV_ROWS == 0
    cb = qk_off // tc
    halo_per_blk = CONV_ROWS // HALO_ROWS
    n_halo = t // HALO_ROWS
    return pl.pallas_call(
        functools.partial(_conv_kernel, n_lat_rows=n_lat_rows, lat_len=lat_len, ctx_len=ctx_len,
                          k_col_block=(width // 2) // tc),
        grid=(t // CONV_ROWS, width // tc),
        in_specs=[
            pl.BlockSpec((CONV_ROWS, tc), lambda i, j: (i, cb + j)),
            pl.BlockSpec((HALO_ROWS, tc), lambda i, j: (jnp.maximum(i * halo_per_blk - 1, 0), cb + j)),
            pl.BlockSpec((HALO_ROWS, tc), lambda i, j: (jnp.minimum((i + 1) * halo_per_blk, n_halo - 1), cb + j)),
            pl.BlockSpec((3, tc), lambda i, j: (0, j)),
            pl.BlockSpec((1, tc), lambda i, j: (0, j)),
        ],
        out_specs=pl.BlockSpec((CONV_ROWS, tc), lambda i, j: (i, j)),
        out_shape=jax.ShapeDtypeStruct((t, width), BF16),
        compiler_params=_cparams(("arbitrary", "arbitrary")),
        name="qk_conv",
    )(proj, proj, proj, conv_w, conv_b.reshape(1, width))


ROPE_ROWS = 256
ROPE_HEAD_GROUP = 4


def _rope_kernel(q_ref, k_ref, cos_ref, s1_ref, s2_ref, gq_ref, gk_ref, qo_ref, ko_ref, *, n_q_heads, n_k_heads):
    cos = cos_ref[...]
    s1 = s1_ref[...]
    s2 = s2_ref[...]

    def prep_heads(src_ref, dst_ref, g, n_heads, scale):
        for h0 in range(0, n_heads, ROPE_HEAD_GROUP):
            sls = [slice(h * HEAD_DIM, (h + 1) * HEAD_DIM) for h in range(h0, min(h0 + ROPE_HEAD_GROUP, n_heads))]
            xs = [src_ref[:, sl].astype(F32) for sl in sls]
            inv = [lax.rsqrt(jnp.mean(x * x, axis=-1, keepdims=True) + EPS) for x in xs]
            xn = [x * r * g for x, r in zip(xs, inv)]
            up = [pltpu.roll(x, HEAD_DIM - ROPE_PAIRS, 1) for x in xn]
            dn = [pltpu.roll(x, ROPE_PAIRS, 1) for x in xn]
            for sl, x, u, dwn in zip(sls, xn, up, dn):
                y = x * cos + u * s1 + dwn * s2
                dst_ref[:, sl] = (y * scale if scale != 1.0 else y).astype(dst_ref.dtype)

    prep_heads(q_ref, qo_ref, gq_ref[...], n_q_heads, ATTN_SCALE)
    prep_heads(k_ref, ko_ref, gk_ref[...], n_k_heads, 1.0)


def _rope(proj, tabs, g_q, g_k, aw, kvw, k_off, n_lat_rows):
    t = proj.shape[0]
    assert k_off % kvw == 0
    kb = k_off // kvw
    n_lat_blk = n_lat_rows // ROPE_ROWS
    lat_blk_per_batch = (tabs[0].shape[0] - ROPE_ROWS) // ROPE_ROWS

    def tab_map(i):
        return (jnp.where(i < n_lat_blk, i % lat_blk_per_batch, lat_blk_per_batch), 0)

    tab_spec = pl.BlockSpec((ROPE_ROWS, HEAD_DIM), tab_map)
    return pl.pallas_call(
        functools.partial(_rope_kernel, n_q_heads=aw // HEAD_DIM, n_k_heads=kvw // HEAD_DIM),
        grid=(t // ROPE_ROWS,),
        in_specs=[
            pl.BlockSpec((ROPE_ROWS, aw), lambda i: (i, 0)),
            pl.BlockSpec((ROPE_ROWS, kvw), lambda i: (i, kb)),
            tab_spec, tab_spec, tab_spec,
            pl.BlockSpec((1, HEAD_DIM), lambda i: (0, 0)),
            pl.BlockSpec((1, HEAD_DIM), lambda i: (0, 0)),
        ],
        out_specs=[
            pl.BlockSpec((ROPE_ROWS, aw), lambda i: (i, 0)),
            pl.BlockSpec((ROPE_ROWS, kvw), lambda i: (i, 0)),
        ],
        out_shape=[jax.ShapeDtypeStruct((t, aw), BF16), jax.ShapeDtypeStruct((t, kvw), BF16)],
        compiler_params=_cparams(("arbitrary",)),
        name="qk_norm_rope",
    )(proj, proj, tabs[0], tabs[1], tabs[2], g_q.reshape(1, HEAD_DIM), g_k.reshape(1, HEAD_DIM))


def _rope_tables(n_lat):
    rows = n_lat // GRID_W
    inv_freq = ROPE_THETA ** (-jnp.arange(ROPE_PAIRS, dtype=F32) / ROPE_PAIRS)
    row_pos = jnp.repeat(jnp.arange(rows, dtype=F32), GRID_W)
    col_pos = jnp.tile(jnp.arange(GRID_W, dtype=F32), rows)
    ang_r = row_pos[:, None] * inv_freq
    ang_c = col_pos[:, None] * inv_freq
    zeros = jnp.zeros_like(ang_r)
    cos = jnp.concatenate([jnp.cos(ang_r), jnp.cos(ang_r), jnp.cos(ang_c), jnp.cos(ang_c)], axis=-1)
    s1 = jnp.concatenate([-jnp.sin(ang_r), zeros, -jnp.sin(ang_c), zeros], axis=-1)
    s2 = jnp.concatenate([zeros, jnp.sin(ang_r), zeros, jnp.sin(ang_c)], axis=-1)
    ident = jnp.ones((ROPE_ROWS, HEAD_DIM), F32)
    zpad = jnp.zeros((ROPE_ROWS, HEAD_DIM), F32)
    return (jnp.concatenate([cos, ident], 0), jnp.concatenate([s1, zpad], 0), jnp.concatenate([s2, zpad], 0))


def _attn_kernel(sink_ref, q_ref, kp_ref, kc_ref, kn_ref, kx_ref, vp_ref, vc_ref, vn_ref, vx_ref, o_ref,
                 *, n_lat_blk, n_kv, group, ctx_len):
    n = pl.program_id(1)
    is_ctx = n >= n_lat_blk
    n_band = 3 * QBLK
    n_keys = n_band + ctx_len
    qi = lax.broadcasted_iota(jnp.int32, (QBLK, n_keys), 0)
    kj = lax.broadcasted_iota(jnp.int32, (QBLK, n_keys), 1)
    rel = kj - QBLK - qi
    kpos = n * QBLK + kj - QBLK
    band_ok = (jnp.abs(rel) <= WINDOW) & (kpos >= 0) & (kpos < n_lat_blk * QBLK) & jnp.logical_not(is_ctx)
    valid = band_ok | (kj >= n_band)

    for hk in range(n_kv):
        ksl = slice(hk * HEAD_DIM, (hk + 1) * HEAD_DIM)
        k_all = jnp.concatenate([kp_ref[:, ksl], kc_ref[:, ksl], kn_ref[:, ksl], kx_ref[:, ksl]], axis=0)
        v_all = jnp.concatenate([vp_ref[:, ksl], vc_ref[:, ksl], vn_ref[:, ksl], vx_ref[:, ksl]], axis=0)
        heads = [hk * group + g for g in range(group)]
        qsl = {h: slice(h * HEAD_DIM, (h + 1) * HEAD_DIM) for h in heads}
        s = {h: jnp.where(valid, lax.dot_general(q_ref[:, qsl[h]], k_all, (((1,), (1,)), ((), ())),
                                                 preferred_element_type=F32), NEG) for h in heads}
        m = {h: jnp.maximum(jnp.max(s[h], axis=-1, keepdims=True), sink_ref[h]) for h in heads}
        p = {h: jnp.exp(s[h] - m[h]) for h in heads}
        denom = {h: jnp.sum(p[h], axis=-1, keepdims=True) + jnp.exp(sink_ref[h] - m[h]) for h in heads}
        o = {h: jnp.dot(p[h].astype(BF16), v_all, preferred_element_type=F32) for h in heads}
        for h in heads:
            o_ref[:, qsl[h]] = (o[h] / denom[h]).astype(o_ref.dtype)


def _attention(sink, qr, kr, proj, aw, kvw, v_off, n_batch, lat_len, ctx_len, with_ctx):
    t = proj.shape[0]
    assert v_off % kvw == 0
    vb = v_off // kvw
    n_lat_blk = lat_len // QBLK
    n_ctx_blk = ctx_len // QBLK
    n_lat_rows = n_batch * lat_len
    nblk = n_lat_blk + (n_ctx_blk if with_ctx else 0)

    def qrow(b, n):
        return jnp.where(n < n_lat_blk, b * n_lat_blk + n, n_lat_rows // QBLK + b * n_ctx_blk + (n - n_lat_blk))

    def band(delta):
        def f(b, n):
            nn = jnp.clip(jnp.minimum(n, n_lat_blk - 1) + delta, 0, n_lat_blk - 1)
            return b * n_lat_blk + nn
        return f

    def ctx_row(b, n):
        return n_lat_rows // ctx_len + b

    def kspec(rowf):
        return pl.BlockSpec((QBLK, kvw), lambda b, n: (rowf(b, n), 0))

    def vspec(rowf):
        return pl.BlockSpec((QBLK, kvw), lambda b, n: (rowf(b, n), vb))

    return pl.pallas_call(
        functools.partial(_attn_kernel, n_lat_blk=n_lat_blk, n_kv=kvw // HEAD_DIM,
                          group=aw // kvw, ctx_len=ctx_len),
        grid=(n_batch, nblk),
        in_specs=[
            pl.BlockSpec(memory_space=pltpu.SMEM),
            pl.BlockSpec((QBLK, aw), lambda b, n: (qrow(b, n), 0)),
            kspec(band(-1)), kspec(band(0)), kspec(band(1)),
            pl.BlockSpec((ctx_len, kvw), lambda b, n: (ctx_row(b, n), 0)),
            vspec(band(-1)), vspec(band(0)), vspec(band(1)),
            pl.BlockSpec((ctx_len, kvw), lambda b, n: (ctx_row(b, n), vb)),
        ],
        out_specs=pl.BlockSpec((QBLK, aw), lambda b, n: (qrow(b, n), 0)),
        out_shape=jax.ShapeDtypeStruct((n_lat_rows + (n_batch * ctx_len if with_ctx else 0), aw), BF16),
        compiler_params=_cparams(("arbitrary", "arbitrary")),
        name="attention",
    )(sink, qr, kr, kr, kr, kr, proj, proj, proj, proj)


MLSTM_GROUP = 8


def _mlstm_kernel(*refs, n_heads):
    ins, outs, state = refs[:8], refs[8:10], refs[10:]
    n_scans = 2 * n_heads
    c_scrs, nm_scrs = state[:n_scans], state[n_scans:]
    c = pl.program_id(1)
    L = MLSTM_CHUNK
    dk = MLSTM_QK_DIM
    dv = MLSTM_V_DIM

    @pl.when(c == 0)
    def _():
        for scr in state:
            scr[...] = jnp.zeros_like(scr)

    r = lax.broadcasted_iota(jnp.int32, (L, L), 0)
    s = lax.broadcasted_iota(jnp.int32, (L, L), 1)

    def split_dot_l(mat_b, x):
        hi = x.astype(BF16)
        lo = (x - hi.astype(F32)).astype(BF16)
        return jnp.dot(mat_b, hi, preferred_element_type=F32) + jnp.dot(mat_b, lo, preferred_element_type=F32)

    def split_dot_r(x, mat_b):
        hi = x.astype(BF16)
        lo = (x - hi.astype(F32)).astype(BF16)
        return jnp.dot(hi, mat_b, preferred_element_type=F32) + jnp.dot(lo, mat_b, preferred_element_type=F32)

    tris, gate_cols, gate_rows, cum_cols, cum_rows, end_cols = [], [], [], [], [], []
    for dirn in range(2):
        tri = (s <= r) if dirn == 0 else (s >= r)
        tri_t = (r <= s) if dirn == 0 else (r >= s)
        gates = ins[4 * dirn + 3][...]
        logf = jnp.minimum(gates, 0.0) - jnp.log1p(jnp.exp(-jnp.abs(gates)))
        cum_col = split_dot_l(tri.astype(BF16), logf)
        cum_row = split_dot_r(logf.T, tri_t.astype(BF16))
        tris.append(tri)
        gate_cols.append(gates)
        gate_rows.append(gates.T)
        cum_cols.append(cum_col)
        cum_rows.append(cum_row)
        end_cols.append(cum_col[L - 1:L, :] if dirn == 0 else cum_col[0:1, :])

    nt = (((1,), (1,)), ((), ()))
    scans = [(dirn, h) for dirn in range(2) for h in range(n_heads)]
    for g0 in range(0, n_scans, MLSTM_GROUP):
        hs = scans[g0:g0 + MLSTM_GROUP]
        sid = {x: x[0] * n_heads + x[1] for x in hs}
        qs = {x: ins[4 * x[0]][:, x[1] * dk:(x[1] + 1) * dk] for x in hs}
        ks = {x: ins[4 * x[0] + 1][:, x[1] * dk:(x[1] + 1) * dk] for x in hs}
        vs = {x: ins[4 * x[0] + 2][:, x[1] * dv:(x[1] + 1) * dv] for x in hs}
        b_col = {x: cum_cols[x[0]][:, n_heads + x[1]:n_heads + x[1] + 1] for x in hs}
        b_row = {x: cum_rows[x[0]][n_heads + x[1]:n_heads + x[1] + 1, :] for x in hs}
        i_col = {x: gate_cols[x[0]][:, x[1]:x[1] + 1] for x in hs}
        i_row = {x: gate_rows[x[0]][x[1]:x[1] + 1, :] for x in hs}
        b_end = {x: end_cols[x[0]][:, n_heads + x[1]:n_heads + x[1] + 1] for x in hs}
        n_prev = {x: nm_scrs[sid[x]][0:1, :] for x in hs}
        m_prev = {x: nm_scrs[sid[x]][1:2, 0:1] for x in hs}
        ct_prev = {x: c_scrs[sid[x]][...] for x in hs}

        dmat = {h: jnp.where(tris[h[0]], b_col[h] - b_row[h] + i_row[h], NEG) for h in hs}
        m_inter = {h: b_col[h] + m_prev[h] for h in hs}
        m_t = {h: jnp.maximum(m_inter[h], jnp.max(dmat[h], axis=-1, keepdims=True)) for h in hs}
        qk = {h: lax.dot_general(qs[h], ks[h], nt, preferred_element_type=F32) for h in hs}
        qc = {h: jnp.dot(qs[h], ct_prev[h].astype(BF16), preferred_element_type=F32) for h in hs}
        qn_prev = {h: jnp.sum(qs[h].astype(F32) * n_prev[h], axis=-1, keepdims=True) for h in hs}
        a = {h: jnp.exp(m_inter[h] - m_t[h]) for h in hs}
        smat = {h: qk[h] * jnp.exp(dmat[h] - m_t[h]) for h in hs}
        sv = {h: jnp.dot(smat[h].astype(BF16), vs[h], preferred_element_type=F32) for h in hs}
        qn = {h: jnp.sum(smat[h], axis=-1, keepdims=True) + a[h] * qn_prev[h] for h in hs}

        g_row = {h: b_end[h] - b_row[h] + i_row[h] for h in hs}
        m_new = {h: jnp.maximum(b_end[h] + m_prev[h], jnp.max(g_row[h], axis=-1, keepdims=True)) for h in hs}
        kw = {h: ks[h].astype(F32) * jnp.exp(b_end[h] - b_col[h] + i_col[h] - m_new[h]) for h in hs}
        a_end = {h: jnp.exp(b_end[h] + m_prev[h] - m_new[h]) for h in hs}
        kv = {h: jnp.dot(kw[h].T.astype(BF16), vs[h], preferred_element_type=F32) for h in hs}

        for h in hs:
            hout = (sv[h] + a[h] * qc[h]) / jnp.maximum(jnp.abs(qn[h]), jnp.exp(-m_t[h]))
            outs[h[0]][:, h[1] * dv:(h[1] + 1) * dv] = hout.astype(outs[h[0]].dtype)
        for h in hs:
            c_scrs[sid[h]][...] = a_end[h] * ct_prev[h] + kv[h]
            nm_scrs[sid[h]][0:1, :] = a_end[h] * n_prev[h] + jnp.sum(kw[h], axis=0, keepdims=True)
            nm_scrs[sid[h]][1:2, :] = jnp.broadcast_to(m_new[h], (1, dk))


def _mlstm(qk, proj, gates, mqk, mw, v_off, n_batch, lat_len, ctx_len):
    t = proj.shape[0]
    L = MLSTM_CHUNK
    n_heads = mw // MLSTM_V_DIM
    assert v_off % mw == 0 and mqk == n_heads * MLSTM_QK_DIM
    vb = v_off // mw
    n_ctx = ctx_len // L
    n_lat = lat_len // L
    lat_blocks = n_batch * n_lat

    def row(d, b, c):
        cc = c if d == 0 else n_ctx - 1 - c
        lc = c - n_ctx if d == 0 else n_lat - 1 - (c - n_ctx)
        return jnp.where(c < n_ctx, lat_blocks + b * n_ctx + cc, b * n_lat + lc)

    def dir_specs(d):
        return [
            pl.BlockSpec((L, mqk), lambda b, c: (row(d, b, c), 0)),
            pl.BlockSpec((L, mqk), lambda b, c: (row(d, b, c), 1)),
            pl.BlockSpec((L, mw), lambda b, c: (row(d, b, c), vb)),
            pl.BlockSpec((None, L, LANES), lambda b, c: (d, row(d, b, c), 0)),
        ]

    return pl.pallas_call(
        functools.partial(_mlstm_kernel, n_heads=n_heads),
        grid=(n_batch, n_ctx + n_lat),
        in_specs=dir_specs(0) + dir_specs(1),
        out_specs=[pl.BlockSpec((L, mw), lambda b, c: (row(0, b, c), 0)),
                   pl.BlockSpec((L, mw), lambda b, c: (row(1, b, c), 0))],
        out_shape=[jax.ShapeDtypeStruct((t, mw), BF16)] * 2,
        scratch_shapes=([pltpu.VMEM((MLSTM_QK_DIM, MLSTM_V_DIM), F32)] * (2 * n_heads)
                        + [pltpu.VMEM((8, MLSTM_QK_DIM), F32)] * (2 * n_heads)),
        compiler_params=_cparams(("arbitrary", "arbitrary")),
        name="mlstm",
    )(qk, qk, proj, gates, qk, qk, proj, gates)


def _branch_kernel(attn_ref, hf_ref, hb_ref, mo_ref, gmh_ref, wa_ref, wm_ref, ga_ref, gm_ref, o_ref, hm_scr, *, n_heads):
    j = pl.program_id(1)

    @pl.when(j == 0)
    def _():
        dv = MLSTM_V_DIM
        for h in range(n_heads):
            sl = slice(h * dv, (h + 1) * dv)
            hsum = hf_ref[:, sl].astype(F32) + hb_ref[:, sl].astype(F32)
            x = jax.nn.sigmoid(mo_ref[:, sl].astype(F32)) * hsum
            y = x * lax.rsqrt(jnp.mean(x * x, axis=-1, keepdims=True) + EPS) * gmh_ref[:, sl]
            hm_scr[:, sl] = y.astype(BF16)

    ya = jnp.dot(attn_ref[...], wa_ref[...], preferred_element_type=F32)
    ym = jnp.dot(hm_scr[...], wm_ref[...], preferred_element_type=F32)
    u = jax.nn.sigmoid(ga_ref[...].astype(F32)) * ya + jax.nn.sigmoid(gm_ref[...].astype(F32)) * ym
    o_ref[...] = u.astype(o_ref.dtype)


def _branch(attn, hfb, proj, g_mh, wa, wm, layer, mo_off, ga_off, gm_off, rows, tm):
    aw = attn.shape[1]
    mw = hfb[0].shape[1]
    d = wa.shape[2]
    tn = _pick(d, (1024, 512, 256, 128))
    assert mo_off % mw == 0 and ga_off % tn == 0 and gm_off % tn == 0
    mob, gab, gmb = mo_off // mw, ga_off // tn, gm_off // tn
    return pl.pallas_call(
        functools.partial(_branch_kernel, n_heads=mw // MLSTM_V_DIM),
        grid=(rows // tm, d // tn),
        in_specs=[
            pl.BlockSpec((tm, aw), lambda i, j: (i, 0)),
            pl.BlockSpec((tm, mw), lambda i, j: (i, 0)),
            pl.BlockSpec((tm, mw), lambda i, j: (i, 0)),
            pl.BlockSpec((tm, mw), lambda i, j: (i, mob)),
            pl.BlockSpec((1, mw), lambda i, j: (0, 0)),
            pl.BlockSpec((None, aw, tn), lambda i, j: (layer, 0, j)),
            pl.BlockSpec((None, mw, tn), lambda i, j: (layer, 0, j)),
            pl.BlockSpec((tm, tn), lambda i, j: (i, gab + j)),
            pl.BlockSpec((tm, tn), lambda i, j: (i, gmb + j)),
        ],
        out_specs=pl.BlockSpec((tm, tn), lambda i, j: (i, j)),
        out_shape=jax.ShapeDtypeStruct((rows, d), BF16),
        scratch_shapes=[pltpu.VMEM((tm, mw), BF16)],
        compiler_params=_cparams(("arbitrary", "arbitrary")),
        name="branch_merge",
    )(attn, hfb[0], hfb[1], proj, g_mh.reshape(1, mw), wa, wm, proj, proj)


def _outproj_kernel(u_ref, w_ref, x_ref, gt_ref, o_ref, *, seg_args):
    seg = _seg_of_block(pl.program_id(0), *seg_args)
    y = jnp.dot(u_ref[...], w_ref[...], preferred_element_type=F32)
    o_ref[...] = x_ref[...] + gt_ref[pl.ds(seg, 1), :] * y


def _outproj(u, w_out, xs, mod, layer, tm, seg_args):
    rows, d = u.shape
    tn = _pick(d, (1024, 512, 256, 128))
    gate_blk = 2 * (d // tn)
    return pl.pallas_call(
        functools.partial(_outproj_kernel, seg_args=seg_args),
        grid=(rows // tm, d // tn),
        in_specs=[
            pl.BlockSpec((tm, d), lambda i, j: (i, 0)),
            pl.BlockSpec((None, d, tn), lambda i, j: (layer, 0, j)),
            pl.BlockSpec((tm, tn), lambda i, j: (i, j)),
            pl.BlockSpec((None, MOD_ROWS, tn), lambda i, j: (layer, 0, gate_blk + j)),
        ],
        out_specs=pl.BlockSpec((tm, tn), lambda i, j: (i, j)),
        out_shape=jax.ShapeDtypeStruct(xs.shape, F32),
        input_output_aliases={2: 0},
        compiler_params=_cparams(("arbitrary", "arbitrary")),
        name="outproj_residual",
    )(u, w_out, xs, mod)


PAIR_BLOCK = 2 * LANES


def _pack_bf16_pairs(h):
    blocks = []
    for b in range(h.shape[1] // PAIR_BLOCK):
        hi = pltpu.bitcast(h[:, b * PAIR_BLOCK:b * PAIR_BLOCK + LANES].astype(BF16).astype(F32), jnp.uint32)
        lo = pltpu.bitcast(h[:, b * PAIR_BLOCK + LANES:(b + 1) * PAIR_BLOCK].astype(BF16).astype(F32), jnp.uint32)
        blocks.append(hi | (lo >> 16))
    return blocks[0] if len(blocks) == 1 else jnp.concatenate(blocks, axis=1)


def _unpack_bf16_pairs(p):
    blocks = []
    for b in range(p.shape[1] // LANES):
        w = p[:, b * LANES:(b + 1) * LANES]
        blocks.append(pltpu.bitcast(w & jnp.uint32(0xFFFF0000), F32))
        blocks.append(pltpu.bitcast(w << 16, F32))
    return jnp.concatenate(blocks, axis=1)


ROUTER_ROWS = 256


def _router_kernel(x_ref, g_ref, sh_ref, sc_ref, wr_ref, br_ref, hp_ref, idx_ref, wt_ref, *, seg_args, n_experts):
    seg = _seg_of_block(pl.program_id(0), *seg_args)
    h = _modulated(x_ref[...], g_ref[...], sc_ref[pl.ds(seg, 1), :], sh_ref[pl.ds(seg, 1), :])
    hp_ref[...] = _pack_bf16_pairs(h)

    wr = wr_ref[...]
    h_hi = h.astype(BF16)
    h_lo = (h - h_hi.astype(F32)).astype(BF16)
    w_hi = wr.astype(BF16)
    w_lo = (wr - w_hi.astype(F32)).astype(BF16)
    nt = (((1,), (1,)), ((), ()))
    logits = (lax.dot_general(w_hi, h_hi, nt, preferred_element_type=F32)
              + lax.dot_general(w_hi, h_lo, nt, preferred_element_type=F32)
              + lax.dot_general(w_lo, h_hi, nt, preferred_element_type=F32))
    aff = jax.nn.sigmoid(logits)
    biased = aff + br_ref[...]
    rb = [biased[e:e + 1, :] for e in range(n_experts)]
    ra = [aff[e:e + 1, :] for e in range(n_experts)]

    epg = EXPERTS_PER_GROUP
    scores = []
    for g in range(N_GROUPS):
        a, b, c, d = rb[epg * g:epg * g + epg]
        hi1, lo1 = jnp.maximum(a, b), jnp.minimum(a, b)
        hi2, lo2 = jnp.maximum(c, d), jnp.minimum(c, d)
        scores.append(jnp.maximum(hi1, hi2) + jnp.maximum(jnp.minimum(hi1, hi2), jnp.maximum(lo1, lo2)))
    best = jnp.zeros(scores[0].shape, jnp.int32)
    best_s = scores[0]
    for g in range(1, N_GROUPS):
        upd = scores[g] > best_s
        best = jnp.where(upd, g, best)
        best_s = jnp.where(upd, scores[g], best_s)

    vb, va = [], []
    for j in range(epg):
        xb, xa = rb[j], ra[j]
        for g in range(1, N_GROUPS):
            sel = best == g
            xb = jnp.where(sel, rb[epg * g + j], xb)
            xa = jnp.where(sel, ra[epg * g + j], xa)
        vb.append(xb)
        va.append(xa)

    i1 = jnp.zeros_like(best)
    m1, a1 = vb[0], va[0]
    for j in range(1, epg):
        upd = vb[j] > m1
        i1 = jnp.where(upd, j, i1)
        m1 = jnp.where(upd, vb[j], m1)
        a1 = jnp.where(upd, va[j], a1)
    i2 = jnp.zeros_like(best)
    m2 = jnp.full_like(m1, -jnp.inf)
    a2 = jnp.zeros_like(a1)
    for j in range(epg):
        upd = (i1 != j) & (vb[j] > m2)
        i2 = jnp.where(upd, j, i2)
        m2 = jnp.where(upd, vb[j], m2)
        a2 = jnp.where(upd, va[j], a2)

    idx_ref[0:1, :] = best * epg + i1
    idx_ref[1:2, :] = best * epg + i2
    tot = a1 + a2
    wt_ref[0:1, :] = a1 / tot
    wt_ref[1:2, :] = a2 / tot


def _router(xs, g, mod, layer, w_router_t, b_router, rows, tm, seg_args):
    d = xs.shape[1]
    e = w_router_t.shape[0]
    assert e == N_GROUPS * EXPERTS_PER_GROUP
    return pl.pallas_call(
        functools.partial(_router_kernel, seg_args=seg_args, n_experts=e),
        grid=(rows // tm,),
        in_specs=[
            pl.BlockSpec((tm, d), lambda i: (i, 0)),
            pl.BlockSpec((1, d), lambda i: (0, 0)),
            pl.BlockSpec((None, MOD_ROWS, d), lambda i: (layer, 0, 3)),
            pl.BlockSpec((None, MOD_ROWS, d), lambda i: (layer, 0, 4)),
            pl.BlockSpec((e, d), lambda i: (0, 0)),
            pl.BlockSpec((e, 1), lambda i: (0, 0)),
        ],
        out_specs=[
            pl.BlockSpec((tm, d // 2), lambda i: (i, 0)),
            pl.BlockSpec((2, tm), lambda i: (0, i)),
            pl.BlockSpec((2, tm), lambda i: (0, i)),
        ],
        out_shape=[
            jax.ShapeDtypeStruct((rows, d // 2), jnp.uint32),
            jax.ShapeDtypeStruct((2, rows), jnp.int32),
            jax.ShapeDtypeStruct((2, rows), F32),
        ],
        compiler_params=_cparams(("arbitrary",)),
        name="ffn_modulate_route",
    )(xs, g, mod, mod, w_router_t, b_router.reshape(e, 1))


def _route_kernel(idx_ref, pos_ref, te_ref, nu_ref, *, n_experts, tm, n_chunks):
    e_iota = lax.broadcasted_iota(jnp.int32, (n_experts, LANES), 0)
    idx_all = idx_ref[...]

    def count_col(k):
        col = jnp.zeros((n_experts, LANES), F32)
        for e in range(n_experts):
            col = jnp.where(e_iota == e, jnp.sum((idx_all[k] == e).astype(F32)), col)
        return col

    c0 = count_col(0)
    counts = c0 + count_col(1)
    tiles_per = jnp.floor((counts + (tm - 1)) * (1.0 / tm))
    tile_end = tiles_per
    s = 1
    while s < n_experts:
        tile_end = tile_end + jnp.where(e_iota >= s, pltpu.roll(tile_end, s, 0), 0.0)
        s *= 2
    row_off = (tile_end - tiles_per) * tm

    r = lax.broadcasted_iota(jnp.int32, (LANES, LANES), 0)
    c = lax.broadcasted_iota(jnp.int32, (LANES, LANES), 1)
    triu = (r <= c).astype(BF16)

    def body(ch, carry):
        new = []
        for k in range(2):
            onehot = (e_iota == idx_ref[k, pl.ds(ch, 1), :]).astype(F32)
            csum = jnp.dot(onehot.astype(BF16), triu, preferred_element_type=F32)
            posv = jnp.sum(onehot * (row_off + carry[k] + csum - 1.0), axis=0, keepdims=True)
            pos_ref[k, pl.ds(ch, 1), :] = posv.astype(jnp.int32)
            new.append(carry[k] + csum[:, LANES - 1:LANES])
        return tuple(new)

    lax.fori_loop(0, n_chunks, body, (jnp.zeros((n_experts, LANES), F32), c0))

    t_iota = lax.broadcasted_iota(jnp.int32, (n_experts, te_ref.shape[1]), 1).astype(F32)
    te = jnp.sum((tile_end[:, 0:1] <= t_iota).astype(F32), axis=0, keepdims=True)
    te_ref[...] = jnp.minimum(te, n_experts - 1.0).astype(jnp.int32)
    nu_ref[...] = tile_end[n_experts - 1:n_experts, :].astype(jnp.int32)


def _route(idx, n_experts, tm):
    k, rows = idx.shape
    assert k == 2 and rows % LANES == 0 and tm & (tm - 1) == 0
    n_chunks = rows // LANES
    n_tiles = (k * rows) // tm + n_experts
    te_width = -(-n_tiles // LANES) * LANES
    pos, te, nu = pl.pallas_call(
        functools.partial(_route_kernel, n_experts=n_experts, tm=tm, n_chunks=n_chunks),
        out_shape=[
            jax.ShapeDtypeStruct((k, n_chunks, LANES), jnp.int32),
            jax.ShapeDtypeStruct((1, te_width), jnp.int32),
            jax.ShapeDtypeStruct((1, LANES), jnp.int32),
        ],
        compiler_params=pltpu.CompilerParams(vmem_limit_bytes=VMEM_LIMIT),
        name="moe_route",
    )(idx.reshape(k, n_chunks, LANES))
    return pos.reshape(k * rows), te[0, :n_tiles], nu[0, :1], n_tiles


DISPATCH_ROWS = 256


def _dispatch_kernel(pos_ref, hp_ref, init_ref, hs_ref, sem, *, n_rows):
    del init_ref
    base = pl.program_id(0) * DISPATCH_ROWS

    def row_copy(k, r):
        return pltpu.make_async_copy(hp_ref.at[pl.ds(r, 1)], hs_ref.at[pl.ds(pos_ref[k * n_rows + base + r], 1)], sem)

    def start(r, carry):
        row_copy(0, r).start()
        row_copy(1, r).start()
        return carry

    def wait(r, carry):
        row_copy(0, r).wait()
        row_copy(1, r).wait()
        return carry

    lax.fori_loop(0, DISPATCH_ROWS, start, 0, unroll=8)
    lax.fori_loop(0, DISPATCH_ROWS, wait, 0, unroll=8)


def _dispatch(pos, hp, n_sorted_rows):
    rows, half = hp.shape
    return pl.pallas_call(
        functools.partial(_dispatch_kernel, n_rows=rows),
        grid_spec=pltpu.PrefetchScalarGridSpec(
            num_scalar_prefetch=1,
            grid=(rows // DISPATCH_ROWS,),
            in_specs=[
                pl.BlockSpec((DISPATCH_ROWS, half), lambda i, pos: (i, 0)),
                pl.BlockSpec(memory_space=pl.ANY),
            ],
            out_specs=pl.BlockSpec(memory_space=pl.ANY),
            scratch_shapes=[pltpu.SemaphoreType.DMA],
        ),
        out_shape=jax.ShapeDtypeStruct((n_sorted_rows, half), hp.dtype),
        input_output_aliases={2: 0},
        compiler_params=_cparams(("arbitrary",)),
        name="moe_dispatch",
    )(pos, hp, jnp.zeros((n_sorted_rows, half), hp.dtype))


def _expert_kernel(te_ref, nused_ref, x_ref, wg_ref, wu_ref, wd_ref, o_ref, xs_scr, act_scr):
    i = pl.program_id(0)

    @pl.when(i >= nused_ref[0])
    def _():
        o_ref[...] = jnp.zeros_like(o_ref)

    @pl.when(i < nused_ref[0])
    def _():
        def rows_body(r, carry):
            rs = pl.ds(pl.multiple_of(r * ROW_CHUNK, ROW_CHUNK), ROW_CHUNK)
            xs_scr[rs, :] = _unpack_bf16_pairs(x_ref[rs, :]).astype(BF16)
            return carry

        lax.fori_loop(0, x_ref.shape[0] // ROW_CHUNK, rows_body, 0)

        xs = xs_scr[...]
        for c in range(act_scr.shape[1] // PAIR_BLOCK):
            cs = slice(c * PAIR_BLOCK, (c + 1) * PAIR_BLOCK)
            gate = jnp.dot(xs, wg_ref[:, cs], preferred_element_type=F32)
            up = jnp.dot(xs, wu_ref[:, cs], preferred_element_type=F32)
            act_scr[:, cs] = (gate * jax.nn.sigmoid(gate) * up).astype(BF16)
        act = act_scr[...]
        for c in range(wd_ref.shape[1] // PAIR_BLOCK):
            y = jnp.dot(act, wd_ref[:, c * PAIR_BLOCK:(c + 1) * PAIR_BLOCK], preferred_element_type=F32)
            o_ref[:, c * LANES:(c + 1) * LANES] = _pack_bf16_pairs(y)


def _experts(tile_expert, n_used, hs, wg, wu, wd, layer, tm):
    p, half = hs.shape
    _, e, d, ff = wg.shape
    assert ff % PAIR_BLOCK == 0
    n_tiles = p // tm

    def row(i, te, nu):
        return (jnp.minimum(i, nu[0] - 1), 0)

    def wspec(shape):
        return pl.BlockSpec((None, None) + shape, lambda i, te, nu: (layer, te[i], 0, 0), pipeline_mode=pl.Buffered(1))

    return pl.pallas_call(
        _expert_kernel,
        grid_spec=pltpu.PrefetchScalarGridSpec(
            num_scalar_prefetch=2,
            grid=(n_tiles,),
            in_specs=[pl.BlockSpec((tm, half), row), wspec((d, ff)), wspec((d, ff)), wspec((ff, d))],
            out_specs=pl.BlockSpec((tm, half), lambda i, te, nu: (i, 0)),
            scratch_shapes=[pltpu.VMEM((tm, d), BF16), pltpu.VMEM((tm, ff), BF16)],
        ),
        out_shape=jax.ShapeDtypeStruct((p, half), jnp.uint32),
        compiler_params=_cparams(("arbitrary",)),
        name="moe_experts",
    )(tile_expert, n_used, hs, wg, wu, wd)


COMBINE_ROWS = 256
COMBINE_CHUNK = 8


def _combine_kernel(pos_ref, x_ref, wt_ref, gt_ref, ys_ref, o_ref, buf, sems, *, seg_args, n_rows):
    i = pl.program_id(0)
    slot = i % 2
    seg = _seg_of_block(i, *seg_args)

    def row_copy(blk, sl, k, r):
        return pltpu.make_async_copy(ys_ref.at[pl.ds(pos_ref[k * n_rows + blk * COMBINE_ROWS + r], 1)],
                                     buf.at[sl, k, pl.ds(r, 1)], sems.at[sl])

    def issue_rows(blk, sl, r0):
        for rr in range(COMBINE_CHUNK):
            row_copy(blk, sl, 0, r0 + rr).start()
            row_copy(blk, sl, 1, r0 + rr).start()

    @pl.when(i == 0)
    def _():
        def body(c, carry):
            issue_rows(0, 0, c * COMBINE_CHUNK)
            return carry

        lax.fori_loop(0, COMBINE_ROWS // COMBINE_CHUNK, body, 0)

    def wait(r, carry):
        row_copy(i, slot, 0, r).wait()
        row_copy(i, slot, 1, r).wait()
        return carry

    lax.fori_loop(0, COMBINE_ROWS, wait, 0, unroll=8)

    last = pl.num_programs(0) - 1
    nxt = jnp.minimum(i + 1, last)
    gt = gt_ref[pl.ds(seg, 1), :]

    def body(c, carry):
        r0 = pl.multiple_of(c * COMBINE_CHUNK, COMBINE_CHUNK)
        issue_rows(nxt, 1 - slot, r0)
        rs = pl.ds(r0, COMBINE_CHUNK)
        y0 = _unpack_bf16_pairs(buf[slot, 0, rs, :])
        y1 = _unpack_bf16_pairs(buf[slot, 1, rs, :])
        w = wt_ref[rs, :]
        o_ref[rs, :] = x_ref[rs, :] + gt * (w[:, 0:1] * y0 + w[:, 1:2] * y1)
        return carry

    lax.fori_loop(0, COMBINE_ROWS // COMBINE_CHUNK, body, 0)

    @pl.when(i == last)
    def _():
        def drain(r, carry):
            row_copy(last, 1 - slot, 0, r).wait()
            row_copy(last, 1 - slot, 1, r).wait()
            return carry

        lax.fori_loop(0, COMBINE_ROWS, drain, 0, unroll=8)


def _combine(pos, xs, wts_t, mod, layer, ys, rows, seg_args):
    d = xs.shape[1]
    half = d // 2
    return pl.pallas_call(
        functools.partial(_combine_kernel, seg_args=seg_args, n_rows=rows),
        grid_spec=pltpu.PrefetchScalarGridSpec(
            num_scalar_prefetch=1,
            grid=(rows // COMBINE_ROWS,),
            in_specs=[
                pl.BlockSpec((COMBINE_ROWS, d), lambda i, pos: (i, 0)),
                pl.BlockSpec((COMBINE_ROWS, 2), lambda i, pos: (i, 0)),
                pl.BlockSpec((None, MOD_ROWS, d), lambda i, pos: (layer, 0, 5)),
                pl.BlockSpec(memory_space=pl.ANY),
            ],
            out_specs=pl.BlockSpec((COMBINE_ROWS, d), lambda i, pos: (i, 0)),
            scratch_shapes=[pltpu.VMEM((2, 2, COMBINE_ROWS, half), jnp.uint32), pltpu.SemaphoreType.DMA((2,))],
        ),
        out_shape=jax.ShapeDtypeStruct((rows, d), F32),
        compiler_params=_cparams(("arbitrary",)),
        name="moe_combine",
    )(pos, xs, wts_t, mod, ys)


def kernel(x, c, ctx, c_ctx, w_ada, b_ada, g_mix, g_ffn, w_in, b_in, g_q, g_k, sink, conv_w, conv_b, g_mh,
           w_br_attn, w_br_mlstm, w_out, w_router, b_router, w_gate, w_up, w_down):
    n_batch, lat_len, d = x.shape
    ctx_len = ctx.shape[1]
    depth = w_ada.shape[0]
    d_in = w_in.shape[2]
    aw = w_br_attn.shape[1]
    mw = w_br_mlstm.shape[1]
    mqk = conv_w.shape[2] // 2
    n_mh = mw // MLSTM_V_DIM
    kvw = (d_in - aw - 2 * mqk - 2 * mw - 4 * n_mh - 2 * d) // 2
    n_experts = w_router.shape[1]
    assert n_batch + 1 <= MOD_ROWS and 2 * n_mh <= LANES

    n_lat_rows = n_batch * lat_len
    n_ctx_rows = n_batch * ctx_len
    tm = _pick(n_ctx_rows, (512, 256))
    assert lat_len % tm == 0
    seg_args = (n_lat_rows // tm, lat_len // tm, n_batch)
    tm_e = 256

    o_aq = 0
    o_ak = o_aq + aw
    o_av = o_ak + kvw
    o_mq = o_av + kvw
    o_mk = o_mq + mqk
    o_mv = o_mk + mqk
    o_mo = o_mv + mw
    o_g = o_mo + mw
    o_ga = o_g + 4 * n_mh
    o_gm = o_ga + d
    order = [(o_aq, aw), (o_mv, mw), (o_mo, mw), (o_ga, d), (o_gm, d), (o_ak, kvw), (o_av, kvw), (o_mq, mqk), (o_mk, mqk)]
    starts = [sum(w for _, w in order[:k]) for k in range(len(order))]
    _, n_mv, n_mo, n_ga, n_gm, n_ak, n_av, n_mq, _ = starts

    xs = jnp.concatenate([x.reshape(n_lat_rows, d), ctx.reshape(n_ctx_rows, d)], axis=0)
    cvec = jnp.zeros((MOD_ROWS, d), F32).at[:n_batch].set(c).at[n_batch].set(c_ctx)
    mod = _adaln(cvec, w_ada, b_ada)
    tabs = _rope_tables(lat_len)
    w_router_t = w_router.T

    w_main = _permute_cast(w_in, order, _pick(sum(w for _, w in order), INPROJ_TILES))
    b_main = jnp.concatenate([b_in[:, o:o + w] for o, w in order], axis=1).reshape(depth, 1, -1)
    wg = jnp.zeros((depth, 2, d, LANES), F32)
    bg = jnp.zeros((depth, 2, 1, LANES), F32)
    for dr in range(2):
        gsl = slice(o_g + 2 * n_mh * dr, o_g + 2 * n_mh * (dr + 1))
        wg = wg.at[:, dr, :, :2 * n_mh].set(w_in[:, :, gsl])
        bg = bg.at[:, dr, 0, :2 * n_mh].set(b_in[:, gsl])
    wg = wg.astype(BF16)
    wa_b, wm_b, wo_b = w_br_attn.astype(BF16), w_br_mlstm.astype(BF16), w_out.astype(BF16)
    wgate_b, wup_b, wdown_b = w_gate.astype(BF16), w_up.astype(BF16), w_down.astype(BF16)

    for l in range(depth):
        need_ctx = l < depth - 1
        rows = n_lat_rows + (n_ctx_rows if need_ctx else 0)

        proj, gates = _inproj(xs, g_mix[l].reshape(1, d), mod, l, w_main, b_main, wg, bg, tm, seg_args)
        qk = _conv(proj, conv_w[l], conv_b[l], n_mq, n_lat_rows, lat_len, ctx_len)
        hfb = _mlstm(qk, proj, gates, mqk, mw, n_mv, n_batch, lat_len, ctx_len)
        qr, kr = _rope(proj, tabs, g_q[l], g_k[l], aw, kvw, n_ak, n_lat_rows)
        attn = _attention(sink[l], qr, kr, proj, aw, kvw, n_av, n_batch, lat_len, ctx_len, need_ctx)
        u = _branch(attn, hfb, proj, g_mh[l], wa_b, wm_b, l, n_mo, n_ga, n_gm, rows, tm)
        xs = _outproj(u, wo_b, xs, mod, l, tm, seg_args)

        hp, idx, wts = _router(xs, g_ffn[l].reshape(1, d), mod, l, w_router_t, b_router, rows, ROUTER_ROWS,
                               (n_lat_rows // ROUTER_ROWS, lat_len // ROUTER_ROWS, n_batch))
        pos, tile_expert, n_used, n_tiles = _route(idx, n_experts, tm_e)
        hs = _dispatch(pos, hp, n_tiles * tm_e)
        ys = _experts(tile_expert, n_used, hs, wgate_b, wup_b, wdown_b, l, tm_e)
        xs = _combine(pos, xs, wts.T, mod, l, ys, rows, (n_lat_rows // COMBINE_ROWS, lat_len // COMBINE_ROWS, n_batch))

    return xs[:n_lat_rows].reshape(n_batch, lat_len, d)
```

```python
import functools
import math

import jax
import jax.numpy as jnp
from jax import lax
from jax.experimental import pallas as pl
from jax.experimental.pallas import tpu as pltpu

GRID_W = 64
HEAD_DIM = 128
WINDOW = 128
QBLK = 128
ROPE_THETA = 10000.0
ROPE_PAIRS = HEAD_DIM // 4
ATTN_SCALE = HEAD_DIM ** -0.5
MLSTM_QK_DIM = 128
MLSTM_V_DIM = 256
MLSTM_CHUNK = 128
N_GROUPS = 4
EXPERTS_PER_GROUP = 4
EPS = 1e-6
NEG = -1e30

LANES = 128
MOD_ROWS = 8
ROW_CHUNK = 64
VMEM_LIMIT = 56 << 20

F32 = jnp.float32
BF16 = jnp.bfloat16


def _pick(n, cands):
    for c in cands:
        if n % c == 0:
            return c
    raise ValueError(f"no tile in {cands} divides {n}")


def _cparams(sem, vmem=VMEM_LIMIT):
    return pltpu.CompilerParams(dimension_semantics=sem, vmem_limit_bytes=vmem)


def _seg_of_block(i, n_lat_blocks, blocks_per_batch, n_batch):
    return jnp.where(i < n_lat_blocks, i // blocks_per_batch, n_batch)


def _modulated(x, g, sc, sh):
    ms = jnp.mean(x * x, axis=-1, keepdims=True)
    y = x * lax.rsqrt(ms + EPS) * g
    return y * (1.0 + sc) + sh


def _adaln_kernel(c_ref, w_ref, b_ref, o_ref):
    c = c_ref[...]
    cs = (c * jax.nn.sigmoid(c)).astype(BF16)
    o_ref[...] = jnp.dot(cs, w_ref[...].astype(BF16), preferred_element_type=F32) + b_ref[...]


def _adaln(cvec, w_ada, b_ada):
    depth, d, n6 = w_ada.shape
    tn = _pick(n6, (512, 256, 128))
    return pl.pallas_call(
        _adaln_kernel,
        grid=(depth, n6 // tn),
        in_specs=[
            pl.BlockSpec((MOD_ROWS, d), lambda l, j: (0, 0)),
            pl.BlockSpec((None, d, tn), lambda l, j: (l, 0, j)),
            pl.BlockSpec((None, 1, tn), lambda l, j: (l, 0, j)),
        ],
        out_specs=pl.BlockSpec((None, MOD_ROWS, tn), lambda l, j: (l, 0, j)),
        out_shape=jax.ShapeDtypeStruct((depth, MOD_ROWS, n6), F32),
        compiler_params=_cparams(("arbitrary", "arbitrary")),
        name="adaln",
    )(cvec, w_ada, b_ada.reshape(depth, 1, n6))


def _wprep_kernel(a_ref, b_ref, o_ref, *, first_shifted, shift):
    j = pl.program_id(2)

    @pl.when(j < first_shifted)
    def _():
        o_ref[...] = a_ref[...].T.astype(o_ref.dtype)

    @pl.when(j >= first_shifted)
    def _():
        rows = jnp.concatenate([a_ref[shift:, :], b_ref[:shift, :]], axis=0)
        o_ref[...] = rows.T.astype(o_ref.dtype)


def _wprep(w_in, cut_start, cut_width):
    depth, d, d_in = w_in.shape
    nc = d_in - cut_width
    tw = _pick(math.gcd(cut_start, nc), (512, 256, 128))
    tk = _pick(d, (1024, 512, 256, 128))
    assert cut_width % 8 == 0 and cut_width <= LANES and tw % LANES == 0
    w_t = jnp.swapaxes(w_in, 1, 2)
    nxt = tw // LANES
    return pl.pallas_call(
        functools.partial(_wprep_kernel, first_shifted=cut_start // tw, shift=cut_width),
        grid=(depth, d // tk, nc // tw),
        in_specs=[
            pl.BlockSpec((None, tw, tk), lambda l, i, j: (l, j, i)),
            pl.BlockSpec((None, LANES, tk), lambda l, i, j: (l, (j + 1) * nxt, i)),
        ],
        out_specs=pl.BlockSpec((None, tk, tw), lambda l, i, j: (l, i, j)),
        out_shape=jax.ShapeDtypeStruct((depth, d, nc), BF16),
        compiler_params=_cparams(("arbitrary", "arbitrary", "arbitrary")),
        name="w_in_relayout",
    )(w_t, w_t)


INPROJ_TILES = (1024, 512, 256, 128)


def _inproj_kernel(x_ref, g_ref, sh_ref, sc_ref, w_ref, b_ref, wg_ref, bg_ref, o_ref, og_ref, h_scr, *, seg_args):
    i = pl.program_id(0)
    j = pl.program_id(1)

    @pl.when(j == 0)
    def _():
        seg = _seg_of_block(i, *seg_args)
        g = g_ref[...]
        sc = sc_ref[pl.ds(seg, 1), :]
        sh = sh_ref[pl.ds(seg, 1), :]

        def rows_body(r, carry):
            rs = pl.ds(pl.multiple_of(r * ROW_CHUNK, ROW_CHUNK), ROW_CHUNK)
            h_scr[rs, :] = _modulated(x_ref[rs, :], g, sc, sh).astype(BF16)
            return carry

        lax.fori_loop(0, x_ref.shape[0] // ROW_CHUNK, rows_body, 0)
        for d in range(2):
            og_ref[d] = jnp.dot(h_scr[...], wg_ref[d], preferred_element_type=F32) + bg_ref[d]

    o_ref[...] = (jnp.dot(h_scr[...], w_ref[...], preferred_element_type=F32) + b_ref[...]).astype(o_ref.dtype)


def _inproj(xs, g, mod, layer, w_main, b_main, w_gates, b_gates, tm, seg_args):
    t, d = xs.shape
    nc = w_main.shape[2]
    tn = _pick(nc, INPROJ_TILES)
    return pl.pallas_call(
        functools.partial(_inproj_kernel, seg_args=seg_args),
        grid=(t // tm, nc // tn),
        in_specs=[
            pl.BlockSpec((tm, d), lambda i, j: (i, 0)),
            pl.BlockSpec((1, d), lambda i, j: (0, 0)),
            pl.BlockSpec((None, MOD_ROWS, d), lambda i, j: (layer, 0, 0)),
            pl.BlockSpec((None, MOD_ROWS, d), lambda i, j: (layer, 0, 1)),
            pl.BlockSpec((None, d, tn), lambda i, j: (layer, 0, j)),
            pl.BlockSpec((None, 1, tn), lambda i, j: (layer, 0, j)),
            pl.BlockSpec((None, 2, d, LANES), lambda i, j: (layer, 0, 0, 0)),
            pl.BlockSpec((None, 2, 1, LANES), lambda i, j: (layer, 0, 0, 0)),
        ],
        out_specs=[
            pl.BlockSpec((tm, tn), lambda i, j: (i, j)),
            pl.BlockSpec((2, tm, LANES), lambda i, j: (0, i, 0)),
        ],
        out_shape=[
            jax.ShapeDtypeStruct((t, nc), BF16),
            jax.ShapeDtypeStruct((2, t, LANES), F32),
        ],
        scratch_shapes=[pltpu.VMEM((tm, d), BF16)],
        compiler_params=_cparams(("arbitrary", "arbitrary")),
        name="inproj",
    )(xs, g, mod, mod, w_main, b_main, w_gates, b_gates)


CONV_ROWS = 256
HALO_ROWS = 16


def _conv_kernel(cur_ref, prev_ref, next_ref, w_ref, b_ref, o_ref, *, n_lat_rows, lat_len, ctx_len, k_col_block):
    i = pl.program_id(0)
    j = pl.program_id(1)
    row0 = i * CONV_ROWS
    in_lat = row0 < n_lat_rows
    seg_len = jnp.where(in_lat, lat_len, ctx_len)
    off = jnp.where(in_lat, row0, row0 - n_lat_rows) % seg_len
    has_prev = (off != 0).astype(F32)
    has_next = (off + CONV_ROWS != seg_len).astype(F32)

    x = cur_ref[...].astype(F32)
    prev_row = prev_ref[HALO_ROWS - 1:HALO_ROWS, :].astype(F32) * has_prev
    next_row = next_ref[0:1, :].astype(F32) * has_next
    rows = lax.broadcasted_iota(jnp.int32, x.shape, 0)
    xm1 = jnp.where(rows == 0, prev_row, pltpu.roll(x, 1, 0))
    xp1 = jnp.where(rows == CONV_ROWS - 1, next_row, pltpu.roll(x, CONV_ROWS - 1, 0))
    w = w_ref[...]
    y = w[0:1, :] * xm1 + w[1:2, :] * x + w[2:3, :] * xp1 + b_ref[...]
    y = y * jax.nn.sigmoid(y)
    scale = jnp.where(j >= k_col_block, MLSTM_QK_DIM ** -0.5, 1.0).astype(F32)
    o_ref[...] = (y * scale).astype(o_ref.dtype)


def _conv(proj, conv_w, conv_b, qk_off, n_lat_rows, lat_len, ctx_len):
    t = proj.shape[0]
    width = conv_w.shape[1]
    tc = _pick(width // 2, (1024, 512, 256, 128))
    assert qk_off % tc == 0 and lat_len % CONV_ROWS == 0 and ctx_len % CONV_ROWS == 0
    cb = qk_off // tc
    halo_per_blk = CONV_ROWS // HALO_ROWS
    n_halo = t // HALO_ROWS
    return pl.pallas_call(
        functools.partial(_conv_kernel, n_lat_rows=n_lat_rows, lat_len=lat_len, ctx_len=ctx_len,
                          k_col_block=(width // 2) // tc),
        grid=(t // CONV_ROWS, width // tc),
        in_specs=[
            pl.BlockSpec((CONV_ROWS, tc), lambda i, j: (i, cb + j)),
            pl.BlockSpec((HALO_ROWS, tc), lambda i, j: (jnp.maximum(i * halo_per_blk - 1, 0), cb + j)),
            pl.BlockSpec((HALO_ROWS, tc), lambda i, j: (jnp.minimum((i + 1) * halo_per_blk, n_halo - 1), cb + j)),
            pl.BlockSpec((3, tc), lambda i, j: (0, j)),
            pl.BlockSpec((1, tc), lambda i, j: (0, j)),
        ],
        out_specs=pl.BlockSpec((CONV_ROWS, tc), lambda i, j: (i, j)),
        out_shape=jax.ShapeDtypeStruct((t, width), BF16),
        compiler_params=_cparams(("arbitrary", "arbitrary")),
        name="qk_conv",
    )(proj, proj, proj, conv_w, conv_b.reshape(1, width))


ROPE_ROWS = 256
ROPE_HEAD_GROUP = 4


def _rope_kernel(q_ref, k_ref, cos_ref, s1_ref, s2_ref, gq_ref, gk_ref, qo_ref, ko_ref, *, n_q_heads, n_k_heads):
    cos = cos_ref[...]
    s1 = s1_ref[...]
    s2 = s2_ref[...]

    def prep_heads(src_ref, dst_ref, g, n_heads, scale):
        for h0 in range(0, n_heads, ROPE_HEAD_GROUP):
            sls = [slice(h * HEAD_DIM, (h + 1) * HEAD_DIM) for h in range(h0, min(h0 + ROPE_HEAD_GROUP, n_heads))]
            xs = [src_ref[:, sl].astype(F32) for sl in sls]
            inv = [lax.rsqrt(jnp.mean(x * x, axis=-1, keepdims=True) + EPS) for x in xs]
            xn = [x * r * g for x, r in zip(xs, inv)]
            up = [pltpu.roll(x, HEAD_DIM - ROPE_PAIRS, 1) for x in xn]
            dn = [pltpu.roll(x, ROPE_PAIRS, 1) for x in xn]
            for sl, x, u, dwn in zip(sls, xn, up, dn):
                y = x * cos + u * s1 + dwn * s2
                dst_ref[:, sl] = (y * scale if scale != 1.0 else y).astype(dst_ref.dtype)

    prep_heads(q_ref, qo_ref, gq_ref[...], n_q_heads, ATTN_SCALE)
    prep_heads(k_ref, ko_ref, gk_ref[...], n_k_heads, 1.0)


def _rope(proj, tabs, g_q, g_k, aw, kvw, k_off, n_lat_rows):
    t = proj.shape[0]
    assert k_off % kvw == 0
    kb = k_off // kvw
    n_lat_blk = n_lat_rows // ROPE_ROWS
    lat_blk_per_batch = (tabs[0].shape[0] - ROPE_ROWS) // ROPE_ROWS

    def tab_map(i):
        return (jnp.where(i < n_lat_blk, i % lat_blk_per_batch, lat_blk_per_batch), 0)

    tab_spec = pl.BlockSpec((ROPE_ROWS, HEAD_DIM), tab_map)
    return pl.pallas_call(
        functools.partial(_rope_kernel, n_q_heads=aw // HEAD_DIM, n_k_heads=kvw // HEAD_DIM),
        grid=(t // ROPE_ROWS,),
        in_specs=[
            pl.BlockSpec((ROPE_ROWS, aw), lambda i: (i, 0)),
            pl.BlockSpec((ROPE_ROWS, kvw), lambda i: (i, kb)),
            tab_spec, tab_spec, tab_spec,
            pl.BlockSpec((1, HEAD_DIM), lambda i: (0, 0)),
            pl.BlockSpec((1, HEAD_DIM), lambda i: (0, 0)),
        ],
        out_specs=[
            pl.BlockSpec((ROPE_ROWS, aw), lambda i: (i, 0)),
            pl.BlockSpec((ROPE_ROWS, kvw), lambda i: (i, 0)),
        ],
        out_shape=[jax.ShapeDtypeStruct((t, aw), BF16), jax.ShapeDtypeStruct((t, kvw), BF16)],
        compiler_params=_cparams(("arbitrary",)),
        name="qk_norm_rope",
    )(proj, proj, tabs[0], tabs[1], tabs[2], g_q.reshape(1, HEAD_DIM), g_k.reshape(1, HEAD_DIM))


def _rope_tables(n_lat):
    rows = n_lat // GRID_W
    inv_freq = ROPE_THETA ** (-jnp.arange(ROPE_PAIRS, dtype=F32) / ROPE_PAIRS)
    row_pos = jnp.repeat(jnp.arange(rows, dtype=F32), GRID_W)
    col_pos = jnp.tile(jnp.arange(GRID_W, dtype=F32), rows)
    ang_r = row_pos[:, None] * inv_freq
    ang_c = col_pos[:, None] * inv_freq
    zeros = jnp.zeros_like(ang_r)
    cos = jnp.concatenate([jnp.cos(ang_r), jnp.cos(ang_r), jnp.cos(ang_c), jnp.cos(ang_c)], axis=-1)
    s1 = jnp.concatenate([-jnp.sin(ang_r), zeros, -jnp.sin(ang_c), zeros], axis=-1)
    s2 = jnp.concatenate([zeros, jnp.sin(ang_r), zeros, jnp.sin(ang_c)], axis=-1)
    ident = jnp.ones((ROPE_ROWS, HEAD_DIM), F32)
    zpad = jnp.zeros((ROPE_ROWS, HEAD_DIM), F32)
    return (jnp.concatenate([cos, ident], 0), jnp.concatenate([s1, zpad], 0), jnp.concatenate([s2, zpad], 0))


def _attn_kernel(sink_ref, q_ref, kp_ref, kc_ref, kn_ref, kx_ref, vp_ref, vc_ref, vn_ref, vx_ref, o_ref,
                 *, n_lat_blk, n_kv, group, ctx_len):
    n = pl.program_id(1)
    is_ctx = n >= n_lat_blk
    n_band = 3 * QBLK
    n_keys = n_band + ctx_len
    qi = lax.broadcasted_iota(jnp.int32, (QBLK, n_keys), 0)
    kj = lax.broadcasted_iota(jnp.int32, (QBLK, n_keys), 1)
    rel = kj - QBLK - qi
    kpos = n * QBLK + kj - QBLK
    band_ok = (jnp.abs(rel) <= WINDOW) & (kpos >= 0) & (kpos < n_lat_blk * QBLK) & jnp.logical_not(is_ctx)
    valid = band_ok | (kj >= n_band)

    for hk in range(n_kv):
        ksl = slice(hk * HEAD_DIM, (hk + 1) * HEAD_DIM)
        k_all = jnp.concatenate([kp_ref[:, ksl], kc_ref[:, ksl], kn_ref[:, ksl], kx_ref[:, ksl]], axis=0)
        v_all = jnp.concatenate([vp_ref[:, ksl], vc_ref[:, ksl], vn_ref[:, ksl], vx_ref[:, ksl]], axis=0)
        heads = [hk * group + g for g in range(group)]
        qsl = {h: slice(h * HEAD_DIM, (h + 1) * HEAD_DIM) for h in heads}
        s = {h: jnp.where(valid, lax.dot_general(q_ref[:, qsl[h]], k_all, (((1,), (1,)), ((), ())),
                                                 preferred_element_type=F32), NEG) for h in heads}
        m = {h: jnp.maximum(jnp.max(s[h], axis=-1, keepdims=True), sink_ref[h]) for h in heads}
        p = {h: jnp.exp(s[h] - m[h]) for h in heads}
        denom = {h: jnp.sum(p[h], axis=-1, keepdims=True) + jnp.exp(sink_ref[h] - m[h]) for h in heads}
        o = {h: jnp.dot(p[h].astype(BF16), v_all, preferred_element_type=F32) for h in heads}
        for h in heads:
            o_ref[:, qsl[h]] = (o[h] / denom[h]).astype(o_ref.dtype)


def _attention(sink, qr, kr, proj, aw, kvw, v_off, n_batch, lat_len, ctx_len, with_ctx):
    t = proj.shape[0]
    assert v_off % kvw == 0
    vb = v_off // kvw
    n_lat_blk = lat_len // QBLK
    n_ctx_blk = ctx_len // QBLK
    n_lat_rows = n_batch * lat_len
    nblk = n_lat_blk + (n_ctx_blk if with_ctx else 0)

    def qrow(b, n):
        return jnp.where(n < n_lat_blk, b * n_lat_blk + n, n_lat_rows // QBLK + b * n_ctx_blk + (n - n_lat_blk))

    def band(delta):
        def f(b, n):
            nn = jnp.clip(jnp.minimum(n, n_lat_blk - 1) + delta, 0, n_lat_blk - 1)
            return b * n_lat_blk + nn
        return f

    def ctx_row(b, n):
        return n_lat_rows // ctx_len + b

    def kspec(rowf):
        return pl.BlockSpec((QBLK, kvw), lambda b, n: (rowf(b, n), 0))

    def vspec(rowf):
        return pl.BlockSpec((QBLK, kvw), lambda b, n: (rowf(b, n), vb))

    return pl.pallas_call(
        functools.partial(_attn_kernel, n_lat_blk=n_lat_blk, n_kv=kvw // HEAD_DIM,
                          group=aw // kvw, ctx_len=ctx_len),
        grid=(n_batch, nblk),
        in_specs=[
            pl.BlockSpec(memory_space=pltpu.SMEM),
            pl.BlockSpec((QBLK, aw), lambda b, n: (qrow(b, n), 0)),
            kspec(band(-1)), kspec(band(0)), kspec(band(1)),
            pl.BlockSpec((ctx_len, kvw), lambda b, n: (ctx_row(b, n), 0)),
            vspec(band(-1)), vspec(band(0)), vspec(band(1)),
            pl.BlockSpec((ctx_len, kvw), lambda b, n: (ctx_row(b, n), vb)),
        ],
        out_specs=pl.BlockSpec((QBLK, aw), lambda b, n: (qrow(b, n), 0)),
        out_shape=jax.ShapeDtypeStruct((n_lat_rows + (n_batch * ctx_len if with_ctx else 0), aw), BF16),
        compiler_params=_cparams(("arbitrary", "arbitrary")),
        name="attention",
    )(sink, qr, kr, kr, kr, kr, proj, proj, proj, proj)


MLSTM_GROUP = 8


def _mlstm_kernel(*refs, n_heads, n_vblk):
    per_dir = 3 + n_vblk
    dir_ins = [refs[:per_dir], refs[per_dir:2 * per_dir]]
    outs, state = refs[2 * per_dir:2 * per_dir + 2], refs[2 * per_dir + 2:]
    heads_per_vblk = n_heads // n_vblk
    n_scans = 2 * n_heads
    c_scrs, nm_scrs = state[:n_scans], state[n_scans:]
    c = pl.program_id(1)
    L = MLSTM_CHUNK
    dk = MLSTM_QK_DIM
    dv = MLSTM_V_DIM

    @pl.when(c == 0)
    def _():
        for scr in state:
            scr[...] = jnp.zeros_like(scr)

    r = lax.broadcasted_iota(jnp.int32, (L, L), 0)
    s = lax.broadcasted_iota(jnp.int32, (L, L), 1)

    def split_dot_l(mat_b, x):
        hi = x.astype(BF16)
        lo = (x - hi.astype(F32)).astype(BF16)
        return jnp.dot(mat_b, hi, preferred_element_type=F32) + jnp.dot(mat_b, lo, preferred_element_type=F32)

    def split_dot_r(x, mat_b):
        hi = x.astype(BF16)
        lo = (x - hi.astype(F32)).astype(BF16)
        return jnp.dot(hi, mat_b, preferred_element_type=F32) + jnp.dot(lo, mat_b, preferred_element_type=F32)

    tris, gate_cols, gate_rows, cum_cols, cum_rows, end_cols = [], [], [], [], [], []
    for dirn in range(2):
        tri = (s <= r) if dirn == 0 else (s >= r)
        tri_t = (r <= s) if dirn == 0 else (r >= s)
        gates = dir_ins[dirn][2][...]
        logf = jnp.minimum(gates, 0.0) - jnp.log1p(jnp.exp(-jnp.abs(gates)))
        cum_col = split_dot_l(tri.astype(BF16), logf)
        cum_row = split_dot_r(logf.T, tri_t.astype(BF16))
        tris.append(tri)
        gate_cols.append(gates)
        gate_rows.append(gates.T)
        cum_cols.append(cum_col)
        cum_rows.append(cum_row)
        end_cols.append(cum_col[L - 1:L, :] if dirn == 0 else cum_col[0:1, :])

    nt = (((1,), (1,)), ((), ()))
    scans = [(dirn, h) for dirn in range(2) for h in range(n_heads)]
    for g0 in range(0, n_scans, MLSTM_GROUP):
        hs = scans[g0:g0 + MLSTM_GROUP]
        sid = {x: x[0] * n_heads + x[1] for x in hs}
        qs = {x: dir_ins[x[0]][0][:, x[1] * dk:(x[1] + 1) * dk] for x in hs}
        ks = {x: dir_ins[x[0]][1][:, x[1] * dk:(x[1] + 1) * dk] for x in hs}
        vs = {x: dir_ins[x[0]][3 + x[1] // heads_per_vblk][:, (x[1] % heads_per_vblk) * dv:
                                                           (x[1] % heads_per_vblk + 1) * dv] for x in hs}
        b_col = {x: cum_cols[x[0]][:, n_heads + x[1]:n_heads + x[1] + 1] for x in hs}
        b_row = {x: cum_rows[x[0]][n_heads + x[1]:n_heads + x[1] + 1, :] for x in hs}
        i_col = {x: gate_cols[x[0]][:, x[1]:x[1] + 1] for x in hs}
        i_row = {x: gate_rows[x[0]][x[1]:x[1] + 1, :] for x in hs}
        b_end = {x: end_cols[x[0]][:, n_heads + x[1]:n_heads + x[1] + 1] for x in hs}
        n_prev = {x: nm_scrs[sid[x]][0:1, :] for x in hs}
        m_prev = {x: nm_scrs[sid[x]][1:2, 0:1] for x in hs}
        ct_prev = {x: c_scrs[sid[x]][...] for x in hs}

        dmat = {h: jnp.where(tris[h[0]], b_col[h] - b_row[h] + i_row[h], NEG) for h in hs}
        m_inter = {h: b_col[h] + m_prev[h] for h in hs}
        m_t = {h: jnp.maximum(m_inter[h], jnp.max(dmat[h], axis=-1, keepdims=True)) for h in hs}
        qk = {h: lax.dot_general(qs[h], ks[h], nt, preferred_element_type=F32) for h in hs}
        qc = {h: jnp.dot(qs[h], ct_prev[h].astype(BF16), preferred_element_type=F32) for h in hs}
        qn_prev = {h: jnp.sum(qs[h].astype(F32) * n_prev[h], axis=-1, keepdims=True) for h in hs}
        a = {h: jnp.exp(m_inter[h] - m_t[h]) for h in hs}
        smat = {h: qk[h] * jnp.exp(dmat[h] - m_t[h]) for h in hs}
        sv = {h: jnp.dot(smat[h].astype(BF16), vs[h], preferred_element_type=F32) for h in hs}
        qn = {h: jnp.sum(smat[h], axis=-1, keepdims=True) + a[h] * qn_prev[h] for h in hs}

        g_row = {h: b_end[h] - b_row[h] + i_row[h] for h in hs}
        m_new = {h: jnp.maximum(b_end[h] + m_prev[h], jnp.max(g_row[h], axis=-1, keepdims=True)) for h in hs}
        kw = {h: ks[h].astype(F32) * jnp.exp(b_end[h] - b_col[h] + i_col[h] - m_new[h]) for h in hs}
        a_end = {h: jnp.exp(b_end[h] + m_prev[h] - m_new[h]) for h in hs}
        kv = {h: jnp.dot(kw[h].T.astype(BF16), vs[h], preferred_element_type=F32) for h in hs}

        for h in hs:
            hout = (sv[h] + a[h] * qc[h]) / jnp.maximum(jnp.abs(qn[h]), jnp.exp(-m_t[h]))
            outs[h[0]][:, h[1] * dv:(h[1] + 1) * dv] = hout.astype(outs[h[0]].dtype)
        for h in hs:
            c_scrs[sid[h]][...] = a_end[h] * ct_prev[h] + kv[h]
            nm_scrs[sid[h]][0:1, :] = a_end[h] * n_prev[h] + jnp.sum(kw[h], axis=0, keepdims=True)
            nm_scrs[sid[h]][1:2, :] = jnp.broadcast_to(m_new[h], (1, dk))


def _mlstm(qk, proj, gates, mqk, mw, v_off, n_batch, lat_len, ctx_len):
    t = proj.shape[0]
    L = MLSTM_CHUNK
    n_heads = mw // MLSTM_V_DIM
    assert mqk == n_heads * MLSTM_QK_DIM
    vw = math.gcd(mw, v_off)
    n_vblk = mw // vw
    assert vw % MLSTM_V_DIM == 0
    n_ctx = ctx_len // L
    n_lat = lat_len // L
    lat_blocks = n_batch * n_lat

    def row(d, b, c):
        cc = c if d == 0 else n_ctx - 1 - c
        lc = c - n_ctx if d == 0 else n_lat - 1 - (c - n_ctx)
        return jnp.where(c < n_ctx, lat_blocks + b * n_ctx + cc, b * n_lat + lc)

    def dir_specs(d):
        return [
            pl.BlockSpec((L, mqk), lambda b, c: (row(d, b, c), 0)),
            pl.BlockSpec((L, mqk), lambda b, c: (row(d, b, c), 1)),
            pl.BlockSpec((None, L, LANES), lambda b, c: (d, row(d, b, c), 0)),
        ] + [pl.BlockSpec((L, vw), lambda b, c, j=j: (row(d, b, c), v_off // vw + j)) for j in range(n_vblk)]

    dir_args = [qk, qk, gates] + [proj] * n_vblk
    return pl.pallas_call(
        functools.partial(_mlstm_kernel, n_heads=n_heads, n_vblk=n_vblk),
        grid=(n_batch, n_ctx + n_lat),
        in_specs=dir_specs(0) + dir_specs(1),
        out_specs=[pl.BlockSpec((L, mw), lambda b, c: (row(0, b, c), 0)),
                   pl.BlockSpec((L, mw), lambda b, c: (row(1, b, c), 0))],
        out_shape=[jax.ShapeDtypeStruct((t, mw), BF16)] * 2,
        scratch_shapes=([pltpu.VMEM((MLSTM_QK_DIM, MLSTM_V_DIM), F32)] * (2 * n_heads)
                        + [pltpu.VMEM((8, MLSTM_QK_DIM), F32)] * (2 * n_heads)),
        compiler_params=_cparams(("arbitrary", "arbitrary")),
        name="mlstm",
    )(*dir_args, *dir_args)


def _branch_kernel(attn_ref, hf_ref, hb_ref, gmh_ref, wa_ref, wm_ref, ga_ref, gm_ref, *rest, n_heads):
    mo_refs, o_ref, hm_scr = rest[:-2], rest[-2], rest[-1]
    heads_per_blk = n_heads // len(mo_refs)
    j = pl.program_id(1)

    @pl.when(j == 0)
    def _():
        dv = MLSTM_V_DIM
        for h in range(n_heads):
            sl = slice(h * dv, (h + 1) * dv)
            mo = mo_refs[h // heads_per_blk][:, (h % heads_per_blk) * dv:(h % heads_per_blk + 1) * dv]
            hsum = hf_ref[:, sl].astype(F32) + hb_ref[:, sl].astype(F32)
            x = jax.nn.sigmoid(mo.astype(F32)) * hsum
            y = x * lax.rsqrt(jnp.mean(x * x, axis=-1, keepdims=True) + EPS) * gmh_ref[:, sl]
            hm_scr[:, sl] = y.astype(BF16)

    ya = jnp.dot(attn_ref[...], wa_ref[...], preferred_element_type=F32)
    ym = jnp.dot(hm_scr[...], wm_ref[...], preferred_element_type=F32)
    u = jax.nn.sigmoid(ga_ref[...].astype(F32)) * ya + jax.nn.sigmoid(gm_ref[...].astype(F32)) * ym
    o_ref[...] = u.astype(o_ref.dtype)


def _branch(attn, hfb, proj, g_mh, wa, wm, layer, mo_off, ga_off, gm_off, rows, tm):
    aw = attn.shape[1]
    mw = hfb[0].shape[1]
    d = wa.shape[2]
    tn = _pick(d, (1024, 512, 256, 128))
    assert ga_off % tn == 0 and gm_off % tn == 0
    gab, gmb = ga_off // tn, gm_off // tn
    mo_w = math.gcd(mw, mo_off)
    n_mo_blk = mw // mo_w
    assert mo_w % MLSTM_V_DIM == 0
    return pl.pallas_call(
        functools.partial(_branch_kernel, n_heads=mw // MLSTM_V_DIM),
        grid=(rows // tm, d // tn),
        in_specs=[
            pl.BlockSpec((tm, aw), lambda i, j: (i, 0)),
            pl.BlockSpec((tm, mw), lambda i, j: (i, 0)),
            pl.BlockSpec((tm, mw), lambda i, j: (i, 0)),
            pl.BlockSpec((1, mw), lambda i, j: (0, 0)),
            pl.BlockSpec((None, aw, tn), lambda i, j: (layer, 0, j)),
            pl.BlockSpec((None, mw, tn), lambda i, j: (layer, 0, j)),
            pl.BlockSpec((tm, tn), lambda i, j: (i, gab + j)),
            pl.BlockSpec((tm, tn), lambda i, j: (i, gmb + j)),
        ] + [pl.BlockSpec((tm, mo_w), lambda i, j, b=b: (i, mo_off // mo_w + b)) for b in range(n_mo_blk)],
        out_specs=pl.BlockSpec((tm, tn), lambda i, j: (i, j)),
        out_shape=jax.ShapeDtypeStruct((rows, d), BF16),
        scratch_shapes=[pltpu.VMEM((tm, mw), BF16)],
        compiler_params=_cparams(("arbitrary", "arbitrary")),
        name="branch_merge",
    )(attn, hfb[0], hfb[1], g_mh.reshape(1, mw), wa, wm, proj, proj, *([proj] * n_mo_blk))


def _outproj_kernel(u_ref, w_ref, x_ref, gt_ref, o_ref, *, seg_args):
    seg = _seg_of_block(pl.program_id(0), *seg_args)
    y = jnp.dot(u_ref[...], w_ref[...], preferred_element_type=F32)
    o_ref[...] = x_ref[...] + gt_ref[pl.ds(seg, 1), :] * y


def _outproj(u, w_out, xs, mod, layer, tm, seg_args):
    rows, d = u.shape
    tn = _pick(d, (1024, 512, 256, 128))
    gate_blk = 2 * (d // tn)
    return pl.pallas_call(
        functools.partial(_outproj_kernel, seg_args=seg_args),
        grid=(rows // tm, d // tn),
        in_specs=[
            pl.BlockSpec((tm, d), lambda i, j: (i, 0)),
            pl.BlockSpec((None, d, tn), lambda i, j: (layer, 0, j)),
            pl.BlockSpec((tm, tn), lambda i, j: (i, j)),
            pl.BlockSpec((None, MOD_ROWS, tn), lambda i, j: (layer, 0, gate_blk + j)),
        ],
        out_specs=pl.BlockSpec((tm, tn), lambda i, j: (i, j)),
        out_shape=jax.ShapeDtypeStruct(xs.shape, F32),
        input_output_aliases={2: 0},
        compiler_params=_cparams(("arbitrary", "arbitrary")),
        name="outproj_residual",
    )(u, w_out, xs, mod)


PAIR_BLOCK = 2 * LANES


def _pack_bf16_pairs(h):
    blocks = []
    for b in range(h.shape[1] // PAIR_BLOCK):
        hi = pltpu.bitcast(h[:, b * PAIR_BLOCK:b * PAIR_BLOCK + LANES].astype(BF16).astype(F32), jnp.uint32)
        lo = pltpu.bitcast(h[:, b * PAIR_BLOCK + LANES:(b + 1) * PAIR_BLOCK].astype(BF16).astype(F32), jnp.uint32)
        blocks.append(hi | (lo >> 16))
    return blocks[0] if len(blocks) == 1 else jnp.concatenate(blocks, axis=1)


def _unpack_bf16_pairs(p):
    blocks = []
    for b in range(p.shape[1] // LANES):
        w = p[:, b * LANES:(b + 1) * LANES]
        blocks.append(pltpu.bitcast(w & jnp.uint32(0xFFFF0000), F32))
        blocks.append(pltpu.bitcast(w << 16, F32))
    return jnp.concatenate(blocks, axis=1)


ROUTER_ROWS = 256


def _router_kernel(x_ref, g_ref, sh_ref, sc_ref, wr_ref, br_ref, hp_ref, idx_ref, wt_ref, *, seg_args, n_experts):
    seg = _seg_of_block(pl.program_id(0), *seg_args)
    h = _modulated(x_ref[...], g_ref[...], sc_ref[pl.ds(seg, 1), :], sh_ref[pl.ds(seg, 1), :])
    hp_ref[...] = _pack_bf16_pairs(h)

    wr = wr_ref[...]
    h_hi = h.astype(BF16)
    h_lo = (h - h_hi.astype(F32)).astype(BF16)
    w_hi = wr.astype(BF16)
    w_lo = (wr - w_hi.astype(F32)).astype(BF16)
    nt = (((1,), (1,)), ((), ()))
    logits = (lax.dot_general(w_hi, h_hi, nt, preferred_element_type=F32)
              + lax.dot_general(w_hi, h_lo, nt, preferred_element_type=F32)
              + lax.dot_general(w_lo, h_hi, nt, preferred_element_type=F32))
    aff = jax.nn.sigmoid(logits)
    biased = aff + br_ref[...]
    rb = [biased[e:e + 1, :] for e in range(n_experts)]
    ra = [aff[e:e + 1, :] for e in range(n_experts)]

    epg = EXPERTS_PER_GROUP
    scores = []
    for g in range(N_GROUPS):
        a, b, c, d = rb[epg * g:epg * g + epg]
        hi1, lo1 = jnp.maximum(a, b), jnp.minimum(a, b)
        hi2, lo2 = jnp.maximum(c, d), jnp.minimum(c, d)
        scores.append(jnp.maximum(hi1, hi2) + jnp.maximum(jnp.minimum(hi1, hi2), jnp.maximum(lo1, lo2)))
    best = jnp.zeros(scores[0].shape, jnp.int32)
    best_s = scores[0]
    for g in range(1, N_GROUPS):
        upd = scores[g] > best_s
        best = jnp.where(upd, g, best)
        best_s = jnp.where(upd, scores[g], best_s)

    vb, va = [], []
    for j in range(epg):
        xb, xa = rb[j], ra[j]
        for g in range(1, N_GROUPS):
            sel = best == g
            xb = jnp.where(sel, rb[epg * g + j], xb)
            xa = jnp.where(sel, ra[epg * g + j], xa)
        vb.append(xb)
        va.append(xa)

    i1 = jnp.zeros_like(best)
    m1, a1 = vb[0], va[0]
    for j in range(1, epg):
        upd = vb[j] > m1
        i1 = jnp.where(upd, j, i1)
        m1 = jnp.where(upd, vb[j], m1)
        a1 = jnp.where(upd, va[j], a1)
    i2 = jnp.zeros_like(best)
    m2 = jnp.full_like(m1, -jnp.inf)
    a2 = jnp.zeros_like(a1)
    for j in range(epg):
        upd = (i1 != j) & (vb[j] > m2)
        i2 = jnp.where(upd, j, i2)
        m2 = jnp.where(upd, vb[j], m2)
        a2 = jnp.where(upd, va[j], a2)

    idx_ref[0:1, :] = best * epg + i1
    idx_ref[1:2, :] = best * epg + i2
    tot = a1 + a2
    wt_ref[0:1, :] = a1 / tot
    wt_ref[1:2, :] = a2 / tot


def _router(xs, g, mod, layer, w_router_t, b_router, rows, tm, seg_args):
    d = xs.shape[1]
    e = w_router_t.shape[0]
    assert e == N_GROUPS * EXPERTS_PER_GROUP
    return pl.pallas_call(
        functools.partial(_router_kernel, seg_args=seg_args, n_experts=e),
        grid=(rows // tm,),
        in_specs=[
            pl.BlockSpec((tm, d), lambda i: (i, 0)),
            pl.BlockSpec((1, d), lambda i: (0, 0)),
            pl.BlockSpec((None, MOD_ROWS, d), lambda i: (layer, 0, 3)),
            pl.BlockSpec((None, MOD_ROWS, d), lambda i: (layer, 0, 4)),
            pl.BlockSpec((e, d), lambda i: (0, 0)),
            pl.BlockSpec((e, 1), lambda i: (0, 0)),
        ],
        out_specs=[
            pl.BlockSpec((tm, d // 2), lambda i: (i, 0)),
            pl.BlockSpec((2, tm), lambda i: (0, i)),
            pl.BlockSpec((2, tm), lambda i: (0, i)),
        ],
        out_shape=[
            jax.ShapeDtypeStruct((rows, d // 2), jnp.uint32),
            jax.ShapeDtypeStruct((2, rows), jnp.int32),
            jax.ShapeDtypeStruct((2, rows), F32),
        ],
        compiler_params=_cparams(("arbitrary",)),
        name="ffn_modulate_route",
    )(xs, g, mod, mod, w_router_t, b_router.reshape(e, 1))


def _route_kernel(idx_ref, pos_ref, te_ref, nu_ref, *, n_experts, tm, n_chunks):
    e_iota = lax.broadcasted_iota(jnp.int32, (n_experts, LANES), 0)
    idx_all = idx_ref[...]

    def count_col(k):
        col = jnp.zeros((n_experts, LANES), F32)
        for e in range(n_experts):
            col = jnp.where(e_iota == e, jnp.sum((idx_all[k] == e).astype(F32)), col)
        return col

    c0 = count_col(0)
    counts = c0 + count_col(1)
    tiles_per = jnp.floor((counts + (tm - 1)) * (1.0 / tm))
    tile_end = tiles_per
    s = 1
    while s < n_experts:
        tile_end = tile_end + jnp.where(e_iota >= s, pltpu.roll(tile_end, s, 0), 0.0)
        s *= 2
    row_off = (tile_end - tiles_per) * tm

    r = lax.broadcasted_iota(jnp.int32, (LANES, LANES), 0)
    c = lax.broadcasted_iota(jnp.int32, (LANES, LANES), 1)
    triu = (r <= c).astype(BF16)

    def body(ch, carry):
        new = []
        for k in range(2):
            onehot = (e_iota == idx_ref[k, pl.ds(ch, 1), :]).astype(F32)
            csum = jnp.dot(onehot.astype(BF16), triu, preferred_element_type=F32)
            posv = jnp.sum(onehot * (row_off + carry[k] + csum - 1.0), axis=0, keepdims=True)
            pos_ref[k, pl.ds(ch, 1), :] = posv.astype(jnp.int32)
            new.append(carry[k] + csum[:, LANES - 1:LANES])
        return tuple(new)

    lax.fori_loop(0, n_chunks, body, (jnp.zeros((n_experts, LANES), F32), c0))

    t_iota = lax.broadcasted_iota(jnp.int32, (n_experts, te_ref.shape[1]), 1).astype(F32)
    te = jnp.sum((tile_end[:, 0:1] <= t_iota).astype(F32), axis=0, keepdims=True)
    te_ref[...] = jnp.minimum(te, n_experts - 1.0).astype(jnp.int32)
    nu_ref[...] = tile_end[n_experts - 1:n_experts, :].astype(jnp.int32)


def _route(idx, n_experts, tm):
    k, rows = idx.shape
    assert k == 2 and rows % LANES == 0 and tm & (tm - 1) == 0
    n_chunks = rows // LANES
    n_tiles = (k * rows) // tm + n_experts
    te_width = -(-n_tiles // LANES) * LANES
    pos, te, nu = pl.pallas_call(
        functools.partial(_route_kernel, n_experts=n_experts, tm=tm, n_chunks=n_chunks),
        out_shape=[
            jax.ShapeDtypeStruct((k, n_chunks, LANES), jnp.int32),
            jax.ShapeDtypeStruct((1, te_width), jnp.int32),
            jax.ShapeDtypeStruct((1, LANES), jnp.int32),
        ],
        compiler_params=pltpu.CompilerParams(vmem_limit_bytes=VMEM_LIMIT),
        name="moe_route",
    )(idx.reshape(k, n_chunks, LANES))
    return pos.reshape(k * rows), te[0, :n_tiles], nu[0, :1], n_tiles


DISPATCH_ROWS = 256


def _dispatch_kernel(pos_ref, hp_ref, init_ref, hs_ref, sem, *, n_rows):
    del init_ref
    base = pl.program_id(0) * DISPATCH_ROWS

    def row_copy(k, r):
        return pltpu.make_async_copy(hp_ref.at[pl.ds(r, 1)], hs_ref.at[pl.ds(pos_ref[k * n_rows + base + r], 1)], sem)

    def start(r, carry):
        row_copy(0, r).start()
        row_copy(1, r).start()
        return carry

    def wait(r, carry):
        row_copy(0, r).wait()
        row_copy(1, r).wait()
        return carry

    lax.fori_loop(0, DISPATCH_ROWS, start, 0, unroll=8)
    lax.fori_loop(0, DISPATCH_ROWS, wait, 0, unroll=8)


def _dispatch(pos, hp, n_sorted_rows):
    rows, half = hp.shape
    return pl.pallas_call(
        functools.partial(_dispatch_kernel, n_rows=rows),
        grid_spec=pltpu.PrefetchScalarGridSpec(
            num_scalar_prefetch=1,
            grid=(rows // DISPATCH_ROWS,),
            in_specs=[
                pl.BlockSpec((DISPATCH_ROWS, half), lambda i, pos: (i, 0)),
                pl.BlockSpec(memory_space=pl.ANY),
            ],
            out_specs=pl.BlockSpec(memory_space=pl.ANY),
            scratch_shapes=[pltpu.SemaphoreType.DMA],
        ),
        out_shape=jax.ShapeDtypeStruct((n_sorted_rows, half), hp.dtype),
        input_output_aliases={2: 0},
        compiler_params=_cparams(("arbitrary",)),
        name="moe_dispatch",
    )(pos, hp, jnp.zeros((n_sorted_rows, half), hp.dtype))


def _expert_kernel(te_ref, nused_ref, x_ref, wg_ref, wu_ref, wd_ref, o_ref, xs_scr, act_scr):
    i = pl.program_id(0)

    @pl.when(i >= nused_ref[0])
    def _():
        o_ref[...] = jnp.zeros_like(o_ref)

    @pl.when(i < nused_ref[0])
    def _():
        def rows_body(r, carry):
            rs = pl.ds(pl.multiple_of(r * ROW_CHUNK, ROW_CHUNK), ROW_CHUNK)
            xs_scr[rs, :] = _unpack_bf16_pairs(x_ref[rs, :]).astype(BF16)
            return carry

        lax.fori_loop(0, x_ref.shape[0] // ROW_CHUNK, rows_body, 0)

        xs = xs_scr[...]
        for c in range(act_scr.shape[1] // PAIR_BLOCK):
            cs = slice(c * PAIR_BLOCK, (c + 1) * PAIR_BLOCK)
            gate = jnp.dot(xs, wg_ref[:, cs], preferred_element_type=F32)
            up = jnp.dot(xs, wu_ref[:, cs], preferred_element_type=F32)
            act_scr[:, cs] = (gate * jax.nn.sigmoid(gate) * up).astype(BF16)
        act = act_scr[...]
        for c in range(wd_ref.shape[1] // PAIR_BLOCK):
            y = jnp.dot(act, wd_ref[:, c * PAIR_BLOCK:(c + 1) * PAIR_BLOCK], preferred_element_type=F32)
            o_ref[:, c * LANES:(c + 1) * LANES] = _pack_bf16_pairs(y)


def _experts(tile_expert, n_used, hs, wg, wu, wd, layer, tm):
    p, half = hs.shape
    _, e, d, ff = wg.shape
    assert ff % PAIR_BLOCK == 0
    n_tiles = p // tm

    def row(i, te, nu):
        return (jnp.minimum(i, nu[0] - 1), 0)

    def wspec(shape):
        return pl.BlockSpec((None, None) + shape, lambda i, te, nu: (layer, te[i], 0, 0), pipeline_mode=pl.Buffered(1))

    return pl.pallas_call(
        _expert_kernel,
        grid_spec=pltpu.PrefetchScalarGridSpec(
            num_scalar_prefetch=2,
            grid=(n_tiles,),
            in_specs=[pl.BlockSpec((tm, half), row), wspec((d, ff)), wspec((d, ff)), wspec((ff, d))],
            out_specs=pl.BlockSpec((tm, half), lambda i, te, nu: (i, 0)),
            scratch_shapes=[pltpu.VMEM((tm, d), BF16), pltpu.VMEM((tm, ff), BF16)],
        ),
        out_shape=jax.ShapeDtypeStruct((p, half), jnp.uint32),
        compiler_params=_cparams(("arbitrary",)),
        name="moe_experts",
    )(tile_expert, n_used, hs, wg, wu, wd)


COMBINE_ROWS = 256
COMBINE_CHUNK = 8


def _combine_kernel(pos_ref, x_ref, wt_ref, gt_ref, ys_ref, o_ref, buf, sems, *, seg_args, n_rows):
    i = pl.program_id(0)
    slot = i % 2
    seg = _seg_of_block(i, *seg_args)

    def row_copy(blk, sl, k, r):
        return pltpu.make_async_copy(ys_ref.at[pl.ds(pos_ref[k * n_rows + blk * COMBINE_ROWS + r], 1)],
                                     buf.at[sl, k, pl.ds(r, 1)], sems.at[sl])

    def issue_rows(blk, sl, r0):
        for rr in range(COMBINE_CHUNK):
            row_copy(blk, sl, 0, r0 + rr).start()
            row_copy(blk, sl, 1, r0 + rr).start()

    @pl.when(i == 0)
    def _():
        def body(c, carry):
            issue_rows(0, 0, c * COMBINE_CHUNK)
            return carry

        lax.fori_loop(0, COMBINE_ROWS // COMBINE_CHUNK, body, 0)

    def wait(r, carry):
        row_copy(i, slot, 0, r).wait()
        row_copy(i, slot, 1, r).wait()
        return carry

    lax.fori_loop(0, COMBINE_ROWS, wait, 0, unroll=8)

    last = pl.num_programs(0) - 1
    nxt = jnp.minimum(i + 1, last)
    gt = gt_ref[pl.ds(seg, 1), :]

    def body(c, carry):
        r0 = pl.multiple_of(c * COMBINE_CHUNK, COMBINE_CHUNK)
        issue_rows(nxt, 1 - slot, r0)
        rs = pl.ds(r0, COMBINE_CHUNK)
        y0 = _unpack_bf16_pairs(buf[slot, 0, rs, :])
        y1 = _unpack_bf16_pairs(buf[slot, 1, rs, :])
        w = wt_ref[rs, :]
        o_ref[rs, :] = x_ref[rs, :] + gt * (w[:, 0:1] * y0 + w[:, 1:2] * y1)
        return carry

    lax.fori_loop(0, COMBINE_ROWS // COMBINE_CHUNK, body, 0)

    @pl.when(i == last)
    def _():
        def drain(r, carry):
            row_copy(last, 1 - slot, 0, r).wait()
            row_copy(last, 1 - slot, 1, r).wait()
            return carry

        lax.fori_loop(0, COMBINE_ROWS, drain, 0, unroll=8)


def _combine(pos, xs, wts_t, mod, layer, ys, rows, seg_args):
    d = xs.shape[1]
    half = d // 2
    return pl.pallas_call(
        functools.partial(_combine_kernel, seg_args=seg_args, n_rows=rows),
        grid_spec=pltpu.PrefetchScalarGridSpec(
            num_scalar_prefetch=1,
            grid=(rows // COMBINE_ROWS,),
            in_specs=[
                pl.BlockSpec((COMBINE_ROWS, d), lambda i, pos: (i, 0)),
                pl.BlockSpec((COMBINE_ROWS, 2), lambda i, pos: (i, 0)),
                pl.BlockSpec((None, MOD_ROWS, d), lambda i, pos: (layer, 0, 5)),
                pl.BlockSpec(memory_space=pl.ANY),
            ],
            out_specs=pl.BlockSpec((COMBINE_ROWS, d), lambda i, pos: (i, 0)),
            scratch_shapes=[pltpu.VMEM((2, 2, COMBINE_ROWS, half), jnp.uint32), pltpu.SemaphoreType.DMA((2,))],
        ),
        out_shape=jax.ShapeDtypeStruct((rows, d), F32),
        compiler_params=_cparams(("arbitrary",)),
        name="moe_combine",
    )(pos, xs, wts_t, mod, ys)


def kernel(x, c, ctx, c_ctx, w_ada, b_ada, g_mix, g_ffn, w_in, b_in, g_q, g_k, sink, conv_w, conv_b, g_mh,
           w_br_attn, w_br_mlstm, w_out, w_router, b_router, w_gate, w_up, w_down):
    n_batch, lat_len, d = x.shape
    ctx_len = ctx.shape[1]
    depth = w_ada.shape[0]
    d_in = w_in.shape[2]
    aw = w_br_attn.shape[1]
    mw = w_br_mlstm.shape[1]
    mqk = conv_w.shape[2] // 2
    n_mh = mw // MLSTM_V_DIM
    kvw = (d_in - aw - 2 * mqk - 2 * mw - 4 * n_mh - 2 * d) // 2
    n_experts = w_router.shape[1]
    assert n_batch + 1 <= MOD_ROWS and 2 * n_mh <= LANES

    n_lat_rows = n_batch * lat_len
    n_ctx_rows = n_batch * ctx_len
    tm = _pick(n_ctx_rows, (512, 256))
    assert lat_len % tm == 0
    seg_args = (n_lat_rows // tm, lat_len // tm, n_batch)
    tm_e = 256

    o_aq = 0
    o_ak = o_aq + aw
    o_av = o_ak + kvw
    o_mq = o_av + kvw
    o_mk = o_mq + mqk
    o_mv = o_mk + mqk
    o_mo = o_mv + mw
    o_g = o_mo + mw
    o_ga = o_g + 4 * n_mh
    o_gm = o_ga + d
    order = [(0, o_g), (o_ga, 2 * d)]
    n_ak, n_av, n_mq, n_mv, n_mo, n_ga, n_gm = o_ak, o_av, o_mq, o_mv, o_mo, o_g, o_g + d

    xs = jnp.concatenate([x.reshape(n_lat_rows, d), ctx.reshape(n_ctx_rows, d)], axis=0)
    cvec = jnp.zeros((MOD_ROWS, d), F32).at[:n_batch].set(c).at[n_batch].set(c_ctx)
    mod = _adaln(cvec, w_ada, b_ada)
    tabs = _rope_tables(lat_len)
    w_router_t = w_router.T

    w_main = _wprep(w_in, o_g, 4 * n_mh)
    b_main = jnp.concatenate([b_in[:, o:o + w] for o, w in order], axis=1).reshape(depth, 1, -1)
    wg = jnp.zeros((depth, 2, d, LANES), F32)
    bg = jnp.zeros((depth, 2, 1, LANES), F32)
    for dr in range(2):
        gsl = slice(o_g + 2 * n_mh * dr, o_g + 2 * n_mh * (dr + 1))
        wg = wg.at[:, dr, :, :2 * n_mh].set(w_in[:, :, gsl])
        bg = bg.at[:, dr, 0, :2 * n_mh].set(b_in[:, gsl])
    wg = wg.astype(BF16)
    wa_b, wm_b, wo_b = w_br_attn.astype(BF16), w_br_mlstm.astype(BF16), w_out.astype(BF16)
    wgate_b, wup_b, wdown_b = w_gate.astype(BF16), w_up.astype(BF16), w_down.astype(BF16)

    for l in range(depth):
        need_ctx = l < depth - 1
        rows = n_lat_rows + (n_ctx_rows if need_ctx else 0)

        proj, gates = _inproj(xs, g_mix[l].reshape(1, d), mod, l, w_main, b_main, wg, bg, tm, seg_args)
        qk = _conv(proj, conv_w[l], conv_b[l], n_mq, n_lat_rows, lat_len, ctx_len)
        hfb = _mlstm(qk, proj, gates, mqk, mw, n_mv, n_batch, lat_len, ctx_len)
        qr, kr = _rope(proj, tabs, g_q[l], g_k[l], aw, kvw, n_ak, n_lat_rows)
        attn = _attention(sink[l], qr, kr, proj, aw, kvw, n_av, n_batch, lat_len, ctx_len, need_ctx)
        u = _branch(attn, hfb, proj, g_mh[l], wa_b, wm_b, l, n_mo, n_ga, n_gm, rows, tm)
        xs = _outproj(u, wo_b, xs, mod, l, tm, seg_args)

        hp, idx, wts = _router(xs, g_ffn[l].reshape(1, d), mod, l, w_router_t, b_router, rows, ROUTER_ROWS,
                               (n_lat_rows // ROUTER_ROWS, lat_len // ROUTER_ROWS, n_batch))
        pos, tile_expert, n_used, n_tiles = _route(idx, n_experts, tm_e)
        hs = _dispatch(pos, hp, n_tiles * tm_e)
        ys = _experts(tile_expert, n_used, hs, wgate_b, wup_b, wdown_b, l, tm_e)
        xs = _combine(pos, xs, wts.T, mod, l, ys, rows, (n_lat_rows // COMBINE_ROWS, lat_len // COMBINE_ROWS, n_batch))

    return xs[:n_lat_rows].reshape(n_batch, lat_len, d)
```

```python
import functools
import math

import jax
import jax.numpy as jnp
from jax import lax
from jax.experimental import pallas as pl
from jax.experimental.pallas import tpu as pltpu

GRID_W = 64
HEAD_DIM = 128
WINDOW = 128
QBLK = 128
ROPE_THETA = 10000.0
ROPE_PAIRS = HEAD_DIM // 4
ATTN_SCALE = HEAD_DIM ** -0.5
MLSTM_QK_DIM = 128
MLSTM_V_DIM = 256
MLSTM_CHUNK = 128
N_GROUPS = 4
EXPERTS_PER_GROUP = 4
TOP_K = 2
EPS = 1e-6
NEG = -1e30

LANES = 128
MOD_ROWS = 8
ROW_CHUNK = 64
VMEM_LIMIT = 56 << 20

F32 = jnp.float32
BF16 = jnp.bfloat16


def _pick(n, cands):
    for c in cands:
        if n % c == 0:
            return c
    raise ValueError(f"no tile in {cands} divides {n}")


def _cparams(sem, vmem=VMEM_LIMIT):
    return pltpu.CompilerParams(dimension_semantics=sem, vmem_limit_bytes=vmem)


def _seg_of_block(i, n_lat_blocks, blocks_per_batch, n_batch):
    return jnp.where(i < n_lat_blocks, i // blocks_per_batch, n_batch)


def _modulated(x, g, sc, sh):
    ms = jnp.mean(x * x, axis=-1, keepdims=True)
    y = x * lax.rsqrt(ms + EPS) * g
    return y * (1.0 + sc) + sh


def _adaln_kernel(c_ref, w_ref, b_ref, o_ref):
    c = c_ref[...]
    cs = (c * jax.nn.sigmoid(c)).astype(BF16)
    o_ref[...] = jnp.dot(cs, w_ref[...].astype(BF16), preferred_element_type=F32) + b_ref[...]


def _adaln(cvec, w_ada, b_ada):
    depth, d, n6 = w_ada.shape
    tn = _pick(n6, (512, 256, 128))
    return pl.pallas_call(
        _adaln_kernel,
        grid=(depth, n6 // tn),
        in_specs=[
            pl.BlockSpec((MOD_ROWS, d), lambda l, j: (0, 0)),
            pl.BlockSpec((None, d, tn), lambda l, j: (l, 0, j)),
            pl.BlockSpec((None, 1, tn), lambda l, j: (l, 0, j)),
        ],
        out_specs=pl.BlockSpec((None, MOD_ROWS, tn), lambda l, j: (l, 0, j)),
        out_shape=jax.ShapeDtypeStruct((depth, MOD_ROWS, n6), F32),
        compiler_params=_cparams(("arbitrary", "arbitrary")),
        name="adaln",
    )(cvec, w_ada, b_ada.reshape(depth, 1, n6))


def _wprep_kernel(a_ref, b_ref, o_ref, *, first_shifted, shift):
    j = pl.program_id(2)

    @pl.when(j < first_shifted)
    def _():
        o_ref[...] = a_ref[...].T.astype(o_ref.dtype)

    @pl.when(j >= first_shifted)
    def _():
        rows = jnp.concatenate([a_ref[shift:, :], b_ref[:shift, :]], axis=0)
        o_ref[...] = rows.T.astype(o_ref.dtype)


def _wprep(w_in, cut_start, cut_width):
    depth, d, d_in = w_in.shape
    nc = d_in - cut_width
    tw = _pick(math.gcd(cut_start, nc), (512, 256, 128))
    tk = _pick(d, (1024, 512, 256, 128))
    assert cut_width % 8 == 0 and cut_width <= LANES and tw % LANES == 0
    w_t = jnp.swapaxes(w_in, 1, 2)
    nxt = tw // LANES
    return pl.pallas_call(
        functools.partial(_wprep_kernel, first_shifted=cut_start // tw, shift=cut_width),
        grid=(depth, d // tk, nc // tw),
        in_specs=[
            pl.BlockSpec((None, tw, tk), lambda l, i, j: (l, j, i)),
            pl.BlockSpec((None, LANES, tk), lambda l, i, j: (l, (j + 1) * nxt, i)),
        ],
        out_specs=pl.BlockSpec((None, tk, tw), lambda l, i, j: (l, i, j)),
        out_shape=jax.ShapeDtypeStruct((depth, d, nc), BF16),
        compiler_params=_cparams(("arbitrary", "arbitrary", "arbitrary")),
        name="w_in_relayout",
    )(w_t, w_t)


INPROJ_TILES = (1024, 512, 256, 128)


def _inproj_kernel(x_ref, g_ref, sh_ref, sc_ref, w_ref, b_ref, wg_ref, bg_ref, o_ref, og_ref, h_scr, *, seg_args):
    i = pl.program_id(0)
    j = pl.program_id(1)

    @pl.when(j == 0)
    def _():
        seg = _seg_of_block(i, *seg_args)
        g = g_ref[...]
        sc = sc_ref[pl.ds(seg, 1), :]
        sh = sh_ref[pl.ds(seg, 1), :]

        def rows_body(r, carry):
            rs = pl.ds(pl.multiple_of(r * ROW_CHUNK, ROW_CHUNK), ROW_CHUNK)
            h_scr[rs, :] = _modulated(x_ref[rs, :], g, sc, sh).astype(BF16)
            return carry

        lax.fori_loop(0, x_ref.shape[0] // ROW_CHUNK, rows_body, 0)
        for d in range(2):
            og_ref[d] = jnp.dot(h_scr[...], wg_ref[d], preferred_element_type=F32) + bg_ref[d]

    o_ref[...] = (jnp.dot(h_scr[...], w_ref[...], preferred_element_type=F32) + b_ref[...]).astype(o_ref.dtype)


def _inproj(xs, g, mod, layer, w_main, b_main, w_gates, b_gates, tm, seg_args):
    t, d = xs.shape
    nc = w_main.shape[2]
    tn = _pick(nc, INPROJ_TILES)
    return pl.pallas_call(
        functools.partial(_inproj_kernel, seg_args=seg_args),
        grid=(t // tm, nc // tn),
        in_specs=[
            pl.BlockSpec((tm, d), lambda i, j: (i, 0)),
            pl.BlockSpec((1, d), lambda i, j: (0, 0)),
            pl.BlockSpec((None, MOD_ROWS, d), lambda i, j: (layer, 0, 0)),
            pl.BlockSpec((None, MOD_ROWS, d), lambda i, j: (layer, 0, 1)),
            pl.BlockSpec((None, d, tn), lambda i, j: (layer, 0, j)),
            pl.BlockSpec((None, 1, tn), lambda i, j: (layer, 0, j)),
            pl.BlockSpec((None, 2, d, LANES), lambda i, j: (layer, 0, 0, 0)),
            pl.BlockSpec((None, 2, 1, LANES), lambda i, j: (layer, 0, 0, 0)),
        ],
        out_specs=[
            pl.BlockSpec((tm, tn), lambda i, j: (i, j)),
            pl.BlockSpec((2, tm, LANES), lambda i, j: (0, i, 0)),
        ],
        out_shape=[
            jax.ShapeDtypeStruct((t, nc), BF16),
            jax.ShapeDtypeStruct((2, t, LANES), F32),
        ],
        scratch_shapes=[pltpu.VMEM((tm, d), BF16)],
        compiler_params=_cparams(("arbitrary", "arbitrary")),
        name="inproj",
    )(xs, g, mod, mod, w_main, b_main, w_gates, b_gates)


CONV_ROWS = 256
HALO_ROWS = 16


def _conv_kernel(cur_ref, prev_ref, next_ref, w_ref, b_ref, o_ref, *, n_lat_rows, lat_len, ctx_len, k_col_block):
    i = pl.program_id(0)
    j = pl.program_id(1)
    row0 = i * CONV_ROWS
    in_lat = row0 < n_lat_rows
    seg_len = jnp.where(in_lat, lat_len, ctx_len)
    off = jnp.where(in_lat, row0, row0 - n_lat_rows) % seg_len
    has_prev = (off != 0).astype(F32)
    has_next = (off + CONV_ROWS != seg_len).astype(F32)

    x = cur_ref[...].astype(F32)
    prev_row = prev_ref[HALO_ROWS - 1:HALO_ROWS, :].astype(F32) * has_prev
    next_row = next_ref[0:1, :].astype(F32) * has_next
    rows = lax.broadcasted_iota(jnp.int32, x.shape, 0)
    xm1 = jnp.where(rows == 0, prev_row, pltpu.roll(x, 1, 0))
    xp1 = jnp.where(rows == CONV_ROWS - 1, next_row, pltpu.roll(x, CONV_ROWS - 1, 0))
    w = w_ref[...]
    y = w[0:1, :] * xm1 + w[1:2, :] * x + w[2:3, :] * xp1 + b_ref[...]
    y = y * jax.nn.sigmoid(y)
    scale = jnp.where(j >= k_col_block, MLSTM_QK_DIM ** -0.5, 1.0).astype(F32)
    o_ref[...] = (y * scale).astype(o_ref.dtype)


def _conv(proj, conv_w, conv_b, qk_off, n_lat_rows, lat_len, ctx_len):
    t = proj.shape[0]
    width = conv_w.shape[1]
    tc = _pick(width // 2, (1024, 512, 256, 128))
    assert qk_off % tc == 0 and lat_len % CONV_ROWS == 0 and ctx_len % CONV_ROWS == 0
    cb = qk_off // tc
    halo_per_blk = CONV_ROWS // HALO_ROWS
    n_halo = t // HALO_ROWS
    return pl.pallas_call(
        functools.partial(_conv_kernel, n_lat_rows=n_lat_rows, lat_len=lat_len, ctx_len=ctx_len,
                          k_col_block=(width // 2) // tc),
        grid=(t // CONV_ROWS, width // tc),
        in_specs=[
            pl.BlockSpec((CONV_ROWS, tc), lambda i, j: (i, cb + j)),
            pl.BlockSpec((HALO_ROWS, tc), lambda i, j: (jnp.maximum(i * halo_per_blk - 1, 0), cb + j)),
            pl.BlockSpec((HALO_ROWS, tc), lambda i, j: (jnp.minimum((i + 1) * halo_per_blk, n_halo - 1), cb + j)),
            pl.BlockSpec((3, tc), lambda i, j: (0, j)),
            pl.BlockSpec((1, tc), lambda i, j: (0, j)),
        ],
        out_specs=pl.BlockSpec((CONV_ROWS, tc), lambda i, j: (i, j)),
        out_shape=jax.ShapeDtypeStruct((t, width), BF16),
        compiler_params=_cparams(("arbitrary", "arbitrary")),
        name="qk_conv",
    )(proj, proj, proj, conv_w, conv_b.reshape(1, width))


ROPE_ROWS = 512
ROPE_HEAD_GROUP = 4


def _rope_kernel(q_ref, k_ref, cos_ref, s1_ref, s2_ref, gq_ref, gk_ref, qo_ref, ko_ref, *, n_q_heads, n_k_heads):
    cos = cos_ref[...]
    s1 = s1_ref[...]
    s2 = s2_ref[...]

    def prep_heads(src_ref, dst_ref, g, n_heads, scale):
        for h0 in range(0, n_heads, ROPE_HEAD_GROUP):
            sls = [slice(h * HEAD_DIM, (h + 1) * HEAD_DIM) for h in range(h0, min(h0 + ROPE_HEAD_GROUP, n_heads))]
            xs = [src_ref[:, sl].astype(F32) for sl in sls]
            inv = [lax.rsqrt(jnp.mean(x * x, axis=-1, keepdims=True) + EPS) for x in xs]
            xn = [x * r * g for x, r in zip(xs, inv)]
            up = [pltpu.roll(x, HEAD_DIM - ROPE_PAIRS, 1) for x in xn]
            dn = [pltpu.roll(x, ROPE_PAIRS, 1) for x in xn]
            for sl, x, u, dwn in zip(sls, xn, up, dn):
                y = x * cos + u * s1 + dwn * s2
                dst_ref[:, sl] = (y * scale if scale != 1.0 else y).astype(dst_ref.dtype)

    prep_heads(q_ref, qo_ref, gq_ref[...], n_q_heads, ATTN_SCALE)
    prep_heads(k_ref, ko_ref, gk_ref[...], n_k_heads, 1.0)


def _rope(proj, tabs, g_q, g_k, aw, kvw, k_off, n_lat_rows):
    t = proj.shape[0]
    assert k_off % kvw == 0
    kb = k_off // kvw
    n_lat_blk = n_lat_rows // ROPE_ROWS
    lat_blk_per_batch = (tabs[0].shape[0] - ROPE_ROWS) // ROPE_ROWS

    def tab_map(i):
        return (jnp.where(i < n_lat_blk, i % lat_blk_per_batch, lat_blk_per_batch), 0)

    tab_spec = pl.BlockSpec((ROPE_ROWS, HEAD_DIM), tab_map)
    return pl.pallas_call(
        functools.partial(_rope_kernel, n_q_heads=aw // HEAD_DIM, n_k_heads=kvw // HEAD_DIM),
        grid=(t // ROPE_ROWS,),
        in_specs=[
            pl.BlockSpec((ROPE_ROWS, aw), lambda i: (i, 0)),
            pl.BlockSpec((ROPE_ROWS, kvw), lambda i: (i, kb)),
            tab_spec, tab_spec, tab_spec,
            pl.BlockSpec((1, HEAD_DIM), lambda i: (0, 0)),
            pl.BlockSpec((1, HEAD_DIM), lambda i: (0, 0)),
        ],
        out_specs=[
            pl.BlockSpec((ROPE_ROWS, aw), lambda i: (i, 0)),
            pl.BlockSpec((ROPE_ROWS, kvw), lambda i: (i, 0)),
        ],
        out_shape=[jax.ShapeDtypeStruct((t, aw), BF16), jax.ShapeDtypeStruct((t, kvw), BF16)],
        compiler_params=_cparams(("arbitrary",)),
        name="qk_norm_rope",
    )(proj, proj, tabs[0], tabs[1], tabs[2], g_q.reshape(1, HEAD_DIM), g_k.reshape(1, HEAD_DIM))


def _rope_tables(n_lat):
    rows = n_lat // GRID_W
    inv_freq = ROPE_THETA ** (-jnp.arange(ROPE_PAIRS, dtype=F32) / ROPE_PAIRS)
    row_pos = jnp.repeat(jnp.arange(rows, dtype=F32), GRID_W)
    col_pos = jnp.tile(jnp.arange(GRID_W, dtype=F32), rows)
    ang_r = row_pos[:, None] * inv_freq
    ang_c = col_pos[:, None] * inv_freq
    zeros = jnp.zeros_like(ang_r)
    cos = jnp.concatenate([jnp.cos(ang_r), jnp.cos(ang_r), jnp.cos(ang_c), jnp.cos(ang_c)], axis=-1)
    s1 = jnp.concatenate([-jnp.sin(ang_r), zeros, -jnp.sin(ang_c), zeros], axis=-1)
    s2 = jnp.concatenate([zeros, jnp.sin(ang_r), zeros, jnp.sin(ang_c)], axis=-1)
    ident = jnp.ones((ROPE_ROWS, HEAD_DIM), F32)
    zpad = jnp.zeros((ROPE_ROWS, HEAD_DIM), F32)
    return (jnp.concatenate([cos, ident], 0), jnp.concatenate([s1, zpad], 0), jnp.concatenate([s2, zpad], 0))


def _attn_kernel(sink_ref, q_ref, kp_ref, kc_ref, kn_ref, kx_ref, vp_ref, vc_ref, vn_ref, vx_ref, o_ref,
                 *, n_lat_blk, n_kv, group, ctx_len):
    n = pl.program_id(1)
    is_ctx = n >= n_lat_blk
    n_band = 3 * QBLK
    n_keys = n_band + ctx_len
    qi = lax.broadcasted_iota(jnp.int32, (QBLK, n_keys), 0)
    kj = lax.broadcasted_iota(jnp.int32, (QBLK, n_keys), 1)
    rel = kj - QBLK - qi
    kpos = n * QBLK + kj - QBLK
    band_ok = (jnp.abs(rel) <= WINDOW) & (kpos >= 0) & (kpos < n_lat_blk * QBLK) & jnp.logical_not(is_ctx)
    valid = band_ok | (kj >= n_band)

    for hk in range(n_kv):
        ksl = slice(hk * HEAD_DIM, (hk + 1) * HEAD_DIM)
        k_all = jnp.concatenate([kp_ref[:, ksl], kc_ref[:, ksl], kn_ref[:, ksl], kx_ref[:, ksl]], axis=0)
        v_all = jnp.concatenate([vp_ref[:, ksl], vc_ref[:, ksl], vn_ref[:, ksl], vx_ref[:, ksl]], axis=0)
        heads = [hk * group + g for g in range(group)]
        qsl = {h: slice(h * HEAD_DIM, (h + 1) * HEAD_DIM) for h in heads}
        s = {h: jnp.where(valid, lax.dot_general(q_ref[:, qsl[h]], k_all, (((1,), (1,)), ((), ())),
                                                 preferred_element_type=F32), NEG) for h in heads}
        m = {h: jnp.maximum(jnp.max(s[h], axis=-1, keepdims=True), sink_ref[h]) for h in heads}
        p = {h: jnp.exp(s[h] - m[h]) for h in heads}
        denom = {h: jnp.sum(p[h], axis=-1, keepdims=True) + jnp.exp(sink_ref[h] - m[h]) for h in heads}
        o = {h: jnp.dot(p[h].astype(BF16), v_all, preferred_element_type=F32) for h in heads}
        for h in heads:
            o_ref[:, qsl[h]] = (o[h] / denom[h]).astype(o_ref.dtype)


def _attention(sink, qr, kr, proj, aw, kvw, v_off, n_batch, lat_len, ctx_len, with_ctx):
    t = proj.shape[0]
    assert v_off % kvw == 0
    vb = v_off // kvw
    n_lat_blk = lat_len // QBLK
    n_ctx_blk = ctx_len // QBLK
    n_lat_rows = n_batch * lat_len
    nblk = n_lat_blk + (n_ctx_blk if with_ctx else 0)

    def qrow(b, n):
        return jnp.where(n < n_lat_blk, b * n_lat_blk + n, n_lat_rows // QBLK + b * n_ctx_blk + (n - n_lat_blk))

    def band(delta):
        def f(b, n):
            nn = jnp.clip(jnp.minimum(n, n_lat_blk - 1) + delta, 0, n_lat_blk - 1)
            return b * n_lat_blk + nn
        return f

    def ctx_row(b, n):
        return n_lat_rows // ctx_len + b

    def kspec(rowf):
        return pl.BlockSpec((QBLK, kvw), lambda b, n: (rowf(b, n), 0))

    def vspec(rowf):
        return pl.BlockSpec((QBLK, kvw), lambda b, n: (rowf(b, n), vb))

    return pl.pallas_call(
        functools.partial(_attn_kernel, n_lat_blk=n_lat_blk, n_kv=kvw // HEAD_DIM,
                          group=aw // kvw, ctx_len=ctx_len),
        grid=(n_batch, nblk),
        in_specs=[
            pl.BlockSpec(memory_space=pltpu.SMEM),
            pl.BlockSpec((QBLK, aw), lambda b, n: (qrow(b, n), 0)),
            kspec(band(-1)), kspec(band(0)), kspec(band(1)),
            pl.BlockSpec((ctx_len, kvw), lambda b, n: (ctx_row(b, n), 0)),
            vspec(band(-1)), vspec(band(0)), vspec(band(1)),
            pl.BlockSpec((ctx_len, kvw), lambda b, n: (ctx_row(b, n), vb)),
        ],
        out_specs=pl.BlockSpec((QBLK, aw), lambda b, n: (qrow(b, n), 0)),
        out_shape=jax.ShapeDtypeStruct((n_lat_rows + (n_batch * ctx_len if with_ctx else 0), aw), BF16),
        compiler_params=_cparams(("arbitrary", "arbitrary")),
        name="attention",
    )(sink, qr, kr, kr, kr, kr, proj, proj, proj, proj)


MLSTM_GROUP = 8


def _mlstm_kernel(*refs, n_heads, n_vblk):
    per_dir = 3 + n_vblk
    dir_ins = [refs[:per_dir], refs[per_dir:2 * per_dir]]
    outs, state = refs[2 * per_dir:2 * per_dir + 2], refs[2 * per_dir + 2:]
    heads_per_vblk = n_heads // n_vblk
    n_scans = 2 * n_heads
    c_scrs, nm_scrs = state[:n_scans], state[n_scans:]
    c = pl.program_id(1)
    L = MLSTM_CHUNK
    dk = MLSTM_QK_DIM
    dv = MLSTM_V_DIM

    @pl.when(c == 0)
    def _():
        for scr in state:
            scr[...] = jnp.zeros_like(scr)

    r = lax.broadcasted_iota(jnp.int32, (L, L), 0)
    s = lax.broadcasted_iota(jnp.int32, (L, L), 1)

    def split_dot_l(mat_b, x):
        hi = x.astype(BF16)
        lo = (x - hi.astype(F32)).astype(BF16)
        return jnp.dot(mat_b, hi, preferred_element_type=F32) + jnp.dot(mat_b, lo, preferred_element_type=F32)

    def split_dot_r(x, mat_b):
        hi = x.astype(BF16)
        lo = (x - hi.astype(F32)).astype(BF16)
        return jnp.dot(hi, mat_b, preferred_element_type=F32) + jnp.dot(lo, mat_b, preferred_element_type=F32)

    tris, gate_cols, gate_rows, cum_cols, cum_rows, end_cols = [], [], [], [], [], []
    for dirn in range(2):
        tri = (s <= r) if dirn == 0 else (s >= r)
        tri_t = (r <= s) if dirn == 0 else (r >= s)
        gates = dir_ins[dirn][2][...]
        logf = jnp.minimum(gates, 0.0) - jnp.log1p(jnp.exp(-jnp.abs(gates)))
        cum_col = split_dot_l(tri.astype(BF16), logf)
        cum_row = split_dot_r(logf.T, tri_t.astype(BF16))
        tris.append(tri)
        gate_cols.append(gates)
        gate_rows.append(gates.T)
        cum_cols.append(cum_col)
        cum_rows.append(cum_row)
        end_cols.append(cum_col[L - 1:L, :] if dirn == 0 else cum_col[0:1, :])

    nt = (((1,), (1,)), ((), ()))
    scans = [(dirn, h) for dirn in range(2) for h in range(n_heads)]
    for g0 in range(0, n_scans, MLSTM_GROUP):
        hs = scans[g0:g0 + MLSTM_GROUP]
        sid = {x: x[0] * n_heads + x[1] for x in hs}
        qs = {x: dir_ins[x[0]][0][:, x[1] * dk:(x[1] + 1) * dk] for x in hs}
        ks = {x: dir_ins[x[0]][1][:, x[1] * dk:(x[1] + 1) * dk] for x in hs}
        vs = {x: dir_ins[x[0]][3 + x[1] // heads_per_vblk][:, (x[1] % heads_per_vblk) * dv:
                                                           (x[1] % heads_per_vblk + 1) * dv] for x in hs}
        b_col = {x: cum_cols[x[0]][:, n_heads + x[1]:n_heads + x[1] + 1] for x in hs}
        b_row = {x: cum_rows[x[0]][n_heads + x[1]:n_heads + x[1] + 1, :] for x in hs}
        i_col = {x: gate_cols[x[0]][:, x[1]:x[1] + 1] for x in hs}
        i_row = {x: gate_rows[x[0]][x[1]:x[1] + 1, :] for x in hs}
        b_end = {x: end_cols[x[0]][:, n_heads + x[1]:n_heads + x[1] + 1] for x in hs}
        n_prev = {x: nm_scrs[sid[x]][0:1, :] for x in hs}
        m_prev = {x: nm_scrs[sid[x]][1:2, 0:1] for x in hs}
        ct_prev = {x: c_scrs[sid[x]][...] for x in hs}

        dmat = {h: jnp.where(tris[h[0]], b_col[h] - b_row[h] + i_row[h], NEG) for h in hs}
        m_inter = {h: b_col[h] + m_prev[h] for h in hs}
        m_t = {h: jnp.maximum(m_inter[h], jnp.max(dmat[h], axis=-1, keepdims=True)) for h in hs}
        qk = {h: lax.dot_general(qs[h], ks[h], nt, preferred_element_type=F32) for h in hs}
        qc = {h: jnp.dot(qs[h], ct_prev[h].astype(BF16), preferred_element_type=F32) for h in hs}
        qn_prev = {h: jnp.sum(qs[h].astype(F32) * n_prev[h], axis=-1, keepdims=True) for h in hs}
        a = {h: jnp.exp(m_inter[h] - m_t[h]) for h in hs}
        smat = {h: qk[h] * jnp.exp(dmat[h] - m_t[h]) for h in hs}
        sv = {h: jnp.dot(smat[h].astype(BF16), vs[h], preferred_element_type=F32) for h in hs}
        qn = {h: jnp.sum(smat[h], axis=-1, keepdims=True) + a[h] * qn_prev[h] for h in hs}

        g_row = {h: b_end[h] - b_row[h] + i_row[h] for h in hs}
        m_new = {h: jnp.maximum(b_end[h] + m_prev[h], jnp.max(g_row[h], axis=-1, keepdims=True)) for h in hs}
        kw = {h: ks[h].astype(F32) * jnp.exp(b_end[h] - b_col[h] + i_col[h] - m_new[h]) for h in hs}
        a_end = {h: jnp.exp(b_end[h] + m_prev[h] - m_new[h]) for h in hs}
        kv = {h: jnp.dot(kw[h].T.astype(BF16), vs[h], preferred_element_type=F32) for h in hs}

        for h in hs:
            hout = (sv[h] + a[h] * qc[h]) / jnp.maximum(jnp.abs(qn[h]), jnp.exp(-m_t[h]))
            outs[h[0]][:, h[1] * dv:(h[1] + 1) * dv] = hout.astype(outs[h[0]].dtype)
        for h in hs:
            c_scrs[sid[h]][...] = a_end[h] * ct_prev[h] + kv[h]
            nm_scrs[sid[h]][0:1, :] = a_end[h] * n_prev[h] + jnp.sum(kw[h], axis=0, keepdims=True)
            nm_scrs[sid[h]][1:2, :] = jnp.broadcast_to(m_new[h], (1, dk))


def _mlstm(qk, proj, gates, mqk, mw, v_off, n_batch, lat_len, ctx_len):
    t = proj.shape[0]
    L = MLSTM_CHUNK
    n_heads = mw // MLSTM_V_DIM
    assert mqk == n_heads * MLSTM_QK_DIM
    vw = math.gcd(mw, v_off)
    n_vblk = mw // vw
    assert vw % MLSTM_V_DIM == 0
    n_ctx = ctx_len // L
    n_lat = lat_len // L
    lat_blocks = n_batch * n_lat

    def row(d, b, c):
        cc = c if d == 0 else n_ctx - 1 - c
        lc = c - n_ctx if d == 0 else n_lat - 1 - (c - n_ctx)
        return jnp.where(c < n_ctx, lat_blocks + b * n_ctx + cc, b * n_lat + lc)

    def dir_specs(d):
        return [
            pl.BlockSpec((L, mqk), lambda b, c: (row(d, b, c), 0)),
            pl.BlockSpec((L, mqk), lambda b, c: (row(d, b, c), 1)),
            pl.BlockSpec((None, L, LANES), lambda b, c: (d, row(d, b, c), 0)),
        ] + [pl.BlockSpec((L, vw), lambda b, c, j=j: (row(d, b, c), v_off // vw + j)) for j in range(n_vblk)]

    dir_args = [qk, qk, gates] + [proj] * n_vblk
    return pl.pallas_call(
        functools.partial(_mlstm_kernel, n_heads=n_heads, n_vblk=n_vblk),
        grid=(n_batch, n_ctx + n_lat),
        in_specs=dir_specs(0) + dir_specs(1),
        out_specs=[pl.BlockSpec((L, mw), lambda b, c: (row(0, b, c), 0)),
                   pl.BlockSpec((L, mw), lambda b, c: (row(1, b, c), 0))],
        out_shape=[jax.ShapeDtypeStruct((t, mw), BF16)] * 2,
        scratch_shapes=([pltpu.VMEM((MLSTM_QK_DIM, MLSTM_V_DIM), F32)] * (2 * n_heads)
                        + [pltpu.VMEM((8, MLSTM_QK_DIM), F32)] * (2 * n_heads)),
        compiler_params=_cparams(("arbitrary", "arbitrary")),
        name="mlstm",
    )(*dir_args, *dir_args)


def _branch_kernel(attn_ref, hf_ref, hb_ref, gmh_ref, wa_ref, wm_ref, ga_ref, gm_ref, *rest, n_heads):
    mo_refs, o_ref, hm_scr = rest[:-2], rest[-2], rest[-1]
    heads_per_blk = n_heads // len(mo_refs)
    j = pl.program_id(1)

    @pl.when(j == 0)
    def _():
        dv = MLSTM_V_DIM
        for h in range(n_heads):
            sl = slice(h * dv, (h + 1) * dv)
            mo = mo_refs[h // heads_per_blk][:, (h % heads_per_blk) * dv:(h % heads_per_blk + 1) * dv]
            hsum = hf_ref[:, sl].astype(F32) + hb_ref[:, sl].astype(F32)
            x = jax.nn.sigmoid(mo.astype(F32)) * hsum
            y = x * lax.rsqrt(jnp.mean(x * x, axis=-1, keepdims=True) + EPS) * gmh_ref[:, sl]
            hm_scr[:, sl] = y.astype(BF16)

    ya = jnp.dot(attn_ref[...], wa_ref[...], preferred_element_type=F32)
    ym = jnp.dot(hm_scr[...], wm_ref[...], preferred_element_type=F32)
    u = jax.nn.sigmoid(ga_ref[...].astype(F32)) * ya + jax.nn.sigmoid(gm_ref[...].astype(F32)) * ym
    o_ref[...] = u.astype(o_ref.dtype)


def _branch(attn, hfb, proj, g_mh, wa, wm, layer, mo_off, ga_off, gm_off, rows, tm):
    aw = attn.shape[1]
    mw = hfb[0].shape[1]
    d = wa.shape[2]
    tn = _pick(d, (1024, 512, 256, 128))
    assert ga_off % tn == 0 and gm_off % tn == 0
    gab, gmb = ga_off // tn, gm_off // tn
    mo_w = math.gcd(mw, mo_off)
    n_mo_blk = mw // mo_w
    assert mo_w % MLSTM_V_DIM == 0
    return pl.pallas_call(
        functools.partial(_branch_kernel, n_heads=mw // MLSTM_V_DIM),
        grid=(rows // tm, d // tn),
        in_specs=[
            pl.BlockSpec((tm, aw), lambda i, j: (i, 0)),
            pl.BlockSpec((tm, mw), lambda i, j: (i, 0)),
            pl.BlockSpec((tm, mw), lambda i, j: (i, 0)),
            pl.BlockSpec((1, mw), lambda i, j: (0, 0)),
            pl.BlockSpec((None, aw, tn), lambda i, j: (layer, 0, j)),
            pl.BlockSpec((None, mw, tn), lambda i, j: (layer, 0, j)),
            pl.BlockSpec((tm, tn), lambda i, j: (i, gab + j)),
            pl.BlockSpec((tm, tn), lambda i, j: (i, gmb + j)),
        ] + [pl.BlockSpec((tm, mo_w), lambda i, j, b=b: (i, mo_off // mo_w + b)) for b in range(n_mo_blk)],
        out_specs=pl.BlockSpec((tm, tn), lambda i, j: (i, j)),
        out_shape=jax.ShapeDtypeStruct((rows, d), BF16),
        scratch_shapes=[pltpu.VMEM((tm, mw), BF16)],
        compiler_params=_cparams(("arbitrary", "arbitrary")),
        name="branch_merge",
    )(attn, hfb[0], hfb[1], g_mh.reshape(1, mw), wa, wm, proj, proj, *([proj] * n_mo_blk))


def _outproj_kernel(u_ref, w_ref, x_ref, gt_ref, o_ref, *, seg_args):
    seg = _seg_of_block(pl.program_id(0), *seg_args)
    y = jnp.dot(u_ref[...], w_ref[...], preferred_element_type=F32)
    o_ref[...] = x_ref[...] + gt_ref[pl.ds(seg, 1), :] * y


def _outproj(u, w_out, xs, mod, layer, tm, seg_args):
    rows, d = u.shape
    tn = _pick(d, (1024, 512, 256, 128))
    gate_blk = 2 * (d // tn)
    return pl.pallas_call(
        functools.partial(_outproj_kernel, seg_args=seg_args),
        grid=(rows // tm, d // tn),
        in_specs=[
            pl.BlockSpec((tm, d), lambda i, j: (i, 0)),
            pl.BlockSpec((None, d, tn), lambda i, j: (layer, 0, j)),
            pl.BlockSpec((tm, tn), lambda i, j: (i, j)),
            pl.BlockSpec((None, MOD_ROWS, tn), lambda i, j: (layer, 0, gate_blk + j)),
        ],
        out_specs=pl.BlockSpec((tm, tn), lambda i, j: (i, j)),
        out_shape=jax.ShapeDtypeStruct(xs.shape, F32),
        input_output_aliases={2: 0},
        compiler_params=_cparams(("arbitrary", "arbitrary")),
        name="outproj_residual",
    )(u, w_out, xs, mod)


PAIR_BLOCK = 2 * LANES


def _pack_bf16_pairs(h):
    blocks = []
    for b in range(h.shape[1] // PAIR_BLOCK):
        hi = pltpu.bitcast(h[:, b * PAIR_BLOCK:b * PAIR_BLOCK + LANES].astype(BF16).astype(F32), jnp.uint32)
        lo = pltpu.bitcast(h[:, b * PAIR_BLOCK + LANES:(b + 1) * PAIR_BLOCK].astype(BF16).astype(F32), jnp.uint32)
        blocks.append(hi | (lo >> 16))
    return blocks[0] if len(blocks) == 1 else jnp.concatenate(blocks, axis=1)


def _unpack_bf16_pairs(p):
    blocks = []
    for b in range(p.shape[1] // LANES):
        w = p[:, b * LANES:(b + 1) * LANES]
        blocks.append(pltpu.bitcast(w & jnp.uint32(0xFFFF0000), F32))
        blocks.append(pltpu.bitcast(w << 16, F32))
    return jnp.concatenate(blocks, axis=1)


ROUTER_ROWS = 256


def _router_kernel(x_ref, g_ref, sh_ref, sc_ref, wr_ref, br_ref, hp_ref, idx_ref, wt_ref, *, seg_args, n_experts):
    seg = _seg_of_block(pl.program_id(0), *seg_args)
    h = _modulated(x_ref[...], g_ref[...], sc_ref[pl.ds(seg, 1), :], sh_ref[pl.ds(seg, 1), :])
    hp_ref[...] = _pack_bf16_pairs(h)

    wr = wr_ref[...]
    h_hi = h.astype(BF16)
    h_lo = (h - h_hi.astype(F32)).astype(BF16)
    w_hi = wr.astype(BF16)
    w_lo = (wr - w_hi.astype(F32)).astype(BF16)
    nt = (((1,), (1,)), ((), ()))
    logits = (lax.dot_general(w_hi, h_hi, nt, preferred_element_type=F32)
              + lax.dot_general(w_hi, h_lo, nt, preferred_element_type=F32)
              + lax.dot_general(w_lo, h_hi, nt, preferred_element_type=F32))
    aff = jax.nn.sigmoid(logits)
    biased = aff + br_ref[...]
    rb = [biased[e:e + 1, :] for e in range(n_experts)]
    ra = [aff[e:e + 1, :] for e in range(n_experts)]

    epg = EXPERTS_PER_GROUP
    scores = []
    for g in range(N_GROUPS):
        a, b, c, d = rb[epg * g:epg * g + epg]
        hi1, lo1 = jnp.maximum(a, b), jnp.minimum(a, b)
        hi2, lo2 = jnp.maximum(c, d), jnp.minimum(c, d)
        scores.append(jnp.maximum(hi1, hi2) + jnp.maximum(jnp.minimum(hi1, hi2), jnp.maximum(lo1, lo2)))
    best = jnp.zeros(scores[0].shape, jnp.int32)
    best_s = scores[0]
    for g in range(1, N_GROUPS):
        upd = scores[g] > best_s
        best = jnp.where(upd, g, best)
        best_s = jnp.where(upd, scores[g], best_s)

    vb, va = [], []
    for j in range(epg):
        xb, xa = rb[j], ra[j]
        for g in range(1, N_GROUPS):
            sel = best == g
            xb = jnp.where(sel, rb[epg * g + j], xb)
            xa = jnp.where(sel, ra[epg * g + j], xa)
        vb.append(xb)
        va.append(xa)

    i1 = jnp.zeros_like(best)
    m1, a1 = vb[0], va[0]
    for j in range(1, epg):
        upd = vb[j] > m1
        i1 = jnp.where(upd, j, i1)
        m1 = jnp.where(upd, vb[j], m1)
        a1 = jnp.where(upd, va[j], a1)
    i2 = jnp.zeros_like(best)
    m2 = jnp.full_like(m1, -jnp.inf)
    a2 = jnp.zeros_like(a1)
    for j in range(epg):
        upd = (i1 != j) & (vb[j] > m2)
        i2 = jnp.where(upd, j, i2)
        m2 = jnp.where(upd, vb[j], m2)
        a2 = jnp.where(upd, va[j], a2)

    idx_ref[0:1, :] = best * epg + i1
    idx_ref[1:2, :] = best * epg + i2
    tot = a1 + a2
    wt_ref[0:1, :] = a1 / tot
    wt_ref[1:2, :] = a2 / tot


def _router(xs, g, mod, layer, w_router_t, b_router, rows, tm, seg_args):
    d = xs.shape[1]
    e = w_router_t.shape[0]
    assert e == N_GROUPS * EXPERTS_PER_GROUP
    return pl.pallas_call(
        functools.partial(_router_kernel, seg_args=seg_args, n_experts=e),
        grid=(rows // tm,),
        in_specs=[
            pl.BlockSpec((tm, d), lambda i: (i, 0)),
            pl.BlockSpec((1, d), lambda i: (0, 0)),
            pl.BlockSpec((None, MOD_ROWS, d), lambda i: (layer, 0, 3)),
            pl.BlockSpec((None, MOD_ROWS, d), lambda i: (layer, 0, 4)),
            pl.BlockSpec((e, d), lambda i: (0, 0)),
            pl.BlockSpec((e, 1), lambda i: (0, 0)),
        ],
        out_specs=[
            pl.BlockSpec((tm, d // 2), lambda i: (i, 0)),
            pl.BlockSpec((2, tm), lambda i: (0, i)),
            pl.BlockSpec((2, tm), lambda i: (0, i)),
        ],
        out_shape=[
            jax.ShapeDtypeStruct((rows, d // 2), jnp.uint32),
            jax.ShapeDtypeStruct((2, rows), jnp.int32),
            jax.ShapeDtypeStruct((2, rows), F32),
        ],
        compiler_params=_cparams(("arbitrary",)),
        name="ffn_modulate_route",
    )(xs, g, mod, mod, w_router_t, b_router.reshape(e, 1))


def _route_kernel(idx_ref, pos_ref, te_ref, nu_ref, *, n_experts, tm, n_chunks):
    e_iota = lax.broadcasted_iota(jnp.int32, (n_experts, LANES), 0)
    idx_all = idx_ref[...]

    def count_col(k):
        col = jnp.zeros((n_experts, LANES), F32)
        for e in range(n_experts):
            col = jnp.where(e_iota == e, jnp.sum((idx_all[k] == e).astype(F32)), col)
        return col

    c0 = count_col(0)
    counts = c0 + count_col(1)
    tiles_per = jnp.floor((counts + (tm - 1)) * (1.0 / tm))
    tile_end = tiles_per
    s = 1
    while s < n_experts:
        tile_end = tile_end + jnp.where(e_iota >= s, pltpu.roll(tile_end, s, 0), 0.0)
        s *= 2
    row_off = (tile_end - tiles_per) * tm

    r = lax.broadcasted_iota(jnp.int32, (LANES, LANES), 0)
    c = lax.broadcasted_iota(jnp.int32, (LANES, LANES), 1)
    triu = (r <= c).astype(BF16)

    def body(ch, carry):
        new = []
        for k in range(2):
            onehot = (e_iota == idx_ref[k, pl.ds(ch, 1), :]).astype(F32)
            csum = jnp.dot(onehot.astype(BF16), triu, preferred_element_type=F32)
            posv = jnp.sum(onehot * (row_off + carry[k] + csum - 1.0), axis=0, keepdims=True)
            pos_ref[k, pl.ds(ch, 1), :] = posv.astype(jnp.int32)
            new.append(carry[k] + csum[:, LANES - 1:LANES])
        return tuple(new)

    lax.fori_loop(0, n_chunks, body, (jnp.zeros((n_experts, LANES), F32), c0))

    t_iota = lax.broadcasted_iota(jnp.int32, (n_experts, te_ref.shape[1]), 1).astype(F32)
    te = jnp.sum((tile_end[:, 0:1] <= t_iota).astype(F32), axis=0, keepdims=True)
    te_ref[...] = jnp.minimum(te, n_experts - 1.0).astype(jnp.int32)
    nu_ref[...] = tile_end[n_experts - 1:n_experts, :].astype(jnp.int32)


def _route(idx, n_experts, tm, n_tiles):
    k, rows = idx.shape
    assert k == 2 and rows % LANES == 0 and tm & (tm - 1) == 0
    assert n_tiles >= (k * rows) // tm + n_experts
    n_chunks = rows // LANES
    te_width = -(-n_tiles // LANES) * LANES
    pos, te, nu = pl.pallas_call(
        functools.partial(_route_kernel, n_experts=n_experts, tm=tm, n_chunks=n_chunks),
        out_shape=[
            jax.ShapeDtypeStruct((k, n_chunks, LANES), jnp.int32),
            jax.ShapeDtypeStruct((1, te_width), jnp.int32),
            jax.ShapeDtypeStruct((1, LANES), jnp.int32),
        ],
        compiler_params=pltpu.CompilerParams(vmem_limit_bytes=VMEM_LIMIT),
        name="moe_route",
    )(idx.reshape(k, n_chunks, LANES))
    return pos.reshape(k * rows), te[0, :n_tiles], nu[0, :1]


DISPATCH_ROWS = 512


def _dispatch_kernel(pos_ref, hp_ref, init_ref, hs_ref, sem, *, n_rows):
    del init_ref
    base = pl.program_id(0) * DISPATCH_ROWS

    def row_copy(k, r):
        return pltpu.make_async_copy(hp_ref.at[pl.ds(r, 1)], hs_ref.at[pl.ds(pos_ref[k * n_rows + base + r], 1)], sem)

    def start(r, carry):
        row_copy(0, r).start()
        row_copy(1, r).start()
        return carry

    def wait(r, carry):
        row_copy(0, r).wait()
        row_copy(1, r).wait()
        return carry

    lax.fori_loop(0, DISPATCH_ROWS, start, 0, unroll=8)
    lax.fori_loop(0, DISPATCH_ROWS, wait, 0, unroll=8)


def _dispatch(pos, hp, init):
    rows, half = hp.shape
    assert init.shape[1] == half and init.dtype == hp.dtype
    return pl.pallas_call(
        functools.partial(_dispatch_kernel, n_rows=rows),
        grid_spec=pltpu.PrefetchScalarGridSpec(
            num_scalar_prefetch=1,
            grid=(rows // DISPATCH_ROWS,),
            in_specs=[
                pl.BlockSpec((DISPATCH_ROWS, half), lambda i, pos: (i, 0)),
                pl.BlockSpec(memory_space=pl.ANY),
            ],
            out_specs=pl.BlockSpec(memory_space=pl.ANY),
            scratch_shapes=[pltpu.SemaphoreType.DMA],
        ),
        out_shape=jax.ShapeDtypeStruct(init.shape, hp.dtype),
        input_output_aliases={2: 0},
        compiler_params=_cparams(("arbitrary",)),
        name="moe_dispatch",
    )(pos, hp, init)


def _expert_kernel(te_ref, nused_ref, x_ref, wg_ref, wu_ref, wd_ref, o_ref, xs_scr, act_scr):
    i = pl.program_id(0)

    @pl.when(i >= nused_ref[0])
    def _():
        o_ref[...] = jnp.zeros_like(o_ref)

    @pl.when(i < nused_ref[0])
    def _():
        def rows_body(r, carry):
            rs = pl.ds(pl.multiple_of(r * ROW_CHUNK, ROW_CHUNK), ROW_CHUNK)
            xs_scr[rs, :] = _unpack_bf16_pairs(x_ref[rs, :]).astype(BF16)
            return carry

        lax.fori_loop(0, x_ref.shape[0] // ROW_CHUNK, rows_body, 0)

        xs = xs_scr[...]
        for c in range(act_scr.shape[1] // PAIR_BLOCK):
            cs = slice(c * PAIR_BLOCK, (c + 1) * PAIR_BLOCK)
            gate = jnp.dot(xs, wg_ref[:, cs], preferred_element_type=F32)
            up = jnp.dot(xs, wu_ref[:, cs], preferred_element_type=F32)
            act_scr[:, cs] = (gate * jax.nn.sigmoid(gate) * up).astype(BF16)
        act = act_scr[...]
        for c in range(wd_ref.shape[1] // PAIR_BLOCK):
            y = jnp.dot(act, wd_ref[:, c * PAIR_BLOCK:(c + 1) * PAIR_BLOCK], preferred_element_type=F32)
            o_ref[:, c * LANES:(c + 1) * LANES] = _pack_bf16_pairs(y)


def _experts(tile_expert, n_used, hs, wg, wu, wd, layer, tm):
    p, half = hs.shape
    _, e, d, ff = wg.shape
    assert ff % PAIR_BLOCK == 0
    n_tiles = p // tm

    def row(i, te, nu):
        return (jnp.minimum(i, nu[0] - 1), 0)

    def wspec(shape):
        return pl.BlockSpec((None, None) + shape, lambda i, te, nu: (layer, te[i], 0, 0), pipeline_mode=pl.Buffered(1))

    return pl.pallas_call(
        _expert_kernel,
        grid_spec=pltpu.PrefetchScalarGridSpec(
            num_scalar_prefetch=2,
            grid=(n_tiles,),
            in_specs=[pl.BlockSpec((tm, half), row), wspec((d, ff)), wspec((d, ff)), wspec((ff, d))],
            out_specs=pl.BlockSpec((tm, half), lambda i, te, nu: (i, 0)),
            scratch_shapes=[pltpu.VMEM((tm, d), BF16), pltpu.VMEM((tm, ff), BF16)],
        ),
        out_shape=jax.ShapeDtypeStruct((p, half), jnp.uint32),
        compiler_params=_cparams(("arbitrary",)),
        name="moe_experts",
    )(tile_expert, n_used, hs, wg, wu, wd)


COMBINE_ROWS = 256
COMBINE_CHUNK = 8


def _combine_kernel(pos_ref, x_ref, wt_ref, gt_ref, ys_ref, o_ref, buf, sems, *, seg_args, n_rows):
    i = pl.program_id(0)
    slot = i % 2
    seg = _seg_of_block(i, *seg_args)

    def row_copy(blk, sl, k, r):
        return pltpu.make_async_copy(ys_ref.at[pl.ds(pos_ref[k * n_rows + blk * COMBINE_ROWS + r], 1)],
                                     buf.at[sl, k, pl.ds(r, 1)], sems.at[sl])

    def issue_rows(blk, sl, r0):
        for rr in range(COMBINE_CHUNK):
            row_copy(blk, sl, 0, r0 + rr).start()
            row_copy(blk, sl, 1, r0 + rr).start()

    @pl.when(i == 0)
    def _():
        def body(c, carry):
            issue_rows(0, 0, c * COMBINE_CHUNK)
            return carry

        lax.fori_loop(0, COMBINE_ROWS // COMBINE_CHUNK, body, 0)

    def wait(r, carry):
        row_copy(i, slot, 0, r).wait()
        row_copy(i, slot, 1, r).wait()
        return carry

    lax.fori_loop(0, COMBINE_ROWS, wait, 0, unroll=8)

    last = pl.num_programs(0) - 1
    nxt = jnp.minimum(i + 1, last)
    gt = gt_ref[pl.ds(seg, 1), :]

    def body(c, carry):
        r0 = pl.multiple_of(c * COMBINE_CHUNK, COMBINE_CHUNK)
        issue_rows(nxt, 1 - slot, r0)
        rs = pl.ds(r0, COMBINE_CHUNK)
        y0 = _unpack_bf16_pairs(buf[slot, 0, rs, :])
        y1 = _unpack_bf16_pairs(buf[slot, 1, rs, :])
        w = wt_ref[rs, :]
        o_ref[rs, :] = x_ref[rs, :] + gt * (w[:, 0:1] * y0 + w[:, 1:2] * y1)
        return carry

    lax.fori_loop(0, COMBINE_ROWS // COMBINE_CHUNK, body, 0)

    @pl.when(i == last)
    def _():
        def drain(r, carry):
            row_copy(last, 1 - slot, 0, r).wait()
            row_copy(last, 1 - slot, 1, r).wait()
            return carry

        lax.fori_loop(0, COMBINE_ROWS, drain, 0, unroll=8)


def _combine(pos, xs, wts_t, mod, layer, ys, rows, seg_args):
    d = xs.shape[1]
    half = d // 2
    return pl.pallas_call(
        functools.partial(_combine_kernel, seg_args=seg_args, n_rows=rows),
        grid_spec=pltpu.PrefetchScalarGridSpec(
            num_scalar_prefetch=1,
            grid=(rows // COMBINE_ROWS,),
            in_specs=[
                pl.BlockSpec((COMBINE_ROWS, d), lambda i, pos: (i, 0)),
                pl.BlockSpec((COMBINE_ROWS, 2), lambda i, pos: (i, 0)),
                pl.BlockSpec((None, MOD_ROWS, d), lambda i, pos: (layer, 0, 5)),
                pl.BlockSpec(memory_space=pl.ANY),
            ],
            out_specs=pl.BlockSpec((COMBINE_ROWS, d), lambda i, pos: (i, 0)),
            scratch_shapes=[pltpu.VMEM((2, 2, COMBINE_ROWS, half), jnp.uint32), pltpu.SemaphoreType.DMA((2,))],
        ),
        out_shape=jax.ShapeDtypeStruct((rows, d), F32),
        compiler_params=_cparams(("arbitrary",)),
        name="moe_combine",
    )(pos, xs, wts_t, mod, ys)


def kernel(x, c, ctx, c_ctx, w_ada, b_ada, g_mix, g_ffn, w_in, b_in, g_q, g_k, sink, conv_w, conv_b, g_mh,
           w_br_attn, w_br_mlstm, w_out, w_router, b_router, w_gate, w_up, w_down):
    n_batch, lat_len, d = x.shape
    ctx_len = ctx.shape[1]
    depth = w_ada.shape[0]
    d_in = w_in.shape[2]
    aw = w_br_attn.shape[1]
    mw = w_br_mlstm.shape[1]
    mqk = conv_w.shape[2] // 2
    n_mh = mw // MLSTM_V_DIM
    kvw = (d_in - aw - 2 * mqk - 2 * mw - 4 * n_mh - 2 * d) // 2
    n_experts = w_router.shape[1]
    assert n_batch + 1 <= MOD_ROWS and 2 * n_mh <= LANES

    n_lat_rows = n_batch * lat_len
    n_ctx_rows = n_batch * ctx_len
    tm = _pick(n_ctx_rows, (512, 256))
    assert lat_len % tm == 0
    seg_args = (n_lat_rows // tm, lat_len // tm, n_batch)
    tm_e = 256

    o_aq = 0
    o_ak = o_aq + aw
    o_av = o_ak + kvw
    o_mq = o_av + kvw
    o_mk = o_mq + mqk
    o_mv = o_mk + mqk
    o_mo = o_mv + mw
    o_g = o_mo + mw
    o_ga = o_g + 4 * n_mh
    o_gm = o_ga + d
    order = [(0, o_g), (o_ga, 2 * d)]
    n_ak, n_av, n_mq, n_mv, n_mo, n_ga, n_gm = o_ak, o_av, o_mq, o_mv, o_mo, o_g, o_g + d

    xs = jnp.concatenate([x.reshape(n_lat_rows, d), ctx.reshape(n_ctx_rows, d)], axis=0)
    cvec = jnp.zeros((MOD_ROWS, d), F32).at[:n_batch].set(c).at[n_batch].set(c_ctx)
    mod = _adaln(cvec, w_ada, b_ada)
    tabs = _rope_tables(lat_len)
    w_router_t = w_router.T

    w_main = _wprep(w_in, o_g, 4 * n_mh)
    b_main = jnp.concatenate([b_in[:, o:o + w] for o, w in order], axis=1).reshape(depth, 1, -1)
    wg = jnp.zeros((depth, 2, d, LANES), F32)
    bg = jnp.zeros((depth, 2, 1, LANES), F32)
    for dr in range(2):
        gsl = slice(o_g + 2 * n_mh * dr, o_g + 2 * n_mh * (dr + 1))
        wg = wg.at[:, dr, :, :2 * n_mh].set(w_in[:, :, gsl])
        bg = bg.at[:, dr, 0, :2 * n_mh].set(b_in[:, gsl])
    wg = wg.astype(BF16)
    wa_b, wm_b, wo_b = w_br_attn.astype(BF16), w_br_mlstm.astype(BF16), w_out.astype(BF16)
    wgate_b, wup_b, wdown_b = w_gate.astype(BF16), w_up.astype(BF16), w_down.astype(BF16)

    n_tiles = (TOP_K * (n_lat_rows + n_ctx_rows)) // tm_e + n_experts
    hs = None
    for l in range(depth):
        need_ctx = l < depth - 1
        rows = n_lat_rows + (n_ctx_rows if need_ctx else 0)

        proj, gates = _inproj(xs, g_mix[l].reshape(1, d), mod, l, w_main, b_main, wg, bg, tm, seg_args)
        qk = _conv(proj, conv_w[l], conv_b[l], n_mq, n_lat_rows, lat_len, ctx_len)
        hfb = _mlstm(qk, proj, gates, mqk, mw, n_mv, n_batch, lat_len, ctx_len)
        qr, kr = _rope(proj, tabs, g_q[l], g_k[l], aw, kvw, n_ak, n_lat_rows)
        attn = _attention(sink[l], qr, kr, proj, aw, kvw, n_av, n_batch, lat_len, ctx_len, need_ctx)
        u = _branch(attn, hfb, proj, g_mh[l], wa_b, wm_b, l, n_mo, n_ga, n_gm, rows, tm)
        xs = _outproj(u, wo_b, xs, mod, l, tm, seg_args)

        hp, idx, wts = _router(xs, g_ffn[l].reshape(1, d), mod, l, w_router_t, b_router, rows, ROUTER_ROWS,
                               (n_lat_rows // ROUTER_ROWS, lat_len // ROUTER_ROWS, n_batch))
        pos, tile_expert, n_used = _route(idx, n_experts, tm_e, n_tiles)
        hs = _dispatch(pos, hp, jnp.zeros((n_tiles * tm_e, d // 2), jnp.uint32) if hs is None else hs)
        ys = _experts(tile_expert, n_used, hs, wgate_b, wup_b, wdown_b, l, tm_e)
        xs = _combine(pos, xs, wts.T, mod, l, ys, rows, (n_lat_rows // COMBINE_ROWS, lat_len // COMBINE_ROWS, n_batch))

    return xs[:n_lat_rows].reshape(n_batch, lat_len, d)
```

```python
import functools
import math

import jax
import jax.numpy as jnp
from jax import lax
from jax.experimental import pallas as pl
from jax.experimental.pallas import tpu as pltpu

GRID_W = 64
HEAD_DIM = 128
WINDOW = 128
QBLK = 128
ROPE_THETA = 10000.0
ROPE_PAIRS = HEAD_DIM // 4
ATTN_SCALE = HEAD_DIM ** -0.5
MLSTM_QK_DIM = 128
MLSTM_V_DIM = 256
MLSTM_CHUNK = 128
N_GROUPS = 4
EXPERTS_PER_GROUP = 4
TOP_K = 2
EPS = 1e-6
NEG = -1e30

LANES = 128
MOD_ROWS = 8
ROW_CHUNK = 64
VMEM_LIMIT = 56 << 20

F32 = jnp.float32
BF16 = jnp.bfloat16


def _pick(n, cands):
    for c in cands:
        if n % c == 0:
            return c
    raise ValueError(f"no tile in {cands} divides {n}")


def _cparams(sem, vmem=VMEM_LIMIT):
    return pltpu.CompilerParams(dimension_semantics=sem, vmem_limit_bytes=vmem)


def _seg_of_block(i, n_lat_blocks, blocks_per_batch, n_batch):
    return jnp.where(i < n_lat_blocks, i // blocks_per_batch, n_batch)


def _modulated(x, g, sc, sh):
    ms = jnp.mean(x * x, axis=-1, keepdims=True)
    y = x * lax.rsqrt(ms + EPS) * g
    return y * (1.0 + sc) + sh


def _adaln_kernel(c_ref, w_ref, b_ref, o_ref):
    c = c_ref[...]
    cs = (c * jax.nn.sigmoid(c)).astype(BF16)
    o_ref[...] = jnp.dot(cs, w_ref[...].astype(BF16), preferred_element_type=F32) + b_ref[...]


def _adaln(cvec, w_ada, b_ada):
    depth, d, n6 = w_ada.shape
    tn = _pick(n6, (512, 256, 128))
    return pl.pallas_call(
        _adaln_kernel,
        grid=(depth, n6 // tn),
        in_specs=[
            pl.BlockSpec((MOD_ROWS, d), lambda l, j: (0, 0)),
            pl.BlockSpec((None, d, tn), lambda l, j: (l, 0, j)),
            pl.BlockSpec((None, 1, tn), lambda l, j: (l, 0, j)),
        ],
        out_specs=pl.BlockSpec((None, MOD_ROWS, tn), lambda l, j: (l, 0, j)),
        out_shape=jax.ShapeDtypeStruct((depth, MOD_ROWS, n6), F32),
        compiler_params=_cparams(("arbitrary", "arbitrary")),
        name="adaln",
    )(cvec, w_ada, b_ada.reshape(depth, 1, n6))


def _wprep_kernel(a_ref, b_ref, o_ref, *, first_shifted, shift):
    j = pl.program_id(2)

    @pl.when(j < first_shifted)
    def _():
        o_ref[...] = a_ref[...].T.astype(o_ref.dtype)

    @pl.when(j >= first_shifted)
    def _():
        rows = jnp.concatenate([a_ref[shift:, :], b_ref[:shift, :]], axis=0)
        o_ref[...] = rows.T.astype(o_ref.dtype)


def _wprep(w_in, cut_start, cut_width):
    depth, d, d_in = w_in.shape
    nc = d_in - cut_width
    tw = _pick(math.gcd(cut_start, nc), (512, 256, 128))
    tk = _pick(d, (1024, 512, 256, 128))
    assert cut_width % 8 == 0 and cut_width <= LANES and tw % LANES == 0
    w_t = jnp.swapaxes(w_in, 1, 2)
    nxt = tw // LANES
    return pl.pallas_call(
        functools.partial(_wprep_kernel, first_shifted=cut_start // tw, shift=cut_width),
        grid=(depth, d // tk, nc // tw),
        in_specs=[
            pl.BlockSpec((None, tw, tk), lambda l, i, j: (l, j, i)),
            pl.BlockSpec((None, LANES, tk), lambda l, i, j: (l, (j + 1) * nxt, i)),
        ],
        out_specs=pl.BlockSpec((None, tk, tw), lambda l, i, j: (l, i, j)),
        out_shape=jax.ShapeDtypeStruct((depth, d, nc), BF16),
        compiler_params=_cparams(("arbitrary", "arbitrary", "arbitrary")),
        name="w_in_relayout",
    )(w_t, w_t)


INPROJ_TILES = (1024, 512, 256, 128)


def _inproj_kernel(x_ref, g_ref, sh_ref, sc_ref, w_ref, b_ref, wg_ref, bg_ref, o_ref, og_ref, h_scr, *, seg_args):
    i = pl.program_id(0)
    j = pl.program_id(1)

    @pl.when(j == 0)
    def _():
        seg = _seg_of_block(i, *seg_args)
        g = g_ref[...]
        sc = sc_ref[pl.ds(seg, 1), :]
        sh = sh_ref[pl.ds(seg, 1), :]

        def rows_body(r, carry):
            rs = pl.ds(pl.multiple_of(r * ROW_CHUNK, ROW_CHUNK), ROW_CHUNK)
            h_scr[rs, :] = _modulated(x_ref[rs, :], g, sc, sh).astype(BF16)
            return carry

        lax.fori_loop(0, x_ref.shape[0] // ROW_CHUNK, rows_body, 0)
        for d in range(2):
            og_ref[d] = jnp.dot(h_scr[...], wg_ref[d], preferred_element_type=F32) + bg_ref[d]

    o_ref[...] = (jnp.dot(h_scr[...], w_ref[...], preferred_element_type=F32) + b_ref[...]).astype(o_ref.dtype)


def _inproj(xs, g, mod, layer, w_main, b_main, w_gates, b_gates, tm, seg_args):
    t, d = xs.shape
    nc = w_main.shape[2]
    tn = _pick(nc, INPROJ_TILES)
    return pl.pallas_call(
        functools.partial(_inproj_kernel, seg_args=seg_args),
        grid=(t // tm, nc // tn),
        in_specs=[
            pl.BlockSpec((tm, d), lambda i, j: (i, 0)),
            pl.BlockSpec((1, d), lambda i, j: (0, 0)),
            pl.BlockSpec((None, MOD_ROWS, d), lambda i, j: (layer, 0, 0)),
            pl.BlockSpec((None, MOD_ROWS, d), lambda i, j: (layer, 0, 1)),
            pl.BlockSpec((None, d, tn), lambda i, j: (layer, 0, j)),
            pl.BlockSpec((None, 1, tn), lambda i, j: (layer, 0, j)),
            pl.BlockSpec((None, 2, d, LANES), lambda i, j: (layer, 0, 0, 0)),
            pl.BlockSpec((None, 2, 1, LANES), lambda i, j: (layer, 0, 0, 0)),
        ],
        out_specs=[
            pl.BlockSpec((tm, tn), lambda i, j: (i, j)),
            pl.BlockSpec((2, tm, LANES), lambda i, j: (0, i, 0)),
        ],
        out_shape=[
            jax.ShapeDtypeStruct((t, nc), BF16),
            jax.ShapeDtypeStruct((2, t, LANES), F32),
        ],
        scratch_shapes=[pltpu.VMEM((tm, d), BF16)],
        compiler_params=_cparams(("arbitrary", "arbitrary")),
        name="inproj",
    )(xs, g, mod, mod, w_main, b_main, w_gates, b_gates)


CONV_ROWS = 256
HALO_ROWS = 16


def _conv_kernel(cur_ref, prev_ref, next_ref, w_ref, b_ref, o_ref, *, n_lat_rows, lat_len, ctx_len, k_col_block):
    i = pl.program_id(0)
    j = pl.program_id(1)
    row0 = i * CONV_ROWS
    in_lat = row0 < n_lat_rows
    seg_len = jnp.where(in_lat, lat_len, ctx_len)
    off = jnp.where(in_lat, row0, row0 - n_lat_rows) % seg_len
    has_prev = (off != 0).astype(F32)
    has_next = (off + CONV_ROWS != seg_len).astype(F32)

    x = cur_ref[...].astype(F32)
    prev_row = prev_ref[HALO_ROWS - 1:HALO_ROWS, :].astype(F32) * has_prev
    next_row = next_ref[0:1, :].astype(F32) * has_next
    rows = lax.broadcasted_iota(jnp.int32, x.shape, 0)
    xm1 = jnp.where(rows == 0, prev_row, pltpu.roll(x, 1, 0))
    xp1 = jnp.where(rows == CONV_ROWS - 1, next_row, pltpu.roll(x, CONV_ROWS - 1, 0))
    w = w_ref[...]
    y = w[0:1, :] * xm1 + w[1:2, :] * x + w[2:3, :] * xp1 + b_ref[...]
    y = y * jax.nn.sigmoid(y)
    scale = jnp.where(j >= k_col_block, MLSTM_QK_DIM ** -0.5, 1.0).astype(F32)
    o_ref[...] = (y * scale).astype(o_ref.dtype)


def _conv(proj, conv_w, conv_b, qk_off, n_lat_rows, lat_len, ctx_len):
    t = proj.shape[0]
    width = conv_w.shape[1]
    tc = _pick(width // 2, (1024, 512, 256, 128))
    assert qk_off % tc == 0 and lat_len % CONV_ROWS == 0 and ctx_len % CONV_ROWS == 0
    cb = qk_off // tc
    halo_per_blk = CONV_ROWS // HALO_ROWS
    n_halo = t // HALO_ROWS
    return pl.pallas_call(
        functools.partial(_conv_kernel, n_lat_rows=n_lat_rows, lat_len=lat_len, ctx_len=ctx_len,
                          k_col_block=(width // 2) // tc),
        grid=(t // CONV_ROWS, width // tc),
        in_specs=[
            pl.BlockSpec((CONV_ROWS, tc), lambda i, j: (i, cb + j)),
            pl.BlockSpec((HALO_ROWS, tc), lambda i, j: (jnp.maximum(i * halo_per_blk - 1, 0), cb + j)),
            pl.BlockSpec((HALO_ROWS, tc), lambda i, j: (jnp.minimum((i + 1) * halo_per_blk, n_halo - 1), cb + j)),
            pl.BlockSpec((3, tc), lambda i, j: (0, j)),
            pl.BlockSpec((1, tc), lambda i, j: (0, j)),
        ],
        out_specs=pl.BlockSpec((CONV_ROWS, tc), lambda i, j: (i, j)),
        out_shape=jax.ShapeDtypeStruct((t, width), BF16),
        compiler_params=_cparams(("arbitrary", "arbitrary")),
        name="qk_conv",
    )(proj, proj, proj, conv_w, conv_b.reshape(1, width))


ROPE_ROWS = 512
ROPE_HEAD_GROUP = 4


def _rope_kernel(q_ref, k_ref, cos_ref, s1_ref, s2_ref, gq_ref, gk_ref, qo_ref, ko_ref, *, n_q_heads, n_k_heads):
    cos = cos_ref[...]
    s1 = s1_ref[...]
    s2 = s2_ref[...]

    def prep_heads(src_ref, dst_ref, g, n_heads, scale):
        for h0 in range(0, n_heads, ROPE_HEAD_GROUP):
            sls = [slice(h * HEAD_DIM, (h + 1) * HEAD_DIM) for h in range(h0, min(h0 + ROPE_HEAD_GROUP, n_heads))]
            xs = [src_ref[:, sl].astype(F32) for sl in sls]
            inv = [lax.rsqrt(jnp.mean(x * x, axis=-1, keepdims=True) + EPS) for x in xs]
            xn = [x * r * g for x, r in zip(xs, inv)]
            up = [pltpu.roll(x, HEAD_DIM - ROPE_PAIRS, 1) for x in xn]
            dn = [pltpu.roll(x, ROPE_PAIRS, 1) for x in xn]
            for sl, x, u, dwn in zip(sls, xn, up, dn):
                y = x * cos + u * s1 + dwn * s2
                dst_ref[:, sl] = (y * scale if scale != 1.0 else y).astype(dst_ref.dtype)

    prep_heads(q_ref, qo_ref, gq_ref[...], n_q_heads, ATTN_SCALE)
    prep_heads(k_ref, ko_ref, gk_ref[...], n_k_heads, 1.0)


def _rope(proj, tabs, g_q, g_k, aw, kvw, k_off, n_lat_rows):
    t = proj.shape[0]
    assert k_off % kvw == 0
    kb = k_off // kvw
    n_lat_blk = n_lat_rows // ROPE_ROWS
    lat_blk_per_batch = (tabs[0].shape[0] - ROPE_ROWS) // ROPE_ROWS

    def tab_map(i):
        return (jnp.where(i < n_lat_blk, i % lat_blk_per_batch, lat_blk_per_batch), 0)

    tab_spec = pl.BlockSpec((ROPE_ROWS, HEAD_DIM), tab_map)
    return pl.pallas_call(
        functools.partial(_rope_kernel, n_q_heads=aw // HEAD_DIM, n_k_heads=kvw // HEAD_DIM),
        grid=(t // ROPE_ROWS,),
        in_specs=[
            pl.BlockSpec((ROPE_ROWS, aw), lambda i: (i, 0)),
            pl.BlockSpec((ROPE_ROWS, kvw), lambda i: (i, kb)),
            tab_spec, tab_spec, tab_spec,
            pl.BlockSpec((1, HEAD_DIM), lambda i: (0, 0)),
            pl.BlockSpec((1, HEAD_DIM), lambda i: (0, 0)),
        ],
        out_specs=[
            pl.BlockSpec((ROPE_ROWS, aw), lambda i: (i, 0)),
            pl.BlockSpec((ROPE_ROWS, kvw), lambda i: (i, 0)),
        ],
        out_shape=[jax.ShapeDtypeStruct((t, aw), BF16), jax.ShapeDtypeStruct((t, kvw), BF16)],
        compiler_params=_cparams(("arbitrary",)),
        name="qk_norm_rope",
    )(proj, proj, tabs[0], tabs[1], tabs[2], g_q.reshape(1, HEAD_DIM), g_k.reshape(1, HEAD_DIM))


def _rope_tables(n_lat):
    rows = n_lat // GRID_W
    inv_freq = ROPE_THETA ** (-jnp.arange(ROPE_PAIRS, dtype=F32) / ROPE_PAIRS)
    row_pos = jnp.repeat(jnp.arange(rows, dtype=F32), GRID_W)
    col_pos = jnp.tile(jnp.arange(GRID_W, dtype=F32), rows)
    ang_r = row_pos[:, None] * inv_freq
    ang_c = col_pos[:, None] * inv_freq
    zeros = jnp.zeros_like(ang_r)
    cos = jnp.concatenate([jnp.cos(ang_r), jnp.cos(ang_r), jnp.cos(ang_c), jnp.cos(ang_c)], axis=-1)
    s1 = jnp.concatenate([-jnp.sin(ang_r), zeros, -jnp.sin(ang_c), zeros], axis=-1)
    s2 = jnp.concatenate([zeros, jnp.sin(ang_r), zeros, jnp.sin(ang_c)], axis=-1)
    ident = jnp.ones((ROPE_ROWS, HEAD_DIM), F32)
    zpad = jnp.zeros((ROPE_ROWS, HEAD_DIM), F32)
    return (jnp.concatenate([cos, ident], 0), jnp.concatenate([s1, zpad], 0), jnp.concatenate([s2, zpad], 0))


def _attn_kernel(sink_ref, q_ref, kp_ref, kc_ref, kn_ref, kx_ref, vp_ref, vc_ref, vn_ref, vx_ref, o_ref,
                 *, n_lat_blk, n_kv, group, ctx_len):
    n = pl.program_id(1)
    is_ctx = n >= n_lat_blk
    n_band = 3 * QBLK
    n_keys = n_band + ctx_len
    qi = lax.broadcasted_iota(jnp.int32, (QBLK, n_keys), 0)
    kj = lax.broadcasted_iota(jnp.int32, (QBLK, n_keys), 1)
    rel = kj - QBLK - qi
    kpos = n * QBLK + kj - QBLK
    band_ok = (jnp.abs(rel) <= WINDOW) & (kpos >= 0) & (kpos < n_lat_blk * QBLK) & jnp.logical_not(is_ctx)
    valid = band_ok | (kj >= n_band)

    for hk in range(n_kv):
        ksl = slice(hk * HEAD_DIM, (hk + 1) * HEAD_DIM)
        k_all = jnp.concatenate([kp_ref[:, ksl], kc_ref[:, ksl], kn_ref[:, ksl], kx_ref[:, ksl]], axis=0)
        v_all = jnp.concatenate([vp_ref[:, ksl], vc_ref[:, ksl], vn_ref[:, ksl], vx_ref[:, ksl]], axis=0)
        heads = [hk * group + g for g in range(group)]
        qsl = {h: slice(h * HEAD_DIM, (h + 1) * HEAD_DIM) for h in heads}
        s = {h: jnp.where(valid, lax.dot_general(q_ref[:, qsl[h]], k_all, (((1,), (1,)), ((), ())),
                                                 preferred_element_type=F32), NEG) for h in heads}
        m = {h: jnp.maximum(jnp.max(s[h], axis=-1, keepdims=True), sink_ref[h]) for h in heads}
        p = {h: jnp.exp(s[h] - m[h]) for h in heads}
        denom = {h: jnp.sum(p[h], axis=-1, keepdims=True) + jnp.exp(sink_ref[h] - m[h]) for h in heads}
        o = {h: jnp.dot(p[h].astype(BF16), v_all, preferred_element_type=F32) for h in heads}
        for h in heads:
            o_ref[:, qsl[h]] = (o[h] / denom[h]).astype(o_ref.dtype)


def _attention(sink, qr, kr, proj, aw, kvw, v_off, n_batch, lat_len, ctx_len, with_ctx):
    t = proj.shape[0]
    assert v_off % kvw == 0
    vb = v_off // kvw
    n_lat_blk = lat_len // QBLK
    n_ctx_blk = ctx_len // QBLK
    n_lat_rows = n_batch * lat_len
    nblk = n_lat_blk + (n_ctx_blk if with_ctx else 0)

    def qrow(b, n):
        return jnp.where(n < n_lat_blk, b * n_lat_blk + n, n_lat_rows // QBLK + b * n_ctx_blk + (n - n_lat_blk))

    def band(delta):
        def f(b, n):
            nn = jnp.clip(jnp.minimum(n, n_lat_blk - 1) + delta, 0, n_lat_blk - 1)
            return b * n_lat_blk + nn
        return f

    def ctx_row(b, n):
        return n_lat_rows // ctx_len + b

    def kspec(rowf):
        return pl.BlockSpec((QBLK, kvw), lambda b, n: (rowf(b, n), 0))

    def vspec(rowf):
        return pl.BlockSpec((QBLK, kvw), lambda b, n: (rowf(b, n), vb))

    return pl.pallas_call(
        functools.partial(_attn_kernel, n_lat_blk=n_lat_blk, n_kv=kvw // HEAD_DIM,
                          group=aw // kvw, ctx_len=ctx_len),
        grid=(n_batch, nblk),
        in_specs=[
            pl.BlockSpec(memory_space=pltpu.SMEM),
            pl.BlockSpec((QBLK, aw), lambda b, n: (qrow(b, n), 0)),
            kspec(band(-1)), kspec(band(0)), kspec(band(1)),
            pl.BlockSpec((ctx_len, kvw), lambda b, n: (ctx_row(b, n), 0)),
            vspec(band(-1)), vspec(band(0)), vspec(band(1)),
            pl.BlockSpec((ctx_len, kvw), lambda b, n: (ctx_row(b, n), vb)),
        ],
        out_specs=pl.BlockSpec((QBLK, aw), lambda b, n: (qrow(b, n), 0)),
        out_shape=jax.ShapeDtypeStruct((n_lat_rows + (n_batch * ctx_len if with_ctx else 0), aw), BF16),
        compiler_params=_cparams(("arbitrary", "arbitrary")),
        name="attention",
    )(sink, qr, kr, kr, kr, kr, proj, proj, proj, proj)


MLSTM_GROUP = 8


def _mlstm_kernel(*refs, n_heads, n_vblk):
    per_dir = 3 + n_vblk
    dir_ins = [refs[:per_dir], refs[per_dir:2 * per_dir]]
    outs, state = refs[2 * per_dir:2 * per_dir + 2], refs[2 * per_dir + 2:]
    heads_per_vblk = n_heads // n_vblk
    n_scans = 2 * n_heads
    c_scrs, nm_scrs = state[:n_scans], state[n_scans:]
    c = pl.program_id(1)
    L = MLSTM_CHUNK
    dk = MLSTM_QK_DIM
    dv = MLSTM_V_DIM

    @pl.when(c == 0)
    def _():
        for scr in state:
            scr[...] = jnp.zeros_like(scr)

    r = lax.broadcasted_iota(jnp.int32, (L, L), 0)
    s = lax.broadcasted_iota(jnp.int32, (L, L), 1)

    def split_dot_l(mat_b, x):
        hi = x.astype(BF16)
        lo = (x - hi.astype(F32)).astype(BF16)
        return jnp.dot(mat_b, hi, preferred_element_type=F32) + jnp.dot(mat_b, lo, preferred_element_type=F32)

    def split_dot_r(x, mat_b):
        hi = x.astype(BF16)
        lo = (x - hi.astype(F32)).astype(BF16)
        return jnp.dot(hi, mat_b, preferred_element_type=F32) + jnp.dot(lo, mat_b, preferred_element_type=F32)

    tris, gate_cols, gate_rows, cum_cols, cum_rows, end_cols = [], [], [], [], [], []
    for dirn in range(2):
        tri = (s <= r) if dirn == 0 else (s >= r)
        tri_t = (r <= s) if dirn == 0 else (r >= s)
        gates = dir_ins[dirn][2][...]
        logf = jnp.minimum(gates, 0.0) - jnp.log1p(jnp.exp(-jnp.abs(gates)))
        cum_col = split_dot_l(tri.astype(BF16), logf)
        cum_row = split_dot_r(logf.T, tri_t.astype(BF16))
        tris.append(tri)
        gate_cols.append(gates)
        gate_rows.append(gates.T)
        cum_cols.append(cum_col)
        cum_rows.append(cum_row)
        end_cols.append(cum_col[L - 1:L, :] if dirn == 0 else cum_col[0:1, :])

    nt = (((1,), (1,)), ((), ()))
    scans = [(dirn, h) for dirn in range(2) for h in range(n_heads)]
    for g0 in range(0, n_scans, MLSTM_GROUP):
        hs = scans[g0:g0 + MLSTM_GROUP]
        sid = {x: x[0] * n_heads + x[1] for x in hs}
        qs = {x: dir_ins[x[0]][0][:, x[1] * dk:(x[1] + 1) * dk] for x in hs}
        ks = {x: dir_ins[x[0]][1][:, x[1] * dk:(x[1] + 1) * dk] for x in hs}
        vs = {x: dir_ins[x[0]][3 + x[1] // heads_per_vblk][:, (x[1] % heads_per_vblk) * dv:
                                                           (x[1] % heads_per_vblk + 1) * dv] for x in hs}
        b_col = {x: cum_cols[x[0]][:, n_heads + x[1]:n_heads + x[1] + 1] for x in hs}
        b_row = {x: cum_rows[x[0]][n_heads + x[1]:n_heads + x[1] + 1, :] for x in hs}
        i_col = {x: gate_cols[x[0]][:, x[1]:x[1] + 1] for x in hs}
        i_row = {x: gate_rows[x[0]][x[1]:x[1] + 1, :] for x in hs}
        b_end = {x: end_cols[x[0]][:, n_heads + x[1]:n_heads + x[1] + 1] for x in hs}
        n_prev = {x: nm_scrs[sid[x]][0:1, :] for x in hs}
        m_prev = {x: nm_scrs[sid[x]][1:2, 0:1] for x in hs}
        ct_prev = {x: c_scrs[sid[x]][...] for x in hs}

        dmat = {h: jnp.where(tris[h[0]], b_col[h] - b_row[h] + i_row[h], NEG) for h in hs}
        m_inter = {h: b_col[h] + m_prev[h] for h in hs}
        m_t = {h: jnp.maximum(m_inter[h], jnp.max(dmat[h], axis=-1, keepdims=True)) for h in hs}
        qk = {h: lax.dot_general(qs[h], ks[h], nt, preferred_element_type=F32) for h in hs}
        qc = {h: jnp.dot(qs[h], ct_prev[h].astype(BF16), preferred_element_type=F32) for h in hs}
        qn_prev = {h: jnp.sum(qs[h].astype(F32) * n_prev[h], axis=-1, keepdims=True) for h in hs}
        a = {h: jnp.exp(m_inter[h] - m_t[h]) for h in hs}
        smat = {h: qk[h] * jnp.exp(dmat[h] - m_t[h]) for h in hs}
        sv = {h: jnp.dot(smat[h].astype(BF16), vs[h], preferred_element_type=F32) for h in hs}
        qn = {h: jnp.sum(smat[h], axis=-1, keepdims=True) + a[h] * qn_prev[h] for h in hs}

        g_row = {h: b_end[h] - b_row[h] + i_row[h] for h in hs}
        m_new = {h: jnp.maximum(b_end[h] + m_prev[h], jnp.max(g_row[h], axis=-1, keepdims=True)) for h in hs}
        kw = {h: ks[h].astype(F32) * jnp.exp(b_end[h] - b_col[h] + i_col[h] - m_new[h]) for h in hs}
        a_end = {h: jnp.exp(b_end[h] + m_prev[h] - m_new[h]) for h in hs}
        kv = {h: jnp.dot(kw[h].T.astype(BF16), vs[h], preferred_element_type=F32) for h in hs}

        for h in hs:
            hout = (sv[h] + a[h] * qc[h]) / jnp.maximum(jnp.abs(qn[h]), jnp.exp(-m_t[h]))
            outs[h[0]][:, h[1] * dv:(h[1] + 1) * dv] = hout.astype(outs[h[0]].dtype)
        for h in hs:
            c_scrs[sid[h]][...] = a_end[h] * ct_prev[h] + kv[h]
            nm_scrs[sid[h]][0:1, :] = a_end[h] * n_prev[h] + jnp.sum(kw[h], axis=0, keepdims=True)
            nm_scrs[sid[h]][1:2, :] = jnp.broadcast_to(m_new[h], (1, dk))


def _mlstm(qk, proj, gates, mqk, mw, v_off, n_batch, lat_len, ctx_len):
    t = proj.shape[0]
    L = MLSTM_CHUNK
    n_heads = mw // MLSTM_V_DIM
    assert mqk == n_heads * MLSTM_QK_DIM
    vw = math.gcd(mw, v_off)
    n_vblk = mw // vw
    assert vw % MLSTM_V_DIM == 0
    n_ctx = ctx_len // L
    n_lat = lat_len // L
    lat_blocks = n_batch * n_lat

    def row(d, b, c):
        cc = c if d == 0 else n_ctx - 1 - c
        lc = c - n_ctx if d == 0 else n_lat - 1 - (c - n_ctx)
        return jnp.where(c < n_ctx, lat_blocks + b * n_ctx + cc, b * n_lat + lc)

    def dir_specs(d):
        return [
            pl.BlockSpec((L, mqk), lambda b, c: (row(d, b, c), 0)),
            pl.BlockSpec((L, mqk), lambda b, c: (row(d, b, c), 1)),
            pl.BlockSpec((None, L, LANES), lambda b, c: (d, row(d, b, c), 0)),
        ] + [pl.BlockSpec((L, vw), lambda b, c, j=j: (row(d, b, c), v_off // vw + j)) for j in range(n_vblk)]

    dir_args = [qk, qk, gates] + [proj] * n_vblk
    return pl.pallas_call(
        functools.partial(_mlstm_kernel, n_heads=n_heads, n_vblk=n_vblk),
        grid=(n_batch, n_ctx + n_lat),
        in_specs=dir_specs(0) + dir_specs(1),
        out_specs=[pl.BlockSpec((L, mw), lambda b, c: (row(0, b, c), 0)),
                   pl.BlockSpec((L, mw), lambda b, c: (row(1, b, c), 0))],
        out_shape=[jax.ShapeDtypeStruct((t, mw), BF16)] * 2,
        scratch_shapes=([pltpu.VMEM((MLSTM_QK_DIM, MLSTM_V_DIM), F32)] * (2 * n_heads)
                        + [pltpu.VMEM((8, MLSTM_QK_DIM), F32)] * (2 * n_heads)),
        compiler_params=_cparams(("arbitrary", "arbitrary")),
        name="mlstm",
    )(*dir_args, *dir_args)


def _branch_kernel(attn_ref, hf_ref, hb_ref, gmh_ref, wa_ref, wm_ref, ga_ref, gm_ref, *rest, n_heads):
    mo_refs, o_ref, hm_scr = rest[:-2], rest[-2], rest[-1]
    heads_per_blk = n_heads // len(mo_refs)
    j = pl.program_id(1)

    @pl.when(j == 0)
    def _():
        dv = MLSTM_V_DIM
        for h in range(n_heads):
            sl = slice(h * dv, (h + 1) * dv)
            mo = mo_refs[h // heads_per_blk][:, (h % heads_per_blk) * dv:(h % heads_per_blk + 1) * dv]
            hsum = hf_ref[:, sl].astype(F32) + hb_ref[:, sl].astype(F32)
            x = jax.nn.sigmoid(mo.astype(F32)) * hsum
            y = x * lax.rsqrt(jnp.mean(x * x, axis=-1, keepdims=True) + EPS) * gmh_ref[:, sl]
            hm_scr[:, sl] = y.astype(BF16)

    ya = jnp.dot(attn_ref[...], wa_ref[...], preferred_element_type=F32)
    ym = jnp.dot(hm_scr[...], wm_ref[...], preferred_element_type=F32)
    u = jax.nn.sigmoid(ga_ref[...].astype(F32)) * ya + jax.nn.sigmoid(gm_ref[...].astype(F32)) * ym
    o_ref[...] = u.astype(o_ref.dtype)


def _branch(attn, hfb, proj, g_mh, wa, wm, layer, mo_off, ga_off, gm_off, rows, tm):
    aw = attn.shape[1]
    mw = hfb[0].shape[1]
    d = wa.shape[2]
    tn = _pick(d, (1024, 512, 256, 128))
    assert ga_off % tn == 0 and gm_off % tn == 0
    gab, gmb = ga_off // tn, gm_off // tn
    mo_w = math.gcd(mw, mo_off)
    n_mo_blk = mw // mo_w
    assert mo_w % MLSTM_V_DIM == 0
    return pl.pallas_call(
        functools.partial(_branch_kernel, n_heads=mw // MLSTM_V_DIM),
        grid=(rows // tm, d // tn),
        in_specs=[
            pl.BlockSpec((tm, aw), lambda i, j: (i, 0)),
            pl.BlockSpec((tm, mw), lambda i, j: (i, 0)),
            pl.BlockSpec((tm, mw), lambda i, j: (i, 0)),
            pl.BlockSpec((1, mw), lambda i, j: (0, 0)),
            pl.BlockSpec((None, aw, tn), lambda i, j: (layer, 0, j)),
            pl.BlockSpec((None, mw, tn), lambda i, j: (layer, 0, j)),
            pl.BlockSpec((tm, tn), lambda i, j: (i, gab + j)),
            pl.BlockSpec((tm, tn), lambda i, j: (i, gmb + j)),
        ] + [pl.BlockSpec((tm, mo_w), lambda i, j, b=b: (i, mo_off // mo_w + b)) for b in range(n_mo_blk)],
        out_specs=pl.BlockSpec((tm, tn), lambda i, j: (i, j)),
        out_shape=jax.ShapeDtypeStruct((rows, d), BF16),
        scratch_shapes=[pltpu.VMEM((tm, mw), BF16)],
        compiler_params=_cparams(("arbitrary", "arbitrary")),
        name="branch_merge",
    )(attn, hfb[0], hfb[1], g_mh.reshape(1, mw), wa, wm, proj, proj, *([proj] * n_mo_blk))


def _outproj_kernel(u_ref, w_ref, x_ref, gt_ref, o_ref, *, seg_args):
    seg = _seg_of_block(pl.program_id(0), *seg_args)
    y = jnp.dot(u_ref[...], w_ref[...], preferred_element_type=F32)
    o_ref[...] = x_ref[...] + gt_ref[pl.ds(seg, 1), :] * y


def _outproj(u, w_out, xs, mod, layer, tm, seg_args):
    rows, d = u.shape
    tn = _pick(d, (1024, 512, 256, 128))
    gate_blk = 2 * (d // tn)
    return pl.pallas_call(
        functools.partial(_outproj_kernel, seg_args=seg_args),
        grid=(rows // tm, d // tn),
        in_specs=[
            pl.BlockSpec((tm, d), lambda i, j: (i, 0)),
            pl.BlockSpec((None, d, tn), lambda i, j: (layer, 0, j)),
            pl.BlockSpec((tm, tn), lambda i, j: (i, j)),
            pl.BlockSpec((None, MOD_ROWS, tn), lambda i, j: (layer, 0, gate_blk + j)),
        ],
        out_specs=pl.BlockSpec((tm, tn), lambda i, j: (i, j)),
        out_shape=jax.ShapeDtypeStruct(xs.shape, F32),
        input_output_aliases={2: 0},
        compiler_params=_cparams(("arbitrary", "arbitrary")),
        name="outproj_residual",
    )(u, w_out, xs, mod)


PAIR_BLOCK = 2 * LANES


def _pack_bf16_pairs(h):
    blocks = []
    for b in range(h.shape[1] // PAIR_BLOCK):
        hi = pltpu.bitcast(h[:, b * PAIR_BLOCK:b * PAIR_BLOCK + LANES].astype(BF16).astype(F32), jnp.uint32)
        lo = pltpu.bitcast(h[:, b * PAIR_BLOCK + LANES:(b + 1) * PAIR_BLOCK].astype(BF16).astype(F32), jnp.uint32)
        blocks.append(hi | (lo >> 16))
    return blocks[0] if len(blocks) == 1 else jnp.concatenate(blocks, axis=1)


def _unpack_bf16_pairs(p):
    blocks = []
    for b in range(p.shape[1] // LANES):
        w = p[:, b * LANES:(b + 1) * LANES]
        blocks.append(pltpu.bitcast(w & jnp.uint32(0xFFFF0000), F32))
        blocks.append(pltpu.bitcast(w << 16, F32))
    return jnp.concatenate(blocks, axis=1)


ROUTER_ROWS = 256


def _router_kernel(x_ref, g_ref, sh_ref, sc_ref, wr_ref, br_ref, hp_ref, idx_ref, wt_ref, *, seg_args, n_experts):
    seg = _seg_of_block(pl.program_id(0), *seg_args)
    h = _modulated(x_ref[...], g_ref[...], sc_ref[pl.ds(seg, 1), :], sh_ref[pl.ds(seg, 1), :])
    hp_ref[...] = _pack_bf16_pairs(h)

    wr = wr_ref[...]
    h_hi = h.astype(BF16)
    h_lo = (h - h_hi.astype(F32)).astype(BF16)
    w_hi = wr.astype(BF16)
    w_lo = (wr - w_hi.astype(F32)).astype(BF16)
    nt = (((1,), (1,)), ((), ()))
    logits = (lax.dot_general(w_hi, h_hi, nt, preferred_element_type=F32)
              + lax.dot_general(w_hi, h_lo, nt, preferred_element_type=F32)
              + lax.dot_general(w_lo, h_hi, nt, preferred_element_type=F32))
    aff = jax.nn.sigmoid(logits)
    biased = aff + br_ref[...]
    rb = [biased[e:e + 1, :] for e in range(n_experts)]
    ra = [aff[e:e + 1, :] for e in range(n_experts)]

    epg = EXPERTS_PER_GROUP
    scores = []
    for g in range(N_GROUPS):
        a, b, c, d = rb[epg * g:epg * g + epg]
        hi1, lo1 = jnp.maximum(a, b), jnp.minimum(a, b)
        hi2, lo2 = jnp.maximum(c, d), jnp.minimum(c, d)
        scores.append(jnp.maximum(hi1, hi2) + jnp.maximum(jnp.minimum(hi1, hi2), jnp.maximum(lo1, lo2)))
    best = jnp.zeros(scores[0].shape, jnp.int32)
    best_s = scores[0]
    for g in range(1, N_GROUPS):
        upd = scores[g] > best_s
        best = jnp.where(upd, g, best)
        best_s = jnp.where(upd, scores[g], best_s)

    vb, va = [], []
    for j in range(epg):
        xb, xa = rb[j], ra[j]
        for g in range(1, N_GROUPS):
            sel = best == g
            xb = jnp.where(sel, rb[epg * g + j], xb)
            xa = jnp.where(sel, ra[epg * g + j], xa)
        vb.append(xb)
        va.append(xa)

    i1 = jnp.zeros_like(best)
    m1, a1 = vb[0], va[0]
    for j in range(1, epg):
        upd = vb[j] > m1
        i1 = jnp.where(upd, j, i1)
        m1 = jnp.where(upd, vb[j], m1)
        a1 = jnp.where(upd, va[j], a1)
    i2 = jnp.zeros_like(best)
    m2 = jnp.full_like(m1, -jnp.inf)
    a2 = jnp.zeros_like(a1)
    for j in range(epg):
        upd = (i1 != j) & (vb[j] > m2)
        i2 = jnp.where(upd, j, i2)
        m2 = jnp.where(upd, vb[j], m2)
        a2 = jnp.where(upd, va[j], a2)

    idx_ref[0:1, :] = best * epg + i1
    idx_ref[1:2, :] = best * epg + i2
    tot = a1 + a2
    wt_ref[0:1, :] = a1 / tot
    wt_ref[1:2, :] = a2 / tot


def _router(xs, g, mod, layer, w_router_t, b_router, rows, tm, seg_args):
    d = xs.shape[1]
    e = w_router_t.shape[0]
    assert e == N_GROUPS * EXPERTS_PER_GROUP
    return pl.pallas_call(
        functools.partial(_router_kernel, seg_args=seg_args, n_experts=e),
        grid=(rows // tm,),
        in_specs=[
            pl.BlockSpec((tm, d), lambda i: (i, 0)),
            pl.BlockSpec((1, d), lambda i: (0, 0)),
            pl.BlockSpec((None, MOD_ROWS, d), lambda i: (layer, 0, 3)),
            pl.BlockSpec((None, MOD_ROWS, d), lambda i: (layer, 0, 4)),
            pl.BlockSpec((e, d), lambda i: (0, 0)),
            pl.BlockSpec((e, 1), lambda i: (0, 0)),
        ],
        out_specs=[
            pl.BlockSpec((tm, d // 2), lambda i: (i, 0)),
            pl.BlockSpec((2, tm), lambda i: (0, i)),
            pl.BlockSpec((2, tm), lambda i: (0, i)),
        ],
        out_shape=[
            jax.ShapeDtypeStruct((rows, d // 2), jnp.uint32),
            jax.ShapeDtypeStruct((2, rows), jnp.int32),
            jax.ShapeDtypeStruct((2, rows), F32),
        ],
        compiler_params=_cparams(("arbitrary",)),
        name="ffn_modulate_route",
    )(xs, g, mod, mod, w_router_t, b_router.reshape(e, 1))


def _route_kernel(idx_ref, pos_ref, te_ref, nu_ref, *, n_experts, tm, n_chunks):
    e_iota = lax.broadcasted_iota(jnp.int32, (n_experts, LANES), 0)
    idx_all = idx_ref[...]

    def count_col(k):
        col = jnp.zeros((n_experts, LANES), F32)
        for e in range(n_experts):
            col = jnp.where(e_iota == e, jnp.sum((idx_all[k] == e).astype(F32)), col)
        return col

    c0 = count_col(0)
    counts = c0 + count_col(1)
    tiles_per = jnp.floor((counts + (tm - 1)) * (1.0 / tm))
    tile_end = tiles_per
    s = 1
    while s < n_experts:
        tile_end = tile_end + jnp.where(e_iota >= s, pltpu.roll(tile_end, s, 0), 0.0)
        s *= 2
    row_off = (tile_end - tiles_per) * tm

    r = lax.broadcasted_iota(jnp.int32, (LANES, LANES), 0)
    c = lax.broadcasted_iota(jnp.int32, (LANES, LANES), 1)
    triu = (r <= c).astype(BF16)

    def body(ch, carry):
        new = []
        for k in range(2):
            onehot = (e_iota == idx_ref[k, pl.ds(ch, 1), :]).astype(F32)
            csum = jnp.dot(onehot.astype(BF16), triu, preferred_element_type=F32)
            posv = jnp.sum(onehot * (row_off + carry[k] + csum - 1.0), axis=0, keepdims=True)
            pos_ref[k, pl.ds(ch, 1), :] = posv.astype(jnp.int32)
            new.append(carry[k] + csum[:, LANES - 1:LANES])
        return tuple(new)

    lax.fori_loop(0, n_chunks, body, (jnp.zeros((n_experts, LANES), F32), c0))

    t_iota = lax.broadcasted_iota(jnp.int32, (n_experts, te_ref.shape[1]), 1).astype(F32)
    te = jnp.sum((tile_end[:, 0:1] <= t_iota).astype(F32), axis=0, keepdims=True)
    te_ref[...] = jnp.minimum(te, n_experts - 1.0).astype(jnp.int32)
    nu_ref[...] = tile_end[n_experts - 1:n_experts, :].astype(jnp.int32)


def _route(idx, n_experts, tm, n_tiles):
    k, rows = idx.shape
    assert k == 2 and rows % LANES == 0 and tm & (tm - 1) == 0
    assert n_tiles >= (k * rows) // tm + n_experts
    n_chunks = rows // LANES
    te_width = -(-n_tiles // LANES) * LANES
    pos, te, nu = pl.pallas_call(
        functools.partial(_route_kernel, n_experts=n_experts, tm=tm, n_chunks=n_chunks),
        out_shape=[
            jax.ShapeDtypeStruct((k, n_chunks, LANES), jnp.int32),
            jax.ShapeDtypeStruct((1, te_width), jnp.int32),
            jax.ShapeDtypeStruct((1, LANES), jnp.int32),
        ],
        compiler_params=pltpu.CompilerParams(vmem_limit_bytes=VMEM_LIMIT),
        name="moe_route",
    )(idx.reshape(k, n_chunks, LANES))
    return pos.reshape(k * rows), te[0, :n_tiles], nu[0, :1]


DISPATCH_ROWS = 512


def _dispatch_kernel(pos_ref, hp_ref, init_ref, hs_ref, sem, *, n_rows):
    del init_ref
    base = pl.program_id(0) * DISPATCH_ROWS

    def row_copy(k, r):
        return pltpu.make_async_copy(hp_ref.at[pl.ds(r, 1)], hs_ref.at[pl.ds(pos_ref[k * n_rows + base + r], 1)], sem)

    def start(r, carry):
        row_copy(0, r).start()
        row_copy(1, r).start()
        return carry

    def wait(r, carry):
        row_copy(0, r).wait()
        row_copy(1, r).wait()
        return carry

    lax.fori_loop(0, DISPATCH_ROWS, start, 0, unroll=8)
    lax.fori_loop(0, DISPATCH_ROWS, wait, 0, unroll=8)


def _dispatch(pos, hp, init):
    rows, half = hp.shape
    assert init.shape[1] == half and init.dtype == hp.dtype
    return pl.pallas_call(
        functools.partial(_dispatch_kernel, n_rows=rows),
        grid_spec=pltpu.PrefetchScalarGridSpec(
            num_scalar_prefetch=1,
            grid=(rows // DISPATCH_ROWS,),
            in_specs=[
                pl.BlockSpec((DISPATCH_ROWS, half), lambda i, pos: (i, 0)),
                pl.BlockSpec(memory_space=pl.ANY),
            ],
            out_specs=pl.BlockSpec(memory_space=pl.ANY),
            scratch_shapes=[pltpu.SemaphoreType.DMA],
        ),
        out_shape=jax.ShapeDtypeStruct(init.shape, hp.dtype),
        input_output_aliases={2: 0},
        compiler_params=_cparams(("arbitrary",)),
        name="moe_dispatch",
    )(pos, hp, init)


def _expert_kernel(te_ref, nused_ref, x_ref, wg_ref, wu_ref, wd_ref, o_ref, xs_scr, act_scr):
    i = pl.program_id(0)

    @pl.when(i >= nused_ref[0])
    def _():
        o_ref[...] = jnp.zeros_like(o_ref)

    @pl.when(i < nused_ref[0])
    def _():
        def rows_body(r, carry):
            rs = pl.ds(pl.multiple_of(r * ROW_CHUNK, ROW_CHUNK), ROW_CHUNK)
            xs_scr[rs, :] = _unpack_bf16_pairs(x_ref[rs, :]).astype(BF16)
            return carry

        lax.fori_loop(0, x_ref.shape[0] // ROW_CHUNK, rows_body, 0)

        xs = xs_scr[...]
        for c in range(act_scr.shape[1] // PAIR_BLOCK):
            cs = slice(c * PAIR_BLOCK, (c + 1) * PAIR_BLOCK)
            gate = jnp.dot(xs, wg_ref[:, cs], preferred_element_type=F32)
            up = jnp.dot(xs, wu_ref[:, cs], preferred_element_type=F32)
            act_scr[:, cs] = (gate * jax.nn.sigmoid(gate) * up).astype(BF16)
        act = act_scr[...]
        for c in range(wd_ref.shape[1] // PAIR_BLOCK):
            y = jnp.dot(act, wd_ref[:, c * PAIR_BLOCK:(c + 1) * PAIR_BLOCK], preferred_element_type=F32)
            o_ref[:, c * LANES:(c + 1) * LANES] = _pack_bf16_pairs(y)


def _experts(tile_expert, n_used, hs, wg, wu, wd, layer, tm):
    p, half = hs.shape
    _, e, d, ff = wg.shape
    assert ff % PAIR_BLOCK == 0
    n_tiles = p // tm

    def row(i, te, nu):
        return (jnp.minimum(i, nu[0] - 1), 0)

    def wspec(shape, buffers):
        return pl.BlockSpec((None, None) + shape, lambda i, te, nu: (layer, te[i], 0, 0),
                            pipeline_mode=pl.Buffered(buffers))

    return pl.pallas_call(
        _expert_kernel,
        grid_spec=pltpu.PrefetchScalarGridSpec(
            num_scalar_prefetch=2,
            grid=(n_tiles,),
            in_specs=[pl.BlockSpec((tm, half), row), wspec((d, ff), 1), wspec((d, ff), 2), wspec((ff, d), 2)],
            out_specs=pl.BlockSpec((tm, half), lambda i, te, nu: (i, 0)),
            scratch_shapes=[pltpu.VMEM((tm, d), BF16), pltpu.VMEM((tm, ff), BF16)],
        ),
        out_shape=jax.ShapeDtypeStruct((p, half), jnp.uint32),
        compiler_params=_cparams(("arbitrary",)),
        name="moe_experts",
    )(tile_expert, n_used, hs, wg, wu, wd)


COMBINE_ROWS = 256
COMBINE_CHUNK = 8


def _combine_kernel(pos_ref, x_ref, wt_ref, gt_ref, ys_ref, o_ref, buf, sems, *, seg_args, n_rows):
    i = pl.program_id(0)
    slot = i % 2
    seg = _seg_of_block(i, *seg_args)

    def row_copy(blk, sl, k, r):
        return pltpu.make_async_copy(ys_ref.at[pl.ds(pos_ref[k * n_rows + blk * COMBINE_ROWS + r], 1)],
                                     buf.at[sl, k, pl.ds(r, 1)], sems.at[sl])

    def issue_rows(blk, sl, r0):
        for rr in range(COMBINE_CHUNK):
            row_copy(blk, sl, 0, r0 + rr).start()
            row_copy(blk, sl, 1, r0 + rr).start()

    @pl.when(i == 0)
    def _():
        def body(c, carry):
            issue_rows(0, 0, c * COMBINE_CHUNK)
            return carry

        lax.fori_loop(0, COMBINE_ROWS // COMBINE_CHUNK, body, 0)

    def wait(r, carry):
        row_copy(i, slot, 0, r).wait()
        row_copy(i, slot, 1, r).wait()
        return carry

    lax.fori_loop(0, COMBINE_ROWS, wait, 0, unroll=8)

    last = pl.num_programs(0) - 1
    nxt = jnp.minimum(i + 1, last)
    gt = gt_ref[pl.ds(seg, 1), :]

    def body(c, carry):
        r0 = pl.multiple_of(c * COMBINE_CHUNK, COMBINE_CHUNK)
        issue_rows(nxt, 1 - slot, r0)
        rs = pl.ds(r0, COMBINE_CHUNK)
        y0 = _unpack_bf16_pairs(buf[slot, 0, rs, :])
        y1 = _unpack_bf16_pairs(buf[slot, 1, rs, :])
        w = wt_ref[rs, :]
        o_ref[rs, :] = x_ref[rs, :] + gt * (w[:, 0:1] * y0 + w[:, 1:2] * y1)
        return carry

    lax.fori_loop(0, COMBINE_ROWS // COMBINE_CHUNK, body, 0)

    @pl.when(i == last)
    def _():
        def drain(r, carry):
            row_copy(last, 1 - slot, 0, r).wait()
            row_copy(last, 1 - slot, 1, r).wait()
            return carry

        lax.fori_loop(0, COMBINE_ROWS, drain, 0, unroll=8)


def _combine(pos, xs, wts_t, mod, layer, ys, rows, seg_args):
    d = xs.shape[1]
    half = d // 2
    return pl.pallas_call(
        functools.partial(_combine_kernel, seg_args=seg_args, n_rows=rows),
        grid_spec=pltpu.PrefetchScalarGridSpec(
            num_scalar_prefetch=1,
            grid=(rows // COMBINE_ROWS,),
            in_specs=[
                pl.BlockSpec((COMBINE_ROWS, d), lambda i, pos: (i, 0)),
                pl.BlockSpec((COMBINE_ROWS, 2), lambda i, pos: (i, 0)),
                pl.BlockSpec((None, MOD_ROWS, d), lambda i, pos: (layer, 0, 5)),
                pl.BlockSpec(memory_space=pl.ANY),
            ],
            out_specs=pl.BlockSpec((COMBINE_ROWS, d), lambda i, pos: (i, 0)),
            scratch_shapes=[pltpu.VMEM((2, 2, COMBINE_ROWS, half), jnp.uint32), pltpu.SemaphoreType.DMA((2,))],
        ),
        out_shape=jax.ShapeDtypeStruct((rows, d), F32),
        compiler_params=_cparams(("arbitrary",)),
        name="moe_combine",
    )(pos, xs, wts_t, mod, ys)


def kernel(x, c, ctx, c_ctx, w_ada, b_ada, g_mix, g_ffn, w_in, b_in, g_q, g_k, sink, conv_w, conv_b, g_mh,
           w_br_attn, w_br_mlstm, w_out, w_router, b_router, w_gate, w_up, w_down):
    n_batch, lat_len, d = x.shape
    ctx_len = ctx.shape[1]
    depth = w_ada.shape[0]
    d_in = w_in.shape[2]
    aw = w_br_attn.shape[1]
    mw = w_br_mlstm.shape[1]
    mqk = conv_w.shape[2] // 2
    n_mh = mw // MLSTM_V_DIM
    kvw = (d_in - aw - 2 * mqk - 2 * mw - 4 * n_mh - 2 * d) // 2
    n_experts = w_router.shape[1]
    assert n_batch + 1 <= MOD_ROWS and 2 * n_mh <= LANES

    n_lat_rows = n_batch * lat_len
    n_ctx_rows = n_batch * ctx_len
    tm = _pick(n_ctx_rows, (512, 256))
    assert lat_len % tm == 0
    seg_args = (n_lat_rows // tm, lat_len // tm, n_batch)
    tm_e = 256

    o_aq = 0
    o_ak = o_aq + aw
    o_av = o_ak + kvw
    o_mq = o_av + kvw
    o_mk = o_mq + mqk
    o_mv = o_mk + mqk
    o_mo = o_mv + mw
    o_g = o_mo + mw
    o_ga = o_g + 4 * n_mh
    o_gm = o_ga + d
    order = [(0, o_g), (o_ga, 2 * d)]
    n_ak, n_av, n_mq, n_mv, n_mo, n_ga, n_gm = o_ak, o_av, o_mq, o_mv, o_mo, o_g, o_g + d

    xs = jnp.concatenate([x.reshape(n_lat_rows, d), ctx.reshape(n_ctx_rows, d)], axis=0)
    cvec = jnp.zeros((MOD_ROWS, d), F32).at[:n_batch].set(c).at[n_batch].set(c_ctx)
    mod = _adaln(cvec, w_ada, b_ada)
    tabs = _rope_tables(lat_len)
    w_router_t = w_router.T

    w_main = _wprep(w_in, o_g, 4 * n_mh)
    b_main = jnp.concatenate([b_in[:, o:o + w] for o, w in order], axis=1).reshape(depth, 1, -1)
    wg = jnp.zeros((depth, 2, d, LANES), F32)
    bg = jnp.zeros((depth, 2, 1, LANES), F32)
    for dr in range(2):
        gsl = slice(o_g + 2 * n_mh * dr, o_g + 2 * n_mh * (dr + 1))
        wg = wg.at[:, dr, :, :2 * n_mh].set(w_in[:, :, gsl])
        bg = bg.at[:, dr, 0, :2 * n_mh].set(b_in[:, gsl])
    wg = wg.astype(BF16)
    wa_b, wm_b, wo_b = w_br_attn.astype(BF16), w_br_mlstm.astype(BF16), w_out.astype(BF16)
    wgate_b, wup_b, wdown_b = w_gate.astype(BF16), w_up.astype(BF16), w_down.astype(BF16)

    n_tiles = (TOP_K * (n_lat_rows + n_ctx_rows)) // tm_e + n_experts
    hs = None
    for l in range(depth):
        need_ctx = l < depth - 1
        rows = n_lat_rows + (n_ctx_rows if need_ctx else 0)

        proj, gates = _inproj(xs, g_mix[l].reshape(1, d), mod, l, w_main, b_main, wg, bg, tm, seg_args)
        qk = _conv(proj, conv_w[l], conv_b[l], n_mq, n_lat_rows, lat_len, ctx_len)
        hfb = _mlstm(qk, proj, gates, mqk, mw, n_mv, n_batch, lat_len, ctx_len)
        qr, kr = _rope(proj, tabs, g_q[l], g_k[l], aw, kvw, n_ak, n_lat_rows)
        attn = _attention(sink[l], qr, kr, proj, aw, kvw, n_av, n_batch, lat_len, ctx_len, need_ctx)
        u = _branch(attn, hfb, proj, g_mh[l], wa_b, wm_b, l, n_mo, n_ga, n_gm, rows, tm)
        xs = _outproj(u, wo_b, xs, mod, l, tm, seg_args)

        hp, idx, wts = _router(xs, g_ffn[l].reshape(1, d), mod, l, w_router_t, b_router, rows, ROUTER_ROWS,
                               (n_lat_rows // ROUTER_ROWS, lat_len // ROUTER_ROWS, n_batch))
        pos, tile_expert, n_used = _route(idx, n_experts, tm_e, n_tiles)
        hs = _dispatch(pos, hp, jnp.zeros((n_tiles * tm_e, d // 2), jnp.uint32) if hs is None else hs)
        ys = _experts(tile_expert, n_used, hs, wgate_b, wup_b, wdown_b, l, tm_e)
        xs = _combine(pos, xs, wts.T, mod, l, ys, rows, (n_lat_rows // COMBINE_ROWS, lat_len // COMBINE_ROWS, n_batch))

    return xs[:n_lat_rows].reshape(n_batch, lat_len, d)
```

```python
import functools
import math

import jax
import jax.numpy as jnp
from jax import lax
from jax.experimental import pallas as pl
from jax.experimental.pallas import tpu as pltpu

GRID_W = 64
HEAD_DIM = 128
WINDOW = 128
QBLK = 128
ROPE_THETA = 10000.0
ROPE_PAIRS = HEAD_DIM // 4
ATTN_SCALE = HEAD_DIM ** -0.5
MLSTM_QK_DIM = 128
MLSTM_V_DIM = 256
MLSTM_CHUNK = 128
N_GROUPS = 4
EXPERTS_PER_GROUP = 4
TOP_K = 2
EPS = 1e-6
NEG = -1e30

LANES = 128
MOD_ROWS = 8
ROW_CHUNK = 64
VMEM_LIMIT = 56 << 20

F32 = jnp.float32
BF16 = jnp.bfloat16


def _pick(n, cands):
    for c in cands:
        if n % c == 0:
            return c
    raise ValueError(f"no tile in {cands} divides {n}")


def _cparams(sem, vmem=VMEM_LIMIT):
    return pltpu.CompilerParams(dimension_semantics=sem, vmem_limit_bytes=vmem)


def _seg_of_block(i, n_lat_blocks, blocks_per_batch, n_batch):
    return jnp.where(i < n_lat_blocks, i // blocks_per_batch, n_batch)


def _modulated(x, g, sc, sh):
    ms = jnp.mean(x * x, axis=-1, keepdims=True)
    y = x * lax.rsqrt(ms + EPS) * g
    return y * (1.0 + sc) + sh


def _adaln_kernel(c_ref, w_ref, b_ref, o_ref):
    c = c_ref[...]
    cs = (c * jax.nn.sigmoid(c)).astype(BF16)
    o_ref[...] = jnp.dot(cs, w_ref[...].astype(BF16), preferred_element_type=F32) + b_ref[...]


def _adaln(cvec, w_ada, b_ada):
    depth, d, n6 = w_ada.shape
    tn = _pick(n6, (512, 256, 128))
    return pl.pallas_call(
        _adaln_kernel,
        grid=(depth, n6 // tn),
        in_specs=[
            pl.BlockSpec((MOD_ROWS, d), lambda l, j: (0, 0)),
            pl.BlockSpec((None, d, tn), lambda l, j: (l, 0, j)),
            pl.BlockSpec((None, 1, tn), lambda l, j: (l, 0, j)),
        ],
        out_specs=pl.BlockSpec((None, MOD_ROWS, tn), lambda l, j: (l, 0, j)),
        out_shape=jax.ShapeDtypeStruct((depth, MOD_ROWS, n6), F32),
        compiler_params=_cparams(("arbitrary", "arbitrary")),
        name="adaln",
    )(cvec, w_ada, b_ada.reshape(depth, 1, n6))


def _wprep_kernel(a_ref, b_ref, o_ref, *, first_shifted, shift):
    j = pl.program_id(2)

    @pl.when(j < first_shifted)
    def _():
        o_ref[...] = a_ref[...].T.astype(o_ref.dtype)

    @pl.when(j >= first_shifted)
    def _():
        rows = jnp.concatenate([a_ref[shift:, :], b_ref[:shift, :]], axis=0)
        o_ref[...] = rows.T.astype(o_ref.dtype)


def _wprep(w_in, cut_start, cut_width):
    depth, d, d_in = w_in.shape
    nc = d_in - cut_width
    tw = _pick(math.gcd(cut_start, nc), (512, 256, 128))
    tk = _pick(d, (1024, 512, 256, 128))
    assert cut_width % 8 == 0 and cut_width <= LANES and tw % LANES == 0
    w_t = jnp.swapaxes(w_in, 1, 2)
    nxt = tw // LANES
    return pl.pallas_call(
        functools.partial(_wprep_kernel, first_shifted=cut_start // tw, shift=cut_width),
        grid=(depth, d // tk, nc // tw),
        in_specs=[
            pl.BlockSpec((None, tw, tk), lambda l, i, j: (l, j, i)),
            pl.BlockSpec((None, LANES, tk), lambda l, i, j: (l, (j + 1) * nxt, i)),
        ],
        out_specs=pl.BlockSpec((None, tk, tw), lambda l, i, j: (l, i, j)),
        out_shape=jax.ShapeDtypeStruct((depth, d, nc), BF16),
        compiler_params=_cparams(("arbitrary", "arbitrary", "arbitrary")),
        name="w_in_relayout",
    )(w_t, w_t)


INPROJ_TILES = (1024, 512, 256, 128)


def _inproj_kernel(x_ref, g_ref, sh_ref, sc_ref, w_ref, b_ref, wg_ref, bg_ref, o_ref, og_ref, h_scr, *, seg_args):
    i = pl.program_id(0)
    j = pl.program_id(1)

    @pl.when(j == 0)
    def _():
        seg = _seg_of_block(i, *seg_args)
        g = g_ref[...]
        sc = sc_ref[pl.ds(seg, 1), :]
        sh = sh_ref[pl.ds(seg, 1), :]

        def rows_body(r, carry):
            rs = pl.ds(pl.multiple_of(r * ROW_CHUNK, ROW_CHUNK), ROW_CHUNK)
            h_scr[rs, :] = _modulated(x_ref[rs, :], g, sc, sh).astype(BF16)
            return carry

        lax.fori_loop(0, x_ref.shape[0] // ROW_CHUNK, rows_body, 0)
        for d in range(2):
            og_ref[d] = jnp.dot(h_scr[...], wg_ref[d], preferred_element_type=F32) + bg_ref[d]

    o_ref[...] = (jnp.dot(h_scr[...], w_ref[...], preferred_element_type=F32) + b_ref[...]).astype(o_ref.dtype)


def _inproj(xs, g, mod, layer, w_main, b_main, w_gates, b_gates, tm, seg_args):
    t, d = xs.shape
    nc = w_main.shape[2]
    tn = _pick(nc, INPROJ_TILES)
    return pl.pallas_call(
        functools.partial(_inproj_kernel, seg_args=seg_args),
        grid=(t // tm, nc // tn),
        in_specs=[
            pl.BlockSpec((tm, d), lambda i, j: (i, 0)),
            pl.BlockSpec((1, d), lambda i, j: (0, 0)),
            pl.BlockSpec((None, MOD_ROWS, d), lambda i, j: (layer, 0, 0)),
            pl.BlockSpec((None, MOD_ROWS, d), lambda i, j: (layer, 0, 1)),
            pl.BlockSpec((None, d, tn), lambda i, j: (layer, 0, j)),
            pl.BlockSpec((None, 1, tn), lambda i, j: (layer, 0, j)),
            pl.BlockSpec((None, 2, d, LANES), lambda i, j: (layer, 0, 0, 0)),
            pl.BlockSpec((None, 2, 1, LANES), lambda i, j: (layer, 0, 0, 0)),
        ],
        out_specs=[
            pl.BlockSpec((tm, tn), lambda i, j: (i, j)),
            pl.BlockSpec((2, tm, LANES), lambda i, j: (0, i, 0)),
        ],
        out_shape=[
            jax.ShapeDtypeStruct((t, nc), BF16),
            jax.ShapeDtypeStruct((2, t, LANES), F32),
        ],
        scratch_shapes=[pltpu.VMEM((tm, d), BF16)],
        compiler_params=_cparams(("arbitrary", "arbitrary")),
        name="inproj",
    )(xs, g, mod, mod, w_main, b_main, w_gates, b_gates)


CONV_ROWS = 256
HALO_ROWS = 16


def _conv_kernel(cur_ref, prev_ref, next_ref, w_ref, b_ref, o_ref, *, n_lat_rows, lat_len, ctx_len, k_col_block):
    i = pl.program_id(0)
    j = pl.program_id(1)
    row0 = i * CONV_ROWS
    in_lat = row0 < n_lat_rows
    seg_len = jnp.where(in_lat, lat_len, ctx_len)
    off = jnp.where(in_lat, row0, row0 - n_lat_rows) % seg_len
    has_prev = (off != 0).astype(F32)
    has_next = (off + CONV_ROWS != seg_len).astype(F32)

    x = cur_ref[...].astype(F32)
    prev_row = prev_ref[HALO_ROWS - 1:HALO_ROWS, :].astype(F32) * has_prev
    next_row = next_ref[0:1, :].astype(F32) * has_next
    rows = lax.broadcasted_iota(jnp.int32, x.shape, 0)
    xm1 = jnp.where(rows == 0, prev_row, pltpu.roll(x, 1, 0))
    xp1 = jnp.where(rows == CONV_ROWS - 1, next_row, pltpu.roll(x, CONV_ROWS - 1, 0))
    w = w_ref[...]
    y = w[0:1, :] * xm1 + w[1:2, :] * x + w[2:3, :] * xp1 + b_ref[...]
    y = y * jax.nn.sigmoid(y)
    scale = jnp.where(j >= k_col_block, MLSTM_QK_DIM ** -0.5, 1.0).astype(F32)
    o_ref[...] = (y * scale).astype(o_ref.dtype)


def _conv(proj, conv_w, conv_b, qk_off, n_lat_rows, lat_len, ctx_len):
    t = proj.shape[0]
    width = conv_w.shape[1]
    tc = _pick(width // 2, (1024, 512, 256, 128))
    assert qk_off % tc == 0 and lat_len % CONV_ROWS == 0 and ctx_len % CONV_ROWS == 0
    cb = qk_off // tc
    halo_per_blk = CONV_ROWS // HALO_ROWS
    n_halo = t // HALO_ROWS
    return pl.pallas_call(
        functools.partial(_conv_kernel, n_lat_rows=n_lat_rows, lat_len=lat_len, ctx_len=ctx_len,
                          k_col_block=(width // 2) // tc),
        grid=(t // CONV_ROWS, width // tc),
        in_specs=[
            pl.BlockSpec((CONV_ROWS, tc), lambda i, j: (i, cb + j)),
            pl.BlockSpec((HALO_ROWS, tc), lambda i, j: (jnp.maximum(i * halo_per_blk - 1, 0), cb + j)),
            pl.BlockSpec((HALO_ROWS, tc), lambda i, j: (jnp.minimum((i + 1) * halo_per_blk, n_halo - 1), cb + j)),
            pl.BlockSpec((3, tc), lambda i, j: (0, j)),
            pl.BlockSpec((1, tc), lambda i, j: (0, j)),
        ],
        out_specs=pl.BlockSpec((CONV_ROWS, tc), lambda i, j: (i, j)),
        out_shape=jax.ShapeDtypeStruct((t, width), BF16),
        compiler_params=_cparams(("arbitrary", "arbitrary")),
        name="qk_conv",
    )(proj, proj, proj, conv_w, conv_b.reshape(1, width))


ROPE_ROWS = 512
ROPE_HEAD_GROUP = 4


def _rope_kernel(q_ref, k_ref, cos_ref, s1_ref, s2_ref, gq_ref, gk_ref, qo_ref, ko_ref, *, n_q_heads, n_k_heads):
    cos = cos_ref[...]
    s1 = s1_ref[...]
    s2 = s2_ref[...]

    def prep_heads(src_ref, dst_ref, g, n_heads, scale):
        for h0 in range(0, n_heads, ROPE_HEAD_GROUP):
            sls = [slice(h * HEAD_DIM, (h + 1) * HEAD_DIM) for h in range(h0, min(h0 + ROPE_HEAD_GROUP, n_heads))]
            xs = [src_ref[:, sl].astype(F32) for sl in sls]
            inv = [lax.rsqrt(jnp.mean(x * x, axis=-1, keepdims=True) + EPS) for x in xs]
            xn = [x * r * g for x, r in zip(xs, inv)]
            up = [pltpu.roll(x, HEAD_DIM - ROPE_PAIRS, 1) for x in xn]
            dn = [pltpu.roll(x, ROPE_PAIRS, 1) for x in xn]
            for sl, x, u, dwn in zip(sls, xn, up, dn):
                y = x * cos + u * s1 + dwn * s2
                dst_ref[:, sl] = (y * scale if scale != 1.0 else y).astype(dst_ref.dtype)

    prep_heads(q_ref, qo_ref, gq_ref[...], n_q_heads, ATTN_SCALE)
    prep_heads(k_ref, ko_ref, gk_ref[...], n_k_heads, 1.0)


def _rope(proj, tabs, g_q, g_k, aw, kvw, k_off, n_lat_rows):
    t = proj.shape[0]
    assert k_off % kvw == 0
    kb = k_off // kvw
    n_lat_blk = n_lat_rows // ROPE_ROWS
    lat_blk_per_batch = (tabs[0].shape[0] - ROPE_ROWS) // ROPE_ROWS

    def tab_map(i):
        return (jnp.where(i < n_lat_blk, i % lat_blk_per_batch, lat_blk_per_batch), 0)

    tab_spec = pl.BlockSpec((ROPE_ROWS, HEAD_DIM), tab_map)
    return pl.pallas_call(
        functools.partial(_rope_kernel, n_q_heads=aw // HEAD_DIM, n_k_heads=kvw // HEAD_DIM),
        grid=(t // ROPE_ROWS,),
        in_specs=[
            pl.BlockSpec((ROPE_ROWS, aw), lambda i: (i, 0)),
            pl.BlockSpec((ROPE_ROWS, kvw), lambda i: (i, kb)),
            tab_spec, tab_spec, tab_spec,
            pl.BlockSpec((1, HEAD_DIM), lambda i: (0, 0)),
            pl.BlockSpec((1, HEAD_DIM), lambda i: (0, 0)),
        ],
        out_specs=[
            pl.BlockSpec((ROPE_ROWS, aw), lambda i: (i, 0)),
            pl.BlockSpec((ROPE_ROWS, kvw), lambda i: (i, 0)),
        ],
        out_shape=[jax.ShapeDtypeStruct((t, aw), BF16), jax.ShapeDtypeStruct((t, kvw), BF16)],
        compiler_params=_cparams(("arbitrary",)),
        name="qk_norm_rope",
    )(proj, proj, tabs[0], tabs[1], tabs[2], g_q.reshape(1, HEAD_DIM), g_k.reshape(1, HEAD_DIM))


def _rope_tables(n_lat):
    rows = n_lat // GRID_W
    inv_freq = ROPE_THETA ** (-jnp.arange(ROPE_PAIRS, dtype=F32) / ROPE_PAIRS)
    row_pos = jnp.repeat(jnp.arange(rows, dtype=F32), GRID_W)
    col_pos = jnp.tile(jnp.arange(GRID_W, dtype=F32), rows)
    ang_r = row_pos[:, None] * inv_freq
    ang_c = col_pos[:, None] * inv_freq
    zeros = jnp.zeros_like(ang_r)
    cos = jnp.concatenate([jnp.cos(ang_r), jnp.cos(ang_r), jnp.cos(ang_c), jnp.cos(ang_c)], axis=-1)
    s1 = jnp.concatenate([-jnp.sin(ang_r), zeros, -jnp.sin(ang_c), zeros], axis=-1)
    s2 = jnp.concatenate([zeros, jnp.sin(ang_r), zeros, jnp.sin(ang_c)], axis=-1)
    ident = jnp.ones((ROPE_ROWS, HEAD_DIM), F32)
    zpad = jnp.zeros((ROPE_ROWS, HEAD_DIM), F32)
    return (jnp.concatenate([cos, ident], 0), jnp.concatenate([s1, zpad], 0), jnp.concatenate([s2, zpad], 0))


def _attn_kernel(sink_ref, q_ref, kp_ref, kc_ref, kn_ref, kx_ref, vp_ref, vc_ref, vn_ref, vx_ref, o_ref,
                 *, n_lat_blk, n_kv, group, ctx_len):
    n = pl.program_id(1)
    is_ctx = n >= n_lat_blk
    n_band = 3 * QBLK
    n_keys = n_band + ctx_len
    qi = lax.broadcasted_iota(jnp.int32, (QBLK, n_keys), 0)
    kj = lax.broadcasted_iota(jnp.int32, (QBLK, n_keys), 1)
    rel = kj - QBLK - qi
    kpos = n * QBLK + kj - QBLK
    band_ok = (jnp.abs(rel) <= WINDOW) & (kpos >= 0) & (kpos < n_lat_blk * QBLK) & jnp.logical_not(is_ctx)
    valid = band_ok | (kj >= n_band)

    for hk in range(n_kv):
        ksl = slice(hk * HEAD_DIM, (hk + 1) * HEAD_DIM)
        k_all = jnp.concatenate([kp_ref[:, ksl], kc_ref[:, ksl], kn_ref[:, ksl], kx_ref[:, ksl]], axis=0)
        v_all = jnp.concatenate([vp_ref[:, ksl], vc_ref[:, ksl], vn_ref[:, ksl], vx_ref[:, ksl]], axis=0)
        heads = [hk * group + g for g in range(group)]
        qsl = {h: slice(h * HEAD_DIM, (h + 1) * HEAD_DIM) for h in heads}
        s = {h: jnp.where(valid, lax.dot_general(q_ref[:, qsl[h]], k_all, (((1,), (1,)), ((), ())),
                                                 preferred_element_type=F32), NEG) for h in heads}
        m = {h: jnp.maximum(jnp.max(s[h], axis=-1, keepdims=True), sink_ref[h]) for h in heads}
        p = {h: jnp.exp(s[h] - m[h]) for h in heads}
        denom = {h: jnp.sum(p[h], axis=-1, keepdims=True) + jnp.exp(sink_ref[h] - m[h]) for h in heads}
        o = {h: jnp.dot(p[h].astype(BF16), v_all, preferred_element_type=F32) for h in heads}
        for h in heads:
            o_ref[:, qsl[h]] = (o[h] / denom[h]).astype(o_ref.dtype)


def _attention(sink, qr, kr, proj, aw, kvw, v_off, n_batch, lat_len, ctx_len, with_ctx):
    t = proj.shape[0]
    assert v_off % kvw == 0
    vb = v_off // kvw
    n_lat_blk = lat_len // QBLK
    n_ctx_blk = ctx_len // QBLK
    n_lat_rows = n_batch * lat_len
    nblk = n_lat_blk + (n_ctx_blk if with_ctx else 0)

    def qrow(b, n):
        return jnp.where(n < n_lat_blk, b * n_lat_blk + n, n_lat_rows // QBLK + b * n_ctx_blk + (n - n_lat_blk))

    def band(delta):
        def f(b, n):
            nn = jnp.clip(jnp.minimum(n, n_lat_blk - 1) + delta, 0, n_lat_blk - 1)
            return b * n_lat_blk + nn
        return f

    def ctx_row(b, n):
        return n_lat_rows // ctx_len + b

    def kspec(rowf):
        return pl.BlockSpec((QBLK, kvw), lambda b, n: (rowf(b, n), 0))

    def vspec(rowf):
        return pl.BlockSpec((QBLK, kvw), lambda b, n: (rowf(b, n), vb))

    return pl.pallas_call(
        functools.partial(_attn_kernel, n_lat_blk=n_lat_blk, n_kv=kvw // HEAD_DIM,
                          group=aw // kvw, ctx_len=ctx_len),
        grid=(n_batch, nblk),
        in_specs=[
            pl.BlockSpec(memory_space=pltpu.SMEM),
            pl.BlockSpec((QBLK, aw), lambda b, n: (qrow(b, n), 0)),
            kspec(band(-1)), kspec(band(0)), kspec(band(1)),
            pl.BlockSpec((ctx_len, kvw), lambda b, n: (ctx_row(b, n), 0)),
            vspec(band(-1)), vspec(band(0)), vspec(band(1)),
            pl.BlockSpec((ctx_len, kvw), lambda b, n: (ctx_row(b, n), vb)),
        ],
        out_specs=pl.BlockSpec((QBLK, aw), lambda b, n: (qrow(b, n), 0)),
        out_shape=jax.ShapeDtypeStruct((n_lat_rows + (n_batch * ctx_len if with_ctx else 0), aw), BF16),
        compiler_params=_cparams(("arbitrary", "arbitrary")),
        name="attention",
    )(sink, qr, kr, kr, kr, kr, proj, proj, proj, proj)


MLSTM_GROUP = 8


def _mlstm_kernel(*refs, n_heads, n_vblk):
    per_dir = 3 + n_vblk
    dir_ins = [refs[:per_dir], refs[per_dir:2 * per_dir]]
    outs, state = refs[2 * per_dir:2 * per_dir + 2], refs[2 * per_dir + 2:]
    heads_per_vblk = n_heads // n_vblk
    n_scans = 2 * n_heads
    c_scrs, nm_scrs = state[:n_scans], state[n_scans:]
    c = pl.program_id(1)
    L = MLSTM_CHUNK
    dk = MLSTM_QK_DIM
    dv = MLSTM_V_DIM

    @pl.when(c == 0)
    def _():
        for scr in state:
            scr[...] = jnp.zeros_like(scr)

    r = lax.broadcasted_iota(jnp.int32, (L, L), 0)
    s = lax.broadcasted_iota(jnp.int32, (L, L), 1)

    def split_dot_l(mat_b, x):
        hi = x.astype(BF16)
        lo = (x - hi.astype(F32)).astype(BF16)
        return jnp.dot(mat_b, hi, preferred_element_type=F32) + jnp.dot(mat_b, lo, preferred_element_type=F32)

    def split_dot_r(x, mat_b):
        hi = x.astype(BF16)
        lo = (x - hi.astype(F32)).astype(BF16)
        return jnp.dot(hi, mat_b, preferred_element_type=F32) + jnp.dot(lo, mat_b, preferred_element_type=F32)

    tris, gate_cols, gate_rows, cum_cols, cum_rows, end_cols = [], [], [], [], [], []
    for dirn in range(2):
        tri = (s <= r) if dirn == 0 else (s >= r)
        tri_t = (r <= s) if dirn == 0 else (r >= s)
        gates = dir_ins[dirn][2][...]
        logf = jnp.minimum(gates, 0.0) - jnp.log1p(jnp.exp(-jnp.abs(gates)))
        cum_col = split_dot_l(tri.astype(BF16), logf)
        cum_row = split_dot_r(logf.T, tri_t.astype(BF16))
        tris.append(tri)
        gate_cols.append(gates)
        gate_rows.append(gates.T)
        cum_cols.append(cum_col)
        cum_rows.append(cum_row)
        end_cols.append(cum_col[L - 1:L, :] if dirn == 0 else cum_col[0:1, :])

    nt = (((1,), (1,)), ((), ()))
    scans = [(dirn, h) for dirn in range(2) for h in range(n_heads)]
    for g0 in range(0, n_scans, MLSTM_GROUP):
        hs = scans[g0:g0 + MLSTM_GROUP]
        sid = {x: x[0] * n_heads + x[1] for x in hs}
        qs = {x: dir_ins[x[0]][0][:, x[1] * dk:(x[1] + 1) * dk] for x in hs}
        ks = {x: dir_ins[x[0]][1][:, x[1] * dk:(x[1] + 1) * dk] for x in hs}
        vs = {x: dir_ins[x[0]][3 + x[1] // heads_per_vblk][:, (x[1] % heads_per_vblk) * dv:
                                                           (x[1] % heads_per_vblk + 1) * dv] for x in hs}
        b_col = {x: cum_cols[x[0]][:, n_heads + x[1]:n_heads + x[1] + 1] for x in hs}
        b_row = {x: cum_rows[x[0]][n_heads + x[1]:n_heads + x[1] + 1, :] for x in hs}
        i_col = {x: gate_cols[x[0]][:, x[1]:x[1] + 1] for x in hs}
        i_row = {x: gate_rows[x[0]][x[1]:x[1] + 1, :] for x in hs}
        b_end = {x: end_cols[x[0]][:, n_heads + x[1]:n_heads + x[1] + 1] for x in hs}
        n_prev = {x: nm_scrs[sid[x]][0:1, :] for x in hs}
        m_prev = {x: nm_scrs[sid[x]][1:2, 0:1] for x in hs}
        ct_prev = {x: c_scrs[sid[x]][...] for x in hs}

        dmat = {h: jnp.where(tris[h[0]], b_col[h] - b_row[h] + i_row[h], NEG) for h in hs}
        m_inter = {h: b_col[h] + m_prev[h] for h in hs}
        m_t = {h: jnp.maximum(m_inter[h], jnp.max(dmat[h], axis=-1, keepdims=True)) for h in hs}
        qk = {h: lax.dot_general(qs[h], ks[h], nt, preferred_element_type=F32) for h in hs}
        qc = {h: jnp.dot(qs[h], ct_prev[h].astype(BF16), preferred_element_type=F32) for h in hs}
        qn_prev = {h: jnp.sum(qs[h].astype(F32) * n_prev[h], axis=-1, keepdims=True) for h in hs}
        a = {h: jnp.exp(m_inter[h] - m_t[h]) for h in hs}
        smat = {h: qk[h] * jnp.exp(dmat[h] - m_t[h]) for h in hs}
        sv = {h: jnp.dot(smat[h].astype(BF16), vs[h], preferred_element_type=F32) for h in hs}
        qn = {h: jnp.sum(smat[h], axis=-1, keepdims=True) + a[h] * qn_prev[h] for h in hs}

        g_row = {h: b_end[h] - b_row[h] + i_row[h] for h in hs}
        m_new = {h: jnp.maximum(b_end[h] + m_prev[h], jnp.max(g_row[h], axis=-1, keepdims=True)) for h in hs}
        kw = {h: ks[h].astype(F32) * jnp.exp(b_end[h] - b_col[h] + i_col[h] - m_new[h]) for h in hs}
        a_end = {h: jnp.exp(b_end[h] + m_prev[h] - m_new[h]) for h in hs}
        kv = {h: jnp.dot(kw[h].T.astype(BF16), vs[h], preferred_element_type=F32) for h in hs}

        for h in hs:
            hout = (sv[h] + a[h] * qc[h]) / jnp.maximum(jnp.abs(qn[h]), jnp.exp(-m_t[h]))
            outs[h[0]][:, h[1] * dv:(h[1] + 1) * dv] = hout.astype(outs[h[0]].dtype)
        for h in hs:
            c_scrs[sid[h]][...] = a_end[h] * ct_prev[h] + kv[h]
            nm_scrs[sid[h]][0:1, :] = a_end[h] * n_prev[h] + jnp.sum(kw[h], axis=0, keepdims=True)
            nm_scrs[sid[h]][1:2, :] = jnp.broadcast_to(m_new[h], (1, dk))


def _mlstm(qk, proj, gates, mqk, mw, v_off, n_batch, lat_len, ctx_len):
    t = proj.shape[0]
    L = MLSTM_CHUNK
    n_heads = mw // MLSTM_V_DIM
    assert mqk == n_heads * MLSTM_QK_DIM
    vw = math.gcd(mw, v_off)
    n_vblk = mw // vw
    assert vw % MLSTM_V_DIM == 0
    n_ctx = ctx_len // L
    n_lat = lat_len // L
    lat_blocks = n_batch * n_lat

    def row(d, b, c):
        cc = c if d == 0 else n_ctx - 1 - c
        lc = c - n_ctx if d == 0 else n_lat - 1 - (c - n_ctx)
        return jnp.where(c < n_ctx, lat_blocks + b * n_ctx + cc, b * n_lat + lc)

    def dir_specs(d):
        return [
            pl.BlockSpec((L, mqk), lambda b, c: (row(d, b, c), 0)),
            pl.BlockSpec((L, mqk), lambda b, c: (row(d, b, c), 1)),
            pl.BlockSpec((None, L, LANES), lambda b, c: (d, row(d, b, c), 0)),
        ] + [pl.BlockSpec((L, vw), lambda b, c, j=j: (row(d, b, c), v_off // vw + j)) for j in range(n_vblk)]

    dir_args = [qk, qk, gates] + [proj] * n_vblk
    return pl.pallas_call(
        functools.partial(_mlstm_kernel, n_heads=n_heads, n_vblk=n_vblk),
        grid=(n_batch, n_ctx + n_lat),
        in_specs=dir_specs(0) + dir_specs(1),
        out_specs=[pl.BlockSpec((L, mw), lambda b, c: (row(0, b, c), 0)),
                   pl.BlockSpec((L, mw), lambda b, c: (row(1, b, c), 0))],
        out_shape=[jax.ShapeDtypeStruct((t, mw), BF16)] * 2,
        scratch_shapes=([pltpu.VMEM((MLSTM_QK_DIM, MLSTM_V_DIM), F32)] * (2 * n_heads)
                        + [pltpu.VMEM((8, MLSTM_QK_DIM), F32)] * (2 * n_heads)),
        compiler_params=_cparams(("arbitrary", "arbitrary")),
        name="mlstm",
    )(*dir_args, *dir_args)


def _branch_kernel(attn_ref, hf_ref, hb_ref, gmh_ref, wa_ref, wm_ref, ga_ref, gm_ref, *rest, n_heads):
    mo_refs, o_ref, hm_scr = rest[:-2], rest[-2], rest[-1]
    heads_per_blk = n_heads // len(mo_refs)
    j = pl.program_id(1)

    @pl.when(j == 0)
    def _():
        dv = MLSTM_V_DIM
        for h in range(n_heads):
            sl = slice(h * dv, (h + 1) * dv)
            mo = mo_refs[h // heads_per_blk][:, (h % heads_per_blk) * dv:(h % heads_per_blk + 1) * dv]
            hsum = hf_ref[:, sl].astype(F32) + hb_ref[:, sl].astype(F32)
            x = jax.nn.sigmoid(mo.astype(F32)) * hsum
            y = x * lax.rsqrt(jnp.mean(x * x, axis=-1, keepdims=True) + EPS) * gmh_ref[:, sl]
            hm_scr[:, sl] = y.astype(BF16)

    ya = jnp.dot(attn_ref[...], wa_ref[...], preferred_element_type=F32)
    ym = jnp.dot(hm_scr[...], wm_ref[...], preferred_element_type=F32)
    u = jax.nn.sigmoid(ga_ref[...].astype(F32)) * ya + jax.nn.sigmoid(gm_ref[...].astype(F32)) * ym
    o_ref[...] = u.astype(o_ref.dtype)


def _branch(attn, hfb, proj, g_mh, wa, wm, layer, mo_off, ga_off, gm_off, rows, tm):
    aw = attn.shape[1]
    mw = hfb[0].shape[1]
    d = wa.shape[2]
    tn = _pick(d, (1024, 512, 256, 128))
    assert ga_off % tn == 0 and gm_off % tn == 0
    gab, gmb = ga_off // tn, gm_off // tn
    mo_w = math.gcd(mw, mo_off)
    n_mo_blk = mw // mo_w
    assert mo_w % MLSTM_V_DIM == 0
    return pl.pallas_call(
        functools.partial(_branch_kernel, n_heads=mw // MLSTM_V_DIM),
        grid=(rows // tm, d // tn),
        in_specs=[
            pl.BlockSpec((tm, aw), lambda i, j: (i, 0)),
            pl.BlockSpec((tm, mw), lambda i, j: (i, 0)),
            pl.BlockSpec((tm, mw), lambda i, j: (i, 0)),
            pl.BlockSpec((1, mw), lambda i, j: (0, 0)),
            pl.BlockSpec((None, aw, tn), lambda i, j: (layer, 0, j)),
            pl.BlockSpec((None, mw, tn), lambda i, j: (layer, 0, j)),
            pl.BlockSpec((tm, tn), lambda i, j: (i, gab + j)),
            pl.BlockSpec((tm, tn), lambda i, j: (i, gmb + j)),
        ] + [pl.BlockSpec((tm, mo_w), lambda i, j, b=b: (i, mo_off // mo_w + b)) for b in range(n_mo_blk)],
        out_specs=pl.BlockSpec((tm, tn), lambda i, j: (i, j)),
        out_shape=jax.ShapeDtypeStruct((rows, d), BF16),
        scratch_shapes=[pltpu.VMEM((tm, mw), BF16)],
        compiler_params=_cparams(("arbitrary", "arbitrary")),
        name="branch_merge",
    )(attn, hfb[0], hfb[1], g_mh.reshape(1, mw), wa, wm, proj, proj, *([proj] * n_mo_blk))


def _outproj_kernel(u_ref, w_ref, x_ref, gt_ref, o_ref, *, seg_args):
    seg = _seg_of_block(pl.program_id(0), *seg_args)
    y = jnp.dot(u_ref[...], w_ref[...], preferred_element_type=F32)
    o_ref[...] = x_ref[...] + gt_ref[pl.ds(seg, 1), :] * y


def _outproj(u, w_out, xs, mod, layer, tm, seg_args):
    rows, d = u.shape
    tn = _pick(d, (1024, 512, 256, 128))
    gate_blk = 2 * (d // tn)
    return pl.pallas_call(
        functools.partial(_outproj_kernel, seg_args=seg_args),
        grid=(rows // tm, d // tn),
        in_specs=[
            pl.BlockSpec((tm, d), lambda i, j: (i, 0)),
            pl.BlockSpec((None, d, tn), lambda i, j: (layer, 0, j)),
            pl.BlockSpec((tm, tn), lambda i, j: (i, j)),
            pl.BlockSpec((None, MOD_ROWS, tn), lambda i, j: (layer, 0, gate_blk + j)),
        ],
        out_specs=pl.BlockSpec((tm, tn), lambda i, j: (i, j)),
        out_shape=jax.ShapeDtypeStruct(xs.shape, F32),
        input_output_aliases={2: 0},
        compiler_params=_cparams(("arbitrary", "arbitrary")),
        name="outproj_residual",
    )(u, w_out, xs, mod)


PAIR_BLOCK = 2 * LANES


def _pack_bf16_pairs(h):
    blocks = []
    for b in range(h.shape[1] // PAIR_BLOCK):
        hi = pltpu.bitcast(h[:, b * PAIR_BLOCK:b * PAIR_BLOCK + LANES].astype(BF16).astype(F32), jnp.uint32)
        lo = pltpu.bitcast(h[:, b * PAIR_BLOCK + LANES:(b + 1) * PAIR_BLOCK].astype(BF16).astype(F32), jnp.uint32)
        blocks.append(hi | (lo >> 16))
    return blocks[0] if len(blocks) == 1 else jnp.concatenate(blocks, axis=1)


def _unpack_bf16_pairs(p):
    blocks = []
    for b in range(p.shape[1] // LANES):
        w = p[:, b * LANES:(b + 1) * LANES]
        blocks.append(pltpu.bitcast(w & jnp.uint32(0xFFFF0000), F32))
        blocks.append(pltpu.bitcast(w << 16, F32))
    return jnp.concatenate(blocks, axis=1)


ROUTER_ROWS = 256


def _router_kernel(x_ref, g_ref, sh_ref, sc_ref, wr_ref, br_ref, hp_ref, idx_ref, wt_ref, *, seg_args, n_experts):
    seg = _seg_of_block(pl.program_id(0), *seg_args)
    h = _modulated(x_ref[...], g_ref[...], sc_ref[pl.ds(seg, 1), :], sh_ref[pl.ds(seg, 1), :])
    hp_ref[...] = _pack_bf16_pairs(h)

    wr = wr_ref[...]
    h_hi = h.astype(BF16)
    h_lo = (h - h_hi.astype(F32)).astype(BF16)
    w_hi = wr.astype(BF16)
    w_lo = (wr - w_hi.astype(F32)).astype(BF16)
    nt = (((1,), (1,)), ((), ()))
    logits = (lax.dot_general(w_hi, h_hi, nt, preferred_element_type=F32)
              + lax.dot_general(w_hi, h_lo, nt, preferred_element_type=F32)
              + lax.dot_general(w_lo, h_hi, nt, preferred_element_type=F32))
    aff = jax.nn.sigmoid(logits)
    biased = aff + br_ref[...]
    rb = [biased[e:e + 1, :] for e in range(n_experts)]
    ra = [aff[e:e + 1, :] for e in range(n_experts)]

    epg = EXPERTS_PER_GROUP
    scores = []
    for g in range(N_GROUPS):
        a, b, c, d = rb[epg * g:epg * g + epg]
        hi1, lo1 = jnp.maximum(a, b), jnp.minimum(a, b)
        hi2, lo2 = jnp.maximum(c, d), jnp.minimum(c, d)
        scores.append(jnp.maximum(hi1, hi2) + jnp.maximum(jnp.minimum(hi1, hi2), jnp.maximum(lo1, lo2)))
    best = jnp.zeros(scores[0].shape, jnp.int32)
    best_s = scores[0]
    for g in range(1, N_GROUPS):
        upd = scores[g] > best_s
        best = jnp.where(upd, g, best)
        best_s = jnp.where(upd, scores[g], best_s)

    vb, va = [], []
    for j in range(epg):
        xb, xa = rb[j], ra[j]
        for g in range(1, N_GROUPS):
            sel = best == g
            xb = jnp.where(sel, rb[epg * g + j], xb)
            xa = jnp.where(sel, ra[epg * g + j], xa)
        vb.append(xb)
        va.append(xa)

    i1 = jnp.zeros_like(best)
    m1, a1 = vb[0], va[0]
    for j in range(1, epg):
        upd = vb[j] > m1
        i1 = jnp.where(upd, j, i1)
        m1 = jnp.where(upd, vb[j], m1)
        a1 = jnp.where(upd, va[j], a1)
    i2 = jnp.zeros_like(best)
    m2 = jnp.full_like(m1, -jnp.inf)
    a2 = jnp.zeros_like(a1)
    for j in range(epg):
        upd = (i1 != j) & (vb[j] > m2)
        i2 = jnp.where(upd, j, i2)
        m2 = jnp.where(upd, vb[j], m2)
        a2 = jnp.where(upd, va[j], a2)

    idx_ref[0:1, :] = best * epg + i1
    idx_ref[1:2, :] = best * epg + i2
    tot = a1 + a2
    wt_ref[0:1, :] = a1 / tot
    wt_ref[1:2, :] = a2 / tot


def _router(xs, g, mod, layer, w_router_t, b_router, rows, tm, seg_args):
    d = xs.shape[1]
    e = w_router_t.shape[0]
    assert e == N_GROUPS * EXPERTS_PER_GROUP
    return pl.pallas_call(
        functools.partial(_router_kernel, seg_args=seg_args, n_experts=e),
        grid=(rows // tm,),
        in_specs=[
            pl.BlockSpec((tm, d), lambda i: (i, 0)),
            pl.BlockSpec((1, d), lambda i: (0, 0)),
            pl.BlockSpec((None, MOD_ROWS, d), lambda i: (layer, 0, 3)),
            pl.BlockSpec((None, MOD_ROWS, d), lambda i: (layer, 0, 4)),
            pl.BlockSpec((e, d), lambda i: (0, 0)),
            pl.BlockSpec((e, 1), lambda i: (0, 0)),
        ],
        out_specs=[
            pl.BlockSpec((tm, d // 2), lambda i: (i, 0)),
            pl.BlockSpec((2, tm), lambda i: (0, i)),
            pl.BlockSpec((2, tm), lambda i: (0, i)),
        ],
        out_shape=[
            jax.ShapeDtypeStruct((rows, d // 2), jnp.uint32),
            jax.ShapeDtypeStruct((2, rows), jnp.int32),
            jax.ShapeDtypeStruct((2, rows), F32),
        ],
        compiler_params=_cparams(("arbitrary",)),
        name="ffn_modulate_route",
    )(xs, g, mod, mod, w_router_t, b_router.reshape(e, 1))


def _route_kernel(idx_ref, pos_ref, te_ref, nu_ref, *, n_experts, tm, n_chunks):
    e_iota = lax.broadcasted_iota(jnp.int32, (n_experts, LANES), 0)
    idx_all = idx_ref[...]

    def count_col(k):
        col = jnp.zeros((n_experts, LANES), F32)
        for e in range(n_experts):
            col = jnp.where(e_iota == e, jnp.sum((idx_all[k] == e).astype(F32)), col)
        return col

    c0 = count_col(0)
    counts = c0 + count_col(1)
    tiles_per = jnp.floor((counts + (tm - 1)) * (1.0 / tm))
    tile_end = tiles_per
    s = 1
    while s < n_experts:
        tile_end = tile_end + jnp.where(e_iota >= s, pltpu.roll(tile_end, s, 0), 0.0)
        s *= 2
    row_off = (tile_end - tiles_per) * tm

    r = lax.broadcasted_iota(jnp.int32, (LANES, LANES), 0)
    c = lax.broadcasted_iota(jnp.int32, (LANES, LANES), 1)
    triu = (r <= c).astype(BF16)

    def body(ch, carry):
        new = []
        for k in range(2):
            onehot = (e_iota == idx_ref[k, pl.ds(ch, 1), :]).astype(F32)
            csum = jnp.dot(onehot.astype(BF16), triu, preferred_element_type=F32)
            posv = jnp.sum(onehot * (row_off + carry[k] + csum - 1.0), axis=0, keepdims=True)
            pos_ref[k, pl.ds(ch, 1), :] = posv.astype(jnp.int32)
            new.append(carry[k] + csum[:, LANES - 1:LANES])
        return tuple(new)

    lax.fori_loop(0, n_chunks, body, (jnp.zeros((n_experts, LANES), F32), c0))

    t_iota = lax.broadcasted_iota(jnp.int32, (n_experts, te_ref.shape[1]), 1).astype(F32)
    te = jnp.sum((tile_end[:, 0:1] <= t_iota).astype(F32), axis=0, keepdims=True)
    te_ref[...] = jnp.minimum(te, n_experts - 1.0).astype(jnp.int32)
    nu_ref[...] = tile_end[n_experts - 1:n_experts, :].astype(jnp.int32)


def _route(idx, n_experts, tm, n_tiles):
    k, rows = idx.shape
    assert k == 2 and rows % LANES == 0 and tm & (tm - 1) == 0
    assert n_tiles >= (k * rows) // tm + n_experts
    n_chunks = rows // LANES
    te_width = -(-n_tiles // LANES) * LANES
    pos, te, nu = pl.pallas_call(
        functools.partial(_route_kernel, n_experts=n_experts, tm=tm, n_chunks=n_chunks),
        out_shape=[
            jax.ShapeDtypeStruct((k, n_chunks, LANES), jnp.int32),
            jax.ShapeDtypeStruct((1, te_width), jnp.int32),
            jax.ShapeDtypeStruct((1, LANES), jnp.int32),
        ],
        compiler_params=pltpu.CompilerParams(vmem_limit_bytes=VMEM_LIMIT),
        name="moe_route",
    )(idx.reshape(k, n_chunks, LANES))
    return pos.reshape(k * rows), te[0, :n_tiles], nu[0, :1]


DISPATCH_ROWS = 512


def _dispatch_kernel(pos_ref, hp_ref, init_ref, hs_ref, sem, *, n_rows):
    del init_ref
    base = pl.program_id(0) * DISPATCH_ROWS

    def row_copy(k, r):
        return pltpu.make_async_copy(hp_ref.at[pl.ds(r, 1)], hs_ref.at[pl.ds(pos_ref[k * n_rows + base + r], 1)], sem)

    def start(r, carry):
        row_copy(0, r).start(priority=0)
        row_copy(1, r).start(priority=1)
        return carry

    def wait(r, carry):
        row_copy(0, r).wait()
        row_copy(1, r).wait()
        return carry

    lax.fori_loop(0, DISPATCH_ROWS, start, 0, unroll=8)
    lax.fori_loop(0, DISPATCH_ROWS, wait, 0, unroll=8)


def _dispatch(pos, hp, init):
    rows, half = hp.shape
    assert init.shape[1] == half and init.dtype == hp.dtype
    return pl.pallas_call(
        functools.partial(_dispatch_kernel, n_rows=rows),
        grid_spec=pltpu.PrefetchScalarGridSpec(
            num_scalar_prefetch=1,
            grid=(rows // DISPATCH_ROWS,),
            in_specs=[
                pl.BlockSpec((DISPATCH_ROWS, half), lambda i, pos: (i, 0)),
                pl.BlockSpec(memory_space=pl.ANY),
            ],
            out_specs=pl.BlockSpec(memory_space=pl.ANY),
            scratch_shapes=[pltpu.SemaphoreType.DMA],
        ),
        out_shape=jax.ShapeDtypeStruct(init.shape, hp.dtype),
        input_output_aliases={2: 0},
        compiler_params=_cparams(("arbitrary",)),
        name="moe_dispatch",
    )(pos, hp, init)


def _expert_kernel(te_ref, nused_ref, x_ref, wg_ref, wu_ref, wd_ref, o_ref, xs_scr, act_scr):
    i = pl.program_id(0)

    @pl.when(i >= nused_ref[0])
    def _():
        o_ref[...] = jnp.zeros_like(o_ref)

    @pl.when(i < nused_ref[0])
    def _():
        def rows_body(r, carry):
            rs = pl.ds(pl.multiple_of(r * ROW_CHUNK, ROW_CHUNK), ROW_CHUNK)
            xs_scr[rs, :] = _unpack_bf16_pairs(x_ref[rs, :]).astype(BF16)
            return carry

        lax.fori_loop(0, x_ref.shape[0] // ROW_CHUNK, rows_body, 0)

        xs = xs_scr[...]
        for c in range(act_scr.shape[1] // PAIR_BLOCK):
            cs = slice(c * PAIR_BLOCK, (c + 1) * PAIR_BLOCK)
            gate = jnp.dot(xs, wg_ref[:, cs], preferred_element_type=F32)
            up = jnp.dot(xs, wu_ref[:, cs], preferred_element_type=F32)
            act_scr[:, cs] = (gate * jax.nn.sigmoid(gate) * up).astype(BF16)
        act = act_scr[...]
        for c in range(wd_ref.shape[1] // PAIR_BLOCK):
            y = jnp.dot(act, wd_ref[:, c * PAIR_BLOCK:(c + 1) * PAIR_BLOCK], preferred_element_type=F32)
            o_ref[:, c * LANES:(c + 1) * LANES] = _pack_bf16_pairs(y)


def _experts(tile_expert, n_used, hs, wg, wu, wd, layer, tm):
    p, half = hs.shape
    _, e, d, ff = wg.shape
    assert ff % PAIR_BLOCK == 0
    n_tiles = p // tm

    def row(i, te, nu):
        return (jnp.minimum(i, nu[0] - 1), 0)

    def wspec(shape, buffers):
        return pl.BlockSpec((None, None) + shape, lambda i, te, nu: (layer, te[i], 0, 0),
                            pipeline_mode=pl.Buffered(buffers))

    return pl.pallas_call(
        _expert_kernel,
        grid_spec=pltpu.PrefetchScalarGridSpec(
            num_scalar_prefetch=2,
            grid=(n_tiles,),
            in_specs=[pl.BlockSpec((tm, half), row), wspec((d, ff), 1), wspec((d, ff), 2), wspec((ff, d), 2)],
            out_specs=pl.BlockSpec((tm, half), lambda i, te, nu: (i, 0)),
            scratch_shapes=[pltpu.VMEM((tm, d), BF16), pltpu.VMEM((tm, ff), BF16)],
        ),
        out_shape=jax.ShapeDtypeStruct((p, half), jnp.uint32),
        compiler_params=_cparams(("arbitrary",)),
        name="moe_experts",
    )(tile_expert, n_used, hs, wg, wu, wd)


COMBINE_ROWS = 256
COMBINE_CHUNK = 8


def _combine_kernel(pos_ref, x_ref, wt_ref, gt_ref, ys_ref, o_ref, buf, sems, *, seg_args, n_rows):
    i = pl.program_id(0)
    slot = i % 2
    seg = _seg_of_block(i, *seg_args)

    def row_copy(blk, sl, k, r):
        return pltpu.make_async_copy(ys_ref.at[pl.ds(pos_ref[k * n_rows + blk * COMBINE_ROWS + r], 1)],
                                     buf.at[sl, k, pl.ds(r, 1)], sems.at[sl])

    def issue_rows(blk, sl, r0):
        for rr in range(COMBINE_CHUNK):
            row_copy(blk, sl, 0, r0 + rr).start()
            row_copy(blk, sl, 1, r0 + rr).start()

    @pl.when(i == 0)
    def _():
        def body(c, carry):
            issue_rows(0, 0, c * COMBINE_CHUNK)
            return carry

        lax.fori_loop(0, COMBINE_ROWS // COMBINE_CHUNK, body, 0)

    def wait(r, carry):
        row_copy(i, slot, 0, r).wait()
        row_copy(i, slot, 1, r).wait()
        return carry

    lax.fori_loop(0, COMBINE_ROWS, wait, 0, unroll=8)

    last = pl.num_programs(0) - 1
    nxt = jnp.minimum(i + 1, last)
    gt = gt_ref[pl.ds(seg, 1), :]

    def body(c, carry):
        r0 = pl.multiple_of(c * COMBINE_CHUNK, COMBINE_CHUNK)
        issue_rows(nxt, 1 - slot, r0)
        rs = pl.ds(r0, COMBINE_CHUNK)
        y0 = _unpack_bf16_pairs(buf[slot, 0, rs, :])
        y1 = _unpack_bf16_pairs(buf[slot, 1, rs, :])
        w = wt_ref[rs, :]
        o_ref[rs, :] = x_ref[rs, :] + gt * (w[:, 0:1] * y0 + w[:, 1:2] * y1)
        return carry

    lax.fori_loop(0, COMBINE_ROWS // COMBINE_CHUNK, body, 0)

    @pl.when(i == last)
    def _():
        def drain(r, carry):
            row_copy(last, 1 - slot, 0, r).wait()
            row_copy(last, 1 - slot, 1, r).wait()
            return carry

        lax.fori_loop(0, COMBINE_ROWS, drain, 0, unroll=8)


def _combine(pos, xs, wts_t, mod, layer, ys, rows, seg_args):
    d = xs.shape[1]
    half = d // 2
    return pl.pallas_call(
        functools.partial(_combine_kernel, seg_args=seg_args, n_rows=rows),
        grid_spec=pltpu.PrefetchScalarGridSpec(
            num_scalar_prefetch=1,
            grid=(rows // COMBINE_ROWS,),
            in_specs=[
                pl.BlockSpec((COMBINE_ROWS, d), lambda i, pos: (i, 0)),
                pl.BlockSpec((COMBINE_ROWS, 2), lambda i, pos: (i, 0)),
                pl.BlockSpec((None, MOD_ROWS, d), lambda i, pos: (layer, 0, 5)),
                pl.BlockSpec(memory_space=pl.ANY),
            ],
            out_specs=pl.BlockSpec((COMBINE_ROWS, d), lambda i, pos: (i, 0)),
            scratch_shapes=[pltpu.VMEM((2, 2, COMBINE_ROWS, half), jnp.uint32), pltpu.SemaphoreType.DMA((2,))],
        ),
        out_shape=jax.ShapeDtypeStruct((rows, d), F32),
        compiler_params=_cparams(("arbitrary",)),
        name="moe_combine",
    )(pos, xs, wts_t, mod, ys)


def kernel(x, c, ctx, c_ctx, w_ada, b_ada, g_mix, g_ffn, w_in, b_in, g_q, g_k, sink, conv_w, conv_b, g_mh,
           w_br_attn, w_br_mlstm, w_out, w_router, b_router, w_gate, w_up, w_down):
    n_batch, lat_len, d = x.shape
    ctx_len = ctx.shape[1]
    depth = w_ada.shape[0]
    d_in = w_in.shape[2]
    aw = w_br_attn.shape[1]
    mw = w_br_mlstm.shape[1]
    mqk = conv_w.shape[2] // 2
    n_mh = mw // MLSTM_V_DIM
    kvw = (d_in - aw - 2 * mqk - 2 * mw - 4 * n_mh - 2 * d) // 2
    n_experts = w_router.shape[1]
    assert n_batch + 1 <= MOD_ROWS and 2 * n_mh <= LANES

    n_lat_rows = n_batch * lat_len
    n_ctx_rows = n_batch * ctx_len
    tm = _pick(n_ctx_rows, (512, 256))
    assert lat_len % tm == 0
    seg_args = (n_lat_rows // tm, lat_len // tm, n_batch)
    tm_e = 256

    o_aq = 0
    o_ak = o_aq + aw
    o_av = o_ak + kvw
    o_mq = o_av + kvw
    o_mk = o_mq + mqk
    o_mv = o_mk + mqk
    o_mo = o_mv + mw
    o_g = o_mo + mw
    o_ga = o_g + 4 * n_mh
    o_gm = o_ga + d
    order = [(0, o_g), (o_ga, 2 * d)]
    n_ak, n_av, n_mq, n_mv, n_mo, n_ga, n_gm = o_ak, o_av, o_mq, o_mv, o_mo, o_g, o_g + d

    xs = jnp.concatenate([x.reshape(n_lat_rows, d), ctx.reshape(n_ctx_rows, d)], axis=0)
    cvec = jnp.zeros((MOD_ROWS, d), F32).at[:n_batch].set(c).at[n_batch].set(c_ctx)
    mod = _adaln(cvec, w_ada, b_ada)
    tabs = _rope_tables(lat_len)
    w_router_t = w_router.T

    w_main = _wprep(w_in, o_g, 4 * n_mh)
    b_main = jnp.concatenate([b_in[:, o:o + w] for o, w in order], axis=1).reshape(depth, 1, -1)
    wg = jnp.zeros((depth, 2, d, LANES), F32)
    bg = jnp.zeros((depth, 2, 1, LANES), F32)
    for dr in range(2):
        gsl = slice(o_g + 2 * n_mh * dr, o_g + 2 * n_mh * (dr + 1))
        wg = wg.at[:, dr, :, :2 * n_mh].set(w_in[:, :, gsl])
        bg = bg.at[:, dr, 0, :2 * n_mh].set(b_in[:, gsl])
    wg = wg.astype(BF16)
    wa_b, wm_b, wo_b = w_br_attn.astype(BF16), w_br_mlstm.astype(BF16), w_out.astype(BF16)
    wgate_b, wup_b, wdown_b = w_gate.astype(BF16), w_up.astype(BF16), w_down.astype(BF16)

    n_tiles = (TOP_K * (n_lat_rows + n_ctx_rows)) // tm_e + n_experts
    hs = None
    for l in range(depth):
        need_ctx = l < depth - 1
        rows = n_lat_rows + (n_ctx_rows if need_ctx else 0)

        proj, gates = _inproj(xs, g_mix[l].reshape(1, d), mod, l, w_main, b_main, wg, bg, tm, seg_args)
        qk = _conv(proj, conv_w[l], conv_b[l], n_mq, n_lat_rows, lat_len, ctx_len)
        hfb = _mlstm(qk, proj, gates, mqk, mw, n_mv, n_batch, lat_len, ctx_len)
        qr, kr = _rope(proj, tabs, g_q[l], g_k[l], aw, kvw, n_ak, n_lat_rows)
        attn = _attention(sink[l], qr, kr, proj, aw, kvw, n_av, n_batch, lat_len, ctx_len, need_ctx)
        u = _branch(attn, hfb, proj, g_mh[l], wa_b, wm_b, l, n_mo, n_ga, n_gm, rows, tm)
        xs = _outproj(u, wo_b, xs, mod, l, tm, seg_args)

        hp, idx, wts = _router(xs, g_ffn[l].reshape(1, d), mod, l, w_router_t, b_router, rows, ROUTER_ROWS,
                               (n_lat_rows // ROUTER_ROWS, lat_len // ROUTER_ROWS, n_batch))
        pos, tile_expert, n_used = _route(idx, n_experts, tm_e, n_tiles)
        hs = _dispatch(pos, hp, jnp.zeros((n_tiles * tm_e, d // 2), jnp.uint32) if hs is None else hs)
        ys = _experts(tile_expert, n_used, hs, wgate_b, wup_b, wdown_b, l, tm_e)
        xs = _combine(pos, xs, wts.T, mod, l, ys, rows, (n_lat_rows // COMBINE_ROWS, lat_len // COMBINE_ROWS, n_batch))

    return xs[:n_lat_rows].reshape(n_batch, lat_len, d)
```
